```python
import math
import jax
import jax.numpy as jnp
from jax import lax
import numpy as np


D_MODEL = 1024
BATCH = 8
SEQ = 2048
DEPTH = 2

GRID_W = 64
CTX_LEN = 256
N_HEADS_TOTAL = 16
HEAD_DIM = D_MODEL // N_HEADS_TOTAL
N_HEADS_A = 4
N_KV_A = 2
N_HEADS_B = 4
DIFF_DIM = HEAD_DIM // 2
N_HEADS_C = 4
N_KV_C = 2
N_HEADS_D = 4
MIX_WIDTH = (N_HEADS_A + N_HEADS_B + N_HEADS_C + N_HEADS_D) * HEAD_DIM
Q_BLOCK = 128
WINDOW = 128
NA_ROWS = 8
NA_COLS = 16
ROPE_BASE = 10000.0
N_EXPERTS = 16
D_EXPERT = 1024
CAPACITY_FACTOR = 2
EPS = 1e-6
NEG_INF = -1e30
PROJ_WIDTHS = (
    N_HEADS_A * HEAD_DIM, N_KV_A * HEAD_DIM, N_KV_A * HEAD_DIM,
    N_HEADS_B * HEAD_DIM, N_HEADS_B * HEAD_DIM, N_HEADS_B * HEAD_DIM,
    N_HEADS_C * HEAD_DIM, N_KV_C * HEAD_DIM, N_KV_C * HEAD_DIM,
    N_HEADS_D * HEAD_DIM, N_HEADS_D * HEAD_DIM, N_HEADS_D * HEAD_DIM,
)
PROJ_DIM = sum(PROJ_WIDTHS)
SPLIT_POINTS = tuple(int(s) for s in np.cumsum(PROJ_WIDTHS)[:-1])

kernel_name = 'hybrid_dit_parallel_heads_ec_moe'


def rms_norm(x, gain):
    xf = x.astype(jnp.float32)
    y = xf * lax.rsqrt(jnp.mean(xf * xf, axis=-1, keepdims=True) + EPS)
    return (y * gain.astype(jnp.float32)).astype(x.dtype)


def modulate(h, shift, scale):
    return h * (1.0 + scale) + shift


def axial_rope_tables(n_tokens, dim, dtype):
    t = jnp.arange(n_tokens, dtype=jnp.int32)
    rows = (t // GRID_W).astype(jnp.float32)
    cols = (t % GRID_W).astype(jnp.float32)
    n_axis = dim // 4
    inv_freq = ROPE_BASE ** (-jnp.arange(n_axis, dtype=jnp.float32) / n_axis)
    ang = jnp.concatenate([rows[:, None] * inv_freq, cols[:, None] * inv_freq], axis=-1)
    return jnp.cos(ang).astype(dtype), jnp.sin(ang).astype(dtype)


def apply_rope(x, cos, sin):
    shape = (cos.shape[0],) + (1,) * (x.ndim - 3) + (cos.shape[1],)
    cos = cos.reshape(shape)
    sin = sin.reshape(shape)
    xr = x.reshape(x.shape[:-1] + (x.shape[-1] // 2, 2))
    x0, x1 = xr[..., 0], xr[..., 1]
    return jnp.stack([x0 * cos - x1 * sin, x0 * sin + x1 * cos], axis=-1).reshape(x.shape)


def gqa_dense(q, k, v):
    s = jnp.einsum('bqkgd,bskd->bkgqs', q, k).astype(jnp.float32) * (HEAD_DIM ** -0.5)
    p = jax.nn.softmax(s, axis=-1).astype(v.dtype)
    o = jnp.einsum('bkgqs,bskd->bqkgd', p, v)
    return o.reshape(o.shape[0], o.shape[1], -1)


def mixer_axial_gqa(q, k, v, qc, kc, vc, q_gain, k_gain, cos, sin, with_ctx):
    B, L, _ = q.shape
    Lc = kc.shape[1]
    G = N_HEADS_A // N_KV_A
    q = apply_rope(rms_norm(q.reshape(B, L, N_KV_A, G, HEAD_DIM), q_gain), cos, sin)
    k = apply_rope(rms_norm(k.reshape(B, L, N_KV_A, HEAD_DIM), k_gain), cos, sin)
    v = v.reshape(B, L, N_KV_A, HEAD_DIM)
    kc = rms_norm(kc.reshape(B, Lc, N_KV_A, HEAD_DIM), k_gain)
    vc = vc.reshape(B, Lc, N_KV_A, HEAD_DIM)
    k_all = jnp.concatenate([kc, k], axis=1)
    v_all = jnp.concatenate([vc, v], axis=1)
    q_blocks = jnp.moveaxis(q.reshape(B, L // Q_BLOCK, Q_BLOCK, N_KV_A, G, HEAD_DIM), 1, 0)
    out = lax.map(lambda qb: gqa_dense(qb, k_all, v_all), q_blocks)
    out = jnp.moveaxis(out, 0, 1).reshape(B, L, N_HEADS_A * HEAD_DIM)
    out_c = None
    if with_ctx:
        qc = rms_norm(qc.reshape(B, Lc, N_KV_A, G, HEAD_DIM), q_gain)
        out_c = gqa_dense(qc, kc, vc)
    return out, out_c


def mixer_differential(q, k, v, qc, kc, vc, lam_q1, lam_k1, lam_q2, lam_k2, subln_gain, lam_init,
                       cos, sin, with_ctx):
    B, L, _ = q.shape
    Lc = kc.shape[1]
    H = N_HEADS_B
    q = apply_rope(q.reshape(B, L, H, 2, DIFF_DIM), cos, sin)
    k = apply_rope(k.reshape(B, L, H, 2, DIFF_DIM), cos, sin)
    v = v.reshape(B, L, H, HEAD_DIM)
    kc = kc.reshape(B, Lc, H, 2, DIFF_DIM)
    vc = vc.reshape(B, Lc, H, HEAD_DIM)
    lam = (jnp.exp(jnp.sum(lam_q1.astype(jnp.float32) * lam_k1.astype(jnp.float32)))
           - jnp.exp(jnp.sum(lam_q2.astype(jnp.float32) * lam_k2.astype(jnp.float32))) + lam_init)

    def diff_attend(qb, kk, vv):
        s = jnp.einsum('bqhcd,bshcd->bhcqs', qb, kk).astype(jnp.float32) * (DIFF_DIM ** -0.5)
        p = jax.nn.softmax(s, axis=-1)
        a = (p[:, :, 0] - lam * p[:, :, 1]).astype(vv.dtype)
        o = jnp.einsum('bhqs,bshd->bqhd', a, vv)
        o = rms_norm(o, subln_gain) * (1.0 - lam_init)
        return o.reshape(o.shape[0], o.shape[1], H * HEAD_DIM)

    k_all = jnp.concatenate([kc, k], axis=1)
    v_all = jnp.concatenate([vc, v], axis=1)
    q_blocks = jnp.moveaxis(q.reshape(B, L // Q_BLOCK, Q_BLOCK, H, 2, DIFF_DIM), 1, 0)
    out = lax.map(lambda qb: diff_attend(qb, k_all, v_all), q_blocks)
    out = jnp.moveaxis(out, 0, 1).reshape(B, L, H * HEAD_DIM)
    out_c = None
    if with_ctx:
        out_c = diff_attend(qc.reshape(B, Lc, H, 2, DIFF_DIM), kc, vc)
    return out, out_c


def mixer_window_sink(q, k, v, qc, kc, vc, sink, cos, sin, with_ctx):
    B, L, _ = q.shape
    Lc = kc.shape[1]
    G = N_HEADS_C // N_KV_C
    nb = L // Q_BLOCK
    scale = HEAD_DIM ** -0.5
    q = apply_rope(q.reshape(B, L, N_KV_C, G, HEAD_DIM), cos, sin)
    k = apply_rope(k.reshape(B, L, N_KV_C, HEAD_DIM), cos, sin)
    v = v.reshape(B, L, N_KV_C, HEAD_DIM)
    kc = kc.reshape(B, Lc, N_KV_C, HEAD_DIM)
    vc = vc.reshape(B, Lc, N_KV_C, HEAD_DIM)
    sink_kg = sink.reshape(N_KV_C, G).astype(jnp.float32)

    def band(t):
        tp = jnp.pad(t, ((0, 0), (Q_BLOCK, Q_BLOCK), (0, 0), (0, 0)))
        tp = tp.reshape(B, nb + 2, Q_BLOCK, N_KV_C, HEAD_DIM)
        return jnp.concatenate([tp[:, :-2], tp[:, 1:-1], tp[:, 2:]], axis=2)

    kb, vb = band(k), band(v)
    qb = q.reshape(B, nb, Q_BLOCK, N_KV_C, G, HEAD_DIM)
    s_band = jnp.einsum('bnqkgd,bnskd->bnkgqs', qb, kb).astype(jnp.float32) * scale
    s_ctx = jnp.einsum('bnqkgd,bskd->bnkgqs', qb, kc).astype(jnp.float32) * scale
    blk = jnp.arange(nb)[:, None, None] * Q_BLOCK
    qpos = blk + jnp.arange(Q_BLOCK)[None, :, None]
    kpos = blk - Q_BLOCK + jnp.arange(3 * Q_BLOCK)[None, None, :]
    valid = (jnp.abs(kpos - qpos) <= WINDOW) & (kpos >= 0) & (kpos < L)
    s_band = jnp.where(valid[None, :, None, None], s_band, NEG_INF)
    s_sink = jnp.broadcast_to(sink_kg[None, None, :, :, None, None], s_ctx.shape[:-1] + (1,))
    p = jax.nn.softmax(jnp.concatenate([s_sink, s_ctx, s_band], axis=-1), axis=-1)
    p_ctx = p[..., 1:1 + Lc].astype(v.dtype)
    p_band = p[..., 1 + Lc:].astype(v.dtype)
    o = (jnp.einsum('bnkgqs,bskd->bnqkgd', p_ctx, vc)
         + jnp.einsum('bnkgqs,bnskd->bnqkgd', p_band, vb))
    out = o.reshape(B, L, N_HEADS_C * HEAD_DIM)
    out_c = None
    if with_ctx:
        qcg = qc.reshape(B, Lc, N_KV_C, G, HEAD_DIM)
        s = jnp.einsum('bqkgd,bskd->bkgqs', qcg, kc).astype(jnp.float32) * scale
        s_sink_c = jnp.broadcast_to(sink_kg[None, :, :, None, None], s.shape[:-1] + (1,))
        pc = jax.nn.softmax(jnp.concatenate([s_sink_c, s], axis=-1), axis=-1)[..., 1:].astype(v.dtype)
        out_c = jnp.einsum('bkgqs,bskd->bqkgd', pc, vc).reshape(B, Lc, N_HEADS_C * HEAD_DIM)
    return out, out_c


def mixer_neighbourhood(q, k, v, qc, kc, vc, rpb, with_ctx):
    B, L, _ = q.shape
    Lc = kc.shape[1]
    H = N_HEADS_D
    rows = L // GRID_W
    kr = min(NA_ROWS, rows)
    scale = HEAD_DIM ** -0.5
    qg = q.reshape(B, rows, GRID_W, H, HEAD_DIM)
    kg = k.reshape(B, rows, GRID_W, H, HEAD_DIM)
    vg = v.reshape(B, rows, GRID_W, H, HEAD_DIM)
    kc = kc.reshape(B, Lc, H, HEAD_DIM)
    vc = vc.reshape(B, Lc, H, HEAD_DIM)
    r = jnp.arange(rows)
    row_start = jnp.clip(r - kr // 2, 0, rows - kr)
    row_idx = row_start[:, None] + jnp.arange(kr)[None, :]
    k_strip = kg[:, row_idx]
    v_strip = vg[:, row_idx]
    cq = jnp.arange(GRID_W)
    col_start = jnp.clip(cq - NA_COLS // 2, 0, GRID_W - NA_COLS)
    col_valid = (cq[None, :] >= col_start[:, None]) & (cq[None, :] < col_start[:, None] + NA_COLS)
    ri = row_idx - r[:, None] + NA_ROWS - 1
    ci = jnp.clip(cq[None, :] - cq[:, None] + NA_COLS - 1, 0, 2 * NA_COLS - 2)
    bias = rpb[:, ri[:, None, :, None], ci[None, :, None, :]]
    bias = jnp.moveaxis(bias, 0, 1).astype(jnp.float32)
    s = jnp.einsum('brqhd,brkwhd->brhqkw', qg, k_strip).astype(jnp.float32) * scale + bias[None]
    s = jnp.where(col_valid[:, None, :], s, NEG_INF)
    s = s.reshape(B, rows, H, GRID_W, kr * GRID_W)
    s_ctx = jnp.einsum('brqhd,bshd->brhqs', qg, kc).astype(jnp.float32) * scale
    p = jax.nn.softmax(jnp.concatenate([s_ctx, s], axis=-1), axis=-1)
    p_ctx = p[..., :Lc].astype(v.dtype)
    p_nb = p[..., Lc:].reshape(B, rows, H, GRID_W, kr, GRID_W).astype(v.dtype)
    o = (jnp.einsum('brhqkw,brkwhd->brqhd', p_nb, v_strip)
         + jnp.einsum('brhqs,bshd->brqhd', p_ctx, vc))
    out = o.reshape(B, L, H * HEAD_DIM)
    out_c = None
    if with_ctx:
        qcg = qc.reshape(B, Lc, H, HEAD_DIM)
        sc = jnp.einsum('bqhd,bshd->bhqs', qcg, kc).astype(jnp.float32) * scale
        pc = jax.nn.softmax(sc, axis=-1).astype(v.dtype)
        out_c = jnp.einsum('bhqs,bshd->bqhd', pc, vc).reshape(B, Lc, H * HEAD_DIM)
    return out, out_c


def expert_choice_ffn(h, w_router, w_gate, w_up, w_down):
    B, T, D = h.shape
    cap = CAPACITY_FACTOR * T // N_EXPERTS
    aff = jax.nn.softmax(jnp.einsum('btd,de->bte', h, w_router).astype(jnp.float32), axis=-1)
    g, idx = lax.top_k(jnp.swapaxes(aff, 1, 2), cap)
    xe = jax.vmap(lambda hb, ib: hb[ib])(h, idx)
    a = jnp.einsum('becd,edf->becf', xe, w_gate)
    u = jnp.einsum('becd,edf->becf', xe, w_up)
    y = jnp.einsum('becf,efd->becd', jax.nn.silu(a) * u, w_down) * g[..., None].astype(h.dtype)
    return jax.vmap(lambda yb, ib: jnp.zeros((T, D), yb.dtype).at[ib.reshape(-1)].add(yb.reshape(-1, D)))(y, idx)


def setup_inputs(seed: int = 0) -> dict:
    key = jax.random.key(seed)
    ks = jax.random.split(key, 26)
    f32 = jnp.float32
    nrm = lambda k, shape, s: jax.random.normal(k, shape, f32) * s
    return {
        'x': nrm(ks[0], (BATCH, SEQ, D_MODEL), 1.0),
        'c': nrm(ks[1], (BATCH, D_MODEL), 1.0),
        'ctx': nrm(ks[2], (BATCH, CTX_LEN, D_MODEL), 1.0),
        'c_ctx': nrm(ks[3], (D_MODEL,), 1.0),
        'w_ada': nrm(ks[4], (DEPTH, D_MODEL, 6 * D_MODEL), 0.5 * D_MODEL ** -0.5),
        'b_ada': nrm(ks[5], (DEPTH, 6 * D_MODEL), 0.02),
        'g_pre_mix': 1.0 + nrm(ks[6], (DEPTH, D_MODEL), 0.05),
        'g_post_mix': 1.0 + nrm(ks[7], (DEPTH, D_MODEL), 0.05),
        'g_pre_ffn': 1.0 + nrm(ks[8], (DEPTH, D_MODEL), 0.05),
        'g_post_ffn': 1.0 + nrm(ks[9], (DEPTH, D_MODEL), 0.05),
        'w_in': nrm(ks[10], (DEPTH, D_MODEL, PROJ_DIM), D_MODEL ** -0.5),
        'w_out': nrm(ks[11], (DEPTH, MIX_WIDTH, D_MODEL), MIX_WIDTH ** -0.5),
        'q_gain_a': 1.0 + nrm(ks[12], (DEPTH, HEAD_DIM), 0.05),
        'k_gain_a': 1.0 + nrm(ks[13], (DEPTH, HEAD_DIM), 0.05),
        'lam_q1': nrm(ks[14], (DEPTH, DIFF_DIM), 0.1),
        'lam_k1': nrm(ks[15], (DEPTH, DIFF_DIM), 0.1),
        'lam_q2': nrm(ks[16], (DEPTH, DIFF_DIM), 0.1),
        'lam_k2': nrm(ks[17], (DEPTH, DIFF_DIM), 0.1),
        'subln_gain_b': 1.0 + nrm(ks[18], (DEPTH, HEAD_DIM), 0.05),
        'sink_c': nrm(ks[19], (DEPTH, N_HEADS_C), 0.5),
        'rpb_d': nrm(ks[20], (DEPTH, N_HEADS_D, 2 * NA_ROWS - 1, 2 * NA_COLS - 1), 0.5),
        'w_router': nrm(ks[21], (DEPTH, D_MODEL, N_EXPERTS), D_MODEL ** -0.5),
        'w_gate': nrm(ks[22], (DEPTH, N_EXPERTS, D_MODEL, D_EXPERT), D_MODEL ** -0.5),
        'w_up': nrm(ks[23], (DEPTH, N_EXPERTS, D_MODEL, D_EXPERT), D_MODEL ** -0.5),
        'w_down': nrm(ks[24], (DEPTH, N_EXPERTS, D_EXPERT, D_MODEL), D_EXPERT ** -0.5),
    }


def reference(x, c, ctx, c_ctx, w_ada, b_ada, g_pre_mix, g_post_mix, g_pre_ffn, g_post_ffn,
              w_in, w_out, q_gain_a, k_gain_a, lam_q1, lam_k1, lam_q2, lam_k2, subln_gain_b,
              sink_c, rpb_d, w_router, w_gate, w_up, w_down):
    B, L, D = x.shape
    cos_a, sin_a = axial_rope_tables(L, HEAD_DIM, x.dtype)
    cos_b, sin_b = axial_rope_tables(L, DIFF_DIM, x.dtype)
    xc = ctx
    for l in range(DEPTH):
        with_ctx = l < DEPTH - 1
        lam_init = 0.8 - 0.6 * math.exp(-0.3 * l)
        mod = (jax.nn.silu(c) @ w_ada[l] + b_ada[l])[:, None, :]
        mod_c = jax.nn.silu(c_ctx) @ w_ada[l] + b_ada[l]
        sh1, sc1, gt1, sh2, sc2, gt2 = jnp.split(mod, 6, axis=-1)
        sh1c, sc1c, gt1c, sh2c, sc2c, gt2c = jnp.split(mod_c, 6, axis=-1)

        h = modulate(rms_norm(x, g_pre_mix[l]), sh1, sc1)
        hc = modulate(rms_norm(xc, g_pre_mix[l]), sh1c, sc1c)
        p = jnp.split(h @ w_in[l], SPLIT_POINTS, axis=-1)
        pc = jnp.split(hc @ w_in[l], SPLIT_POINTS, axis=-1)
        oa, oa_c = mixer_axial_gqa(p[0], p[1], p[2], pc[0], pc[1], pc[2],
                                   q_gain_a[l], k_gain_a[l], cos_a, sin_a, with_ctx)
        ob, ob_c = mixer_differential(p[3], p[4], p[5], pc[3], pc[4], pc[5],
                                      lam_q1[l], lam_k1[l], lam_q2[l], lam_k2[l], subln_gain_b[l],
                                      lam_init, cos_b, sin_b, with_ctx)
        oc, oc_c = mixer_window_sink(p[6], p[7], p[8], pc[6], pc[7], pc[8],
                                     sink_c[l], cos_a, sin_a, with_ctx)
        od, od_c = mixer_neighbourhood(p[9], p[10], p[11], pc[9], pc[10], pc[11], rpb_d[l], with_ctx)
        o = jnp.concatenate([oa, ob, oc, od], axis=-1) @ w_out[l]
        x = x + gt1 * rms_norm(o, g_post_mix[l])
        if with_ctx:
            o_c = jnp.concatenate([oa_c, ob_c, oc_c, od_c], axis=-1) @ w_out[l]
            xc = xc + gt1c * rms_norm(o_c, g_post_mix[l])

        h = modulate(rms_norm(x, g_pre_ffn[l]), sh2, sc2)
        y = expert_choice_ffn(h, w_router[l], w_gate[l], w_up[l], w_down[l])
        x = x + gt2 * rms_norm(y, g_post_ffn[l])
        if with_ctx:
            hc = modulate(rms_norm(xc, g_pre_ffn[l]), sh2c, sc2c)
            yc = expert_choice_ffn(hc, w_router[l], w_gate[l], w_up[l], w_down[l])
            xc = xc + gt2c * rms_norm(yc, g_post_ffn[l])
    return x
```

```python
import functools
import math

import numpy as np
import jax
import jax.numpy as jnp
from jax import lax
from jax.experimental import pallas as pl
from jax.experimental.pallas import tpu as pltpu

F32 = jnp.float32
BF16 = jnp.bfloat16

D_MODEL = 1024
GRID_W = 64
HEAD_DIM = 64
DIFF_DIM = 32
N_HEADS = 4
MIXER_W = N_HEADS * HEAD_DIM
QKV_W = 4 * 3 * MIXER_W
WINDOW = 128
NA_ROWS = 8
NA_COLS = 16
ROPE_BASE = 10000.0
N_EXPERTS = 16
CAPACITY_FACTOR = 2
EPS = 1e-6
NEG_INF = -1e30
PROJ_WIDTHS = (256, 128, 128, 256, 256, 256, 256, 128, 128, 256, 256, 256)
LANES = 128
TOK_TILE = 256
VMEM_LIMIT = 56 * 1024 * 1024
BISECT_MAX_STEPS = 192


def _cparams(n_axes):
    return pltpu.CompilerParams(dimension_semantics=("arbitrary",) * n_axes,
                                vmem_limit_bytes=VMEM_LIMIT)


def _permute_w_in(w):
    d = w.shape[0]
    offs = np.concatenate([[0], np.cumsum(PROJ_WIDTHS)])

    def pairs(a, n_slots, n_pairs, rep=1):
        a = a.reshape(d, n_slots // rep, n_pairs, 2)
        if rep > 1:
            a = jnp.repeat(a, rep, axis=1)
        return a.transpose(0, 3, 1, 2).reshape(d, MIXER_W)

    out = []
    for m in range(4):
        q, k, v = (w[:, offs[3 * m + j]:offs[3 * m + j + 1]] for j in range(3))
        if m in (0, 2):
            v = jnp.repeat(v.reshape(d, 2, HEAD_DIM), 2, axis=1).reshape(d, MIXER_W)
            out += [pairs(q, 4, 32), pairs(k, 4, 32, rep=2), v]
        elif m == 1:
            out += [pairs(q, 8, 16), pairs(k, 8, 16), v]
        else:
            out += [pairs(q, 4, 32), pairs(k, 4, 32), v]
    return jnp.concatenate(out, axis=1)


def _rope_tables(n_tokens, dim):
    t = jnp.arange(n_tokens, dtype=jnp.int32)
    rows = (t // GRID_W).astype(F32)
    cols = (t % GRID_W).astype(F32)
    n_axis = dim // 4
    inv_freq = ROPE_BASE ** (-jnp.arange(n_axis, dtype=F32) / n_axis)
    ang = jnp.concatenate([rows[:, None] * inv_freq, cols[:, None] * inv_freq], axis=-1)
    return jnp.cos(ang), jnp.sin(ang)


def _rope_lane_tables(n_ctx, n_lat, dim):
    cos, sin = _rope_tables(n_lat, dim)
    reps = LANES // cos.shape[1]
    cos = jnp.concatenate([jnp.ones((n_ctx, LANES), F32), jnp.tile(cos, (1, reps))], axis=0)
    sin = jnp.concatenate([jnp.zeros((n_ctx, LANES), F32), jnp.tile(sin, (1, reps))], axis=0)
    return cos, sin


def _neighbourhood_bias(rpb, n_ctx):
    cq = np.arange(GRID_W)
    col_start = np.clip(cq - NA_COLS // 2, 0, GRID_W - NA_COLS)
    col_valid = (cq[None, :] >= col_start[:, None]) & (cq[None, :] < col_start[:, None] + NA_COLS)
    ci = np.clip(cq[None, :] - cq[:, None] + NA_COLS - 1, 0, 2 * NA_COLS - 2)
    ri = np.arange(NA_ROWS)[None, :] - np.arange(NA_ROWS)[:, None] + NA_ROWS - 1
    bias = rpb[:, ri[:, None, :, None], ci[None, :, None, :]]
    bias = jnp.where(col_valid[None, None, :, None, :], bias.astype(F32), NEG_INF)
    bias = bias.reshape(bias.shape[0], NA_ROWS, GRID_W, NA_ROWS * GRID_W)
    return jnp.concatenate([jnp.zeros(bias.shape[:3] + (n_ctx,), F32), bias], axis=-1)


def _block_diag_ones(n, group):
    r = lax.broadcasted_iota(jnp.int32, (n, n), 0) // group
    c = lax.broadcasted_iota(jnp.int32, (n, n), 1) // group
    return jnp.where(r == c, 1.0, 0.0).astype(BF16)


def _dot(a, b):
    return jnp.dot(a, b, preferred_element_type=F32)


def _dot_nt(a, b):
    return lax.dot_general(a, b, (((1,), (1,)), ((), ())), preferred_element_type=F32)


def _split(a):
    hi = a.astype(BF16)
    return hi, (a - hi.astype(F32)).astype(BF16)


def _dot_hilo(a, b_bf16):
    hi, lo = _split(a)
    return _dot(hi, b_bf16) + _dot(lo, b_bf16)


def _lane_mask(cond):
    return jnp.where(cond, 1.0, 0.0).astype(BF16)


def _norm_modulate(x, gain, shift, scale):
    ms = jnp.mean(x * x, axis=-1, keepdims=True)
    h = (x * lax.rsqrt(ms + EPS)) * gain
    return h * (1.0 + scale) + shift


def _mod_spec(chunk, row0, tile):
    n_ctx_tiles = TOK_TILE // tile
    return pl.BlockSpec((1, 1, 1, 1, D_MODEL),
                        lambda b, i: (b, jnp.minimum((i + row0) // n_ctx_tiles, 1), chunk, 0, 0))


def _ada_body(c_ref, w_ref, b_ref, o_ref):
    c = c_ref[...]
    cs = (c * jax.nn.sigmoid(c)).astype(BF16)
    o_ref[0] = _dot(cs, w_ref[0].astype(BF16)) + b_ref[0]


def _ada(cc, w_ada, b_ada):
    depth, d, n = w_ada.shape
    tn = 512
    return pl.pallas_call(
        _ada_body,
        grid=(depth, n // tn),
        in_specs=[pl.BlockSpec(cc.shape, lambda l, j: (0, 0)),
                  pl.BlockSpec((1, d, tn), lambda l, j: (l, 0, j)),
                  pl.BlockSpec((1, 1, tn), lambda l, j: (l, 0, j))],
        out_specs=pl.BlockSpec((1, cc.shape[0], tn), lambda l, j: (l, 0, j)),
        out_shape=jax.ShapeDtypeStruct((depth, cc.shape[0], n), F32),
        compiler_params=_cparams(2),
        name="ada",
    )(cc, w_ada, b_ada.reshape(depth, 1, n))


def _proj_body(x_ref, sh_ref, sc_ref, g_ref, w_ref, ca_ref, sa_ref, cb_ref, sb_ref, gq_ref, gk_ref,
               o_ref):
    h = _norm_modulate(x_ref[0], g_ref[...], sh_ref[0, 0, 0], sc_ref[0, 0, 0])
    acc = _dot(h.astype(BF16), w_ref[...])
    ca, sa, cb, sb = ca_ref[...], sa_ref[...], cb_ref[...], sb_ref[...]
    bd = _block_diag_ones(LANES, 32)

    def get(m, j):
        c0 = (3 * m) * MIXER_W + j * LANES
        return acc[:, c0:c0 + LANES]

    def put(m, j, v):
        c0 = (3 * m) * MIXER_W + j * LANES
        o_ref[0, :, c0:c0 + LANES] = v.astype(BF16)

    def rope(x0, x1, c, s):
        return x0 * c - x1 * s, x0 * s + x1 * c

    def head_norm(x0, x1, g_ref_):
        gs = _dot_hilo(x0 * x0 + x1 * x1, bd)
        r = lax.rsqrt(gs * (1.0 / HEAD_DIM) + EPS)
        return x0 * r * g_ref_[0:1, :], x1 * r * g_ref_[1:2, :]

    for m in range(4):
        q0, q1, k0, k1 = get(m, 0), get(m, 1), get(m, 2), get(m, 3)
        if m == 0:
            q0, q1 = head_norm(q0, q1, gq_ref)
            k0, k1 = head_norm(k0, k1, gk_ref)
        if m in (0, 2):
            q0, q1 = rope(q0, q1, ca, sa)
            k0, k1 = rope(k0, k1, ca, sa)
        elif m == 1:
            q0, q1 = rope(q0, q1, cb, sb)
            k0, k1 = rope(k0, k1, cb, sb)
        qscale = (DIFF_DIM if m == 1 else HEAD_DIM) ** -0.5
        put(m, 0, q0 * qscale)
        put(m, 1, q1 * qscale)
        put(m, 2, k0)
        put(m, 3, k1)
        put(m, 4, get(m, 4))
        put(m, 5, get(m, 5))


def _proj(xall, mod, gain, w_perm, ca, sa, cb, sb, gq, gk):
    b, lt, d = xall.shape
    tm = TOK_TILE
    tab = pl.BlockSpec((tm, LANES), lambda b_, i: (i, 0))
    full = lambda a: pl.BlockSpec(a.shape, lambda b_, i: (0,) * a.ndim)
    return pl.pallas_call(
        _proj_body,
        grid=(b, lt // tm),
        in_specs=[pl.BlockSpec((1, tm, d), lambda b_, i: (b_, i, 0)),
                  _mod_spec(0, 0, tm), _mod_spec(1, 0, tm),
                  full(gain), full(w_perm), tab, tab, tab, tab, full(gq), full(gk)],
        out_specs=pl.BlockSpec((1, tm, QKV_W), lambda b_, i: (b_, i, 0)),
        out_shape=jax.ShapeDtypeStruct((b, lt, QKV_W), BF16),
        compiler_params=_cparams(2),
        name="proj",
    )(xall, mod, mod, gain, w_perm, ca, sa, cb, sb, gq, gk)


def _softmax_parts(s, extra_logit=None):
    m = jnp.max(s, axis=-1, keepdims=True)
    if extra_logit is not None:
        m = jnp.maximum(m, extra_logit)
    p = jnp.exp(s - m)
    l = jnp.sum(p, axis=-1, keepdims=True)
    if extra_logit is not None:
        l = l + jnp.exp(extra_logit - m)
    return p, l


def _attn_global_body(*refs, row0, diff, lam_init, n_ctx):
    if diff:
        lam_ref, q_ref, k_ref, v_ref, gain_ref, o_ref = refs
    else:
        q_ref, k_ref, v_ref, o_ref = refs
    i = pl.program_id(1) + row0
    lane = lax.broadcasted_iota(jnp.int32, (1, MIXER_W), 1)

    def run(nk):
        q = q_ref[0]
        k = k_ref[0, :nk, :]
        v = v_ref[0, :nk, :]
        acc = jnp.zeros((q.shape[0], MIXER_W), F32)
        for h in range(N_HEADS):
            vm = v * _lane_mask(lane // HEAD_DIM == h)
            if diff:
                lam = lam_ref[0]
                a = None
                for c in range(2):
                    qm = q * _lane_mask((lane % LANES) // (DIFF_DIM // 2) == 2 * h + c)
                    p, l = _softmax_parts(_dot_nt(qm, k))
                    a = p * (1.0 / l) if c == 0 else a - p * (lam / l)
                acc = acc + _dot(a.astype(BF16), vm)
            else:
                qm = q * _lane_mask((lane % LANES) // (HEAD_DIM // 2) == h)
                p, l = _softmax_parts(_dot_nt(qm, k))
                acc = acc + _dot(p.astype(BF16), vm) * (1.0 / l)
        if diff:
            gs = _dot_hilo(acc * acc, _block_diag_ones(MIXER_W, HEAD_DIM))
            acc = acc * lax.rsqrt(gs * (1.0 / HEAD_DIM) + EPS) * gain_ref[...] * (1.0 - lam_init)
        o_ref[0] = acc.astype(BF16)

    if row0 == 0:
        @pl.when(i == 0)
        def _():
            run(n_ctx)

        @pl.when(i > 0)
        def _():
            run(k_ref.shape[1])
    else:
        run(k_ref.shape[1])


def _attn_global(qkv, mixer, row0, lam=None, gain=None, lam_init=0.0):
    b, lt, _ = qkv.shape
    tq = TOK_TILE
    diff = lam is not None
    kv_spec = lambda j: pl.BlockSpec((1, lt, MIXER_W), lambda b_, i: (b_, 0, 3 * mixer + j))
    in_specs = [pl.BlockSpec((1, tq, MIXER_W), lambda b_, i: (b_, i + row0, 3 * mixer)),
                kv_spec(1), kv_spec(2)]
    args = [qkv, qkv, qkv]
    if diff:
        in_specs = [pl.BlockSpec(memory_space=pltpu.SMEM)] + in_specs + [
            pl.BlockSpec(gain.shape, lambda b_, i: (0, 0))]
        args = [lam] + args + [gain]
    return pl.pallas_call(
        functools.partial(_attn_global_body, row0=row0, diff=diff, lam_init=lam_init, n_ctx=TOK_TILE),
        grid=(b, lt // tq - row0),
        in_specs=in_specs,
        out_specs=pl.BlockSpec((1, tq, MIXER_W), lambda b_, i: (b_, i, 0)),
        out_shape=jax.ShapeDtypeStruct((b, lt - row0 * tq, MIXER_W), BF16),
        compiler_params=_cparams(2),
        name="attn_diff" if diff else "attn_global",
    )(*args)


def _attn_window_body(sink_ref, q_ref, k_ref, v_ref, o_ref, *, row0, n_ctx, n_lat):
    tq = q_ref.shape[1]
    band = tq + 2 * WINDOW
    n_ctx_tiles = n_ctx // tq
    i = pl.program_id(1) + row0
    lane = lax.broadcasted_iota(jnp.int32, (1, MIXER_W), 1)

    def run(is_ctx):
        q = q_ref[0]
        kk = k_ref[0, :n_ctx, :]
        vv = v_ref[0, :n_ctx, :]
        valid = None
        if not is_ctx:
            q_start = (i - n_ctx_tiles) * tq
            k_start = jnp.clip(q_start - WINDOW, 0, n_lat - band)
            ks = pl.multiple_of(n_ctx + k_start, LANES)
            kk = jnp.concatenate([kk, k_ref[0, pl.ds(ks, band), :]], axis=0)
            vv = jnp.concatenate([vv, v_ref[0, pl.ds(ks, band), :]], axis=0)
            col = lax.broadcasted_iota(jnp.int32, (1, n_ctx + band), 1)
            row = lax.broadcasted_iota(jnp.int32, (tq, 1), 0)
            dist = (k_start + col - n_ctx) - (q_start + row)
            valid = jnp.where(col < n_ctx, 0, dist)
            valid = jnp.abs(valid) <= WINDOW
        acc = jnp.zeros((tq, MIXER_W), F32)
        for h in range(N_HEADS):
            qm = q * _lane_mask((lane % LANES) // (HEAD_DIM // 2) == h)
            s = _dot_nt(qm, kk)
            if valid is not None:
                s = jnp.where(valid, s, NEG_INF)
            p, l = _softmax_parts(s, extra_logit=sink_ref[h])
            vm = vv * _lane_mask(lane // HEAD_DIM == h)
            acc = acc + _dot(p.astype(BF16), vm) * (1.0 / l)
        o_ref[0] = acc.astype(BF16)

    if row0 == 0:
        @pl.when(i < n_ctx_tiles)
        def _():
            run(True)

        @pl.when(i >= n_ctx_tiles)
        def _():
            run(False)
    else:
        run(False)


def _attn_window(qkv, mixer, row0_tiles, sink):
    b, lt, _ = qkv.shape
    tq = 128
    row0 = row0_tiles * (TOK_TILE // tq)
    kv_spec = lambda j: pl.BlockSpec((1, lt, MIXER_W), lambda b_, i: (b_, 0, 3 * mixer + j))
    return pl.pallas_call(
        functools.partial(_attn_window_body, row0=row0, n_ctx=TOK_TILE, n_lat=lt - TOK_TILE),
        grid=(b, lt // tq - row0),
        in_specs=[pl.BlockSpec(memory_space=pltpu.SMEM),
                  pl.BlockSpec((1, tq, MIXER_W), lambda b_, i: (b_, i + row0, 3 * mixer)),
                  kv_spec(1), kv_spec(2)],
        out_specs=pl.BlockSpec((1, tq, MIXER_W), lambda b_, i: (b_, i, 0)),
        out_shape=jax.ShapeDtypeStruct((b, lt - row0 * tq, MIXER_W), BF16),
        compiler_params=_cparams(2),
        name="attn_window",
    )(sink, qkv, qkv, qkv)


def _strip_offset(i, n_ctx_tiles, n_rows):
    r = jnp.maximum(i - n_ctx_tiles, 0)
    return r, jnp.clip(r - NA_ROWS // 2, 0, n_rows - NA_ROWS)


def _attn_nbr_body(q_ref, k_ref, v_ref, bias_ref, o_ref, *, row0, n_ctx, n_lat):
    tq = q_ref.shape[1]
    strip = NA_ROWS * GRID_W
    n_ctx_tiles = n_ctx // tq
    i = pl.program_id(1) + row0
    lane = lax.broadcasted_iota(jnp.int32, (1, MIXER_W), 1)

    def run(is_ctx):
        q = q_ref[0]
        kk = k_ref[0, :n_ctx, :]
        vv = v_ref[0, :n_ctx, :]
        if not is_ctx:
            _, rs = _strip_offset(i, n_ctx_tiles, n_lat // GRID_W)
            ks = pl.multiple_of(n_ctx + rs * GRID_W, GRID_W)
            kk = jnp.concatenate([kk, k_ref[0, pl.ds(ks, strip), :]], axis=0)
            vv = jnp.concatenate([vv, v_ref[0, pl.ds(ks, strip), :]], axis=0)
        acc = jnp.zeros((tq, MIXER_W), F32)
        for h in range(N_HEADS):
            qm = q * _lane_mask((lane % LANES) // (HEAD_DIM // 2) == h)
            s = _dot_nt(qm, kk)
            if not is_ctx:
                s = s + bias_ref[h, 0]
            p, l = _softmax_parts(s)
            vm = vv * _lane_mask(lane // HEAD_DIM == h)
            acc = acc + _dot(p.astype(BF16), vm) * (1.0 / l)
        o_ref[0] = acc.astype(BF16)

    if row0 == 0:
        @pl.when(i < n_ctx_tiles)
        def _():
            run(True)

        @pl.when(i >= n_ctx_tiles)
        def _():
            run(False)
    else:
        run(False)


def _attn_nbr(qkv, mixer, row0_tiles, bias):
    b, lt, _ = qkv.shape
    tq = GRID_W
    n_ctx_tiles = TOK_TILE // tq
    row0 = row0_tiles * n_ctx_tiles
    n_rows = (lt - TOK_TILE) // GRID_W
    kv_spec = lambda j: pl.BlockSpec((1, lt, MIXER_W), lambda b_, i: (b_, 0, 3 * mixer + j))

    def bias_map(b_, i):
        r, rs = _strip_offset(i + row0, n_ctx_tiles, n_rows)
        return (0, r - rs, 0, 0)

    return pl.pallas_call(
        functools.partial(_attn_nbr_body, row0=row0, n_ctx=TOK_TILE, n_lat=lt - TOK_TILE),
        grid=(b, lt // tq - row0),
        in_specs=[pl.BlockSpec((1, tq, MIXER_W), lambda b_, i: (b_, i + row0, 3 * mixer)),
                  kv_spec(1), kv_spec(2),
                  pl.BlockSpec((N_HEADS, 1) + bias.shape[2:], bias_map)],
        out_specs=pl.BlockSpec((1, tq, MIXER_W), lambda b_, i: (b_, i, 0)),
        out_shape=jax.ShapeDtypeStruct((b, lt - row0 * tq, MIXER_W), BF16),
        compiler_params=_cparams(2),
        name="attn_nbr",
    )(qkv, qkv, qkv, bias)


def _outproj_body(oa_ref, ob_ref, oc_ref, od_ref, w_ref, x_ref, gate_ref, g_ref, out_ref):
    acc = None
    for m, o_ref in enumerate((oa_ref, ob_ref, oc_ref, od_ref)):
        t = _dot(o_ref[0], w_ref[m * MIXER_W:(m + 1) * MIXER_W, :])
        acc = t if acc is None else acc + t
    ms = jnp.mean(acc * acc, axis=-1, keepdims=True)
    y = (acc * lax.rsqrt(ms + EPS)) * g_ref[...]
    out_ref[0] = x_ref[0] + gate_ref[0, 0, 0] * y


def _outproj(o_mix, w_out, xall, mod, gain, row0):
    b, lt, d = xall.shape
    tm = TOK_TILE
    n_tiles = lt // tm - row0
    o_spec = pl.BlockSpec((1, tm, MIXER_W), lambda b_, i: (b_, i, 0))
    return pl.pallas_call(
        _outproj_body,
        grid=(b, n_tiles),
        in_specs=[o_spec, o_spec, o_spec, o_spec,
                  pl.BlockSpec(w_out.shape, lambda b_, i: (0, 0)),
                  pl.BlockSpec((1, tm, d), lambda b_, i: (b_, i + row0, 0)),
                  _mod_spec(2, row0, tm),
                  pl.BlockSpec(gain.shape, lambda b_, i: (0, 0))],
        out_specs=pl.BlockSpec((1, tm, d), lambda b_, i: (b_, i, 0)),
        out_shape=jax.ShapeDtypeStruct((b, n_tiles * tm, d), F32),
        compiler_params=_cparams(2),
        name="outproj",
    )(*o_mix, w_out, xall, mod, gain)


def _route_body(x_ref, sh_ref, sc_ref, g_ref, wr_ref, h_ref, lg_ref):
    h = _norm_modulate(x_ref[0], g_ref[...], sh_ref[0, 0, 0], sc_ref[0, 0, 0])
    h_ref[0] = h.astype(BF16)
    w_hi, w_lo = _split(wr_ref[...])
    h_hi, h_lo = _split(h)
    lg_ref[0] = _dot_nt(w_hi, h_hi) + (_dot_nt(w_hi, h_lo) + _dot_nt(w_lo, h_hi))


def _route(xs, mod, gain, w_router_t, row0):
    b, n_tok, d = xs.shape
    tm = TOK_TILE
    n_e = w_router_t.shape[0]
    return pl.pallas_call(
        _route_body,
        grid=(b, n_tok // tm),
        in_specs=[pl.BlockSpec((1, tm, d), lambda b_, i: (b_, i, 0)),
                  _mod_spec(3, row0, tm), _mod_spec(4, row0, tm),
                  pl.BlockSpec(gain.shape, lambda b_, i: (0, 0)),
                  pl.BlockSpec(w_router_t.shape, lambda b_, i: (0, 0))],
        out_specs=[pl.BlockSpec((1, tm, d), lambda b_, i: (b_, i, 0)),
                   pl.BlockSpec((1, n_e, tm), lambda b_, i: (b_, 0, i))],
        out_shape=[jax.ShapeDtypeStruct((b, n_tok, d), BF16),
                   jax.ShapeDtypeStruct((b, n_e, n_tok), F32)],
        compiler_params=_cparams(2),
        name="route",
    )(xs, mod, mod, gain, w_router_t)


def _exclusive_cumsum(x, tri):
    off = jnp.zeros((x.shape[0], 1), F32)
    outs = []
    for c in range(x.shape[1] // LANES):
        xc = x[:, c * LANES:(c + 1) * LANES]
        inc = _dot(xc.astype(BF16), tri)
        outs.append(inc - xc + off)
        off = off + inc[:, LANES - 1:LANES]
    return jnp.concatenate(outs, axis=1)


def _select_body(lg_ref, pos_ref, aff_ref, *, segments):
    r = lax.broadcasted_iota(jnp.int32, (LANES, LANES), 0)
    c = lax.broadcasted_iota(jnp.int32, (LANES, LANES), 1)
    tri = jnp.where(r <= c, 1.0, 0.0).astype(BF16)
    for t0, t, cap, slot0 in segments:
        lg = lg_ref[0, :, t0:t0 + t]
        e = jnp.exp(lg - jnp.max(lg, axis=0, keepdims=True))
        aff = e / jnp.sum(e, axis=0, keepdims=True)

        def count_above(thr):
            return jnp.sum(jnp.where(aff > thr, 1.0, 0.0), axis=1, keepdims=True)

        def unsettled(carry):
            return jnp.logical_and(carry[2] > 0.0, carry[3] < BISECT_MAX_STEPS)

        def bisect(carry):
            lo, hi, _, step = carry
            mid = 0.5 * (lo + hi)
            cnt = count_above(mid)
            new_lo = jnp.where(cnt >= cap, mid, lo)
            new_hi = jnp.where(cnt <= cap, mid, hi)
            moving = jnp.logical_and(new_lo < new_hi, jnp.logical_and(mid > lo, mid < hi))
            return new_lo, new_hi, jnp.sum(jnp.where(moving, 1.0, 0.0)), step + 1

        lo0 = jnp.full((aff.shape[0], 1), -1.0, F32)
        hi0 = jnp.max(aff, axis=1, keepdims=True)
        lo, hi, _, _ = lax.while_loop(unsettled, bisect, (lo0, hi0, jnp.float32(1.0), jnp.int32(0)))
        gt = aff > hi
        eq = jnp.logical_and(aff > lo, aff <= hi)
        need = cap - count_above(hi)
        eq_rank = _exclusive_cumsum(jnp.where(eq, 1.0, 0.0), tri)
        sel = jnp.logical_or(gt, jnp.logical_and(eq, eq_rank < need))
        slot = _exclusive_cumsum(jnp.where(sel, 1.0, 0.0), tri) + slot0
        pos_ref[0, :, t0:t0 + t] = jnp.where(sel, slot, -1.0).astype(jnp.int32)
        aff_ref[0, :, t0:t0 + t] = aff


def _select(logits, segments):
    b, n_e, lt = logits.shape
    spec = pl.BlockSpec((1, n_e, lt), lambda b_: (b_, 0, 0))
    return pl.pallas_call(
        functools.partial(_select_body, segments=segments),
        grid=(b,),
        in_specs=[spec],
        out_specs=[spec, spec],
        out_shape=[jax.ShapeDtypeStruct((b, n_e, lt), jnp.int32),
                   jax.ShapeDtypeStruct((b, n_e, lt), F32)],
        compiler_params=_cparams(1),
        name="select",
    )(logits)


def _experts_body(pos_ref, aff_ref, h_ref, wg_ref, wu_ref, wd_ref, y_ref, wg_s, wu_s, wd_s):
    @pl.when(pl.program_id(1) == 0)
    def _():
        wg_s[...] = wg_ref[0, 0].astype(BF16)
        wu_s[...] = wu_ref[0, 0].astype(BF16)
        wd_s[...] = wd_ref[0, 0].astype(BF16)

    n_slots = y_ref.shape[2]
    pos = pos_ref[0, 0]
    hit = lax.broadcasted_iota(jnp.int32, (n_slots, pos.shape[1]), 0) == pos
    xe = _dot(jnp.where(hit, 1.0, 0.0).astype(BF16), h_ref[0]).astype(BF16)
    gate = jnp.sum(jnp.where(hit, aff_ref[0, 0], 0.0), axis=1, keepdims=True)
    a = _dot(xe, wg_s[...])
    u = _dot(xe, wu_s[...])
    act = ((a * jax.nn.sigmoid(a)) * u).astype(BF16)
    y_ref[0, 0] = (_dot(act, wd_s[...]) * gate).astype(BF16)


def _experts(pos, aff, h, w_gate, w_up, w_down, layer, n_slots):
    b, n_e, n_tok = pos.shape
    d, f = w_gate.shape[2:]
    pos4 = pos.reshape(b, n_e, 1, n_tok)
    aff4 = aff.reshape(b, n_e, 1, n_tok)
    row_spec = pl.BlockSpec((1, 1, 1, n_tok), lambda e, b_: (b_, e, 0, 0))
    return pl.pallas_call(
        _experts_body,
        grid=(n_e, b),
        in_specs=[row_spec, row_spec,
                  pl.BlockSpec((1, n_tok, d), lambda e, b_: (b_, 0, 0)),
                  pl.BlockSpec((1, 1, d, f), lambda e, b_: (layer, e, 0, 0)),
                  pl.BlockSpec((1, 1, d, f), lambda e, b_: (layer, e, 0, 0)),
                  pl.BlockSpec((1, 1, f, d), lambda e, b_: (layer, e, 0, 0))],
        out_specs=pl.BlockSpec((1, 1, n_slots, d), lambda e, b_: (b_, e, 0, 0)),
        out_shape=jax.ShapeDtypeStruct((b, n_e, n_slots, d), BF16),
        scratch_shapes=[pltpu.VMEM((d, f), BF16), pltpu.VMEM((d, f), BF16), pltpu.VMEM((f, d), BF16)],
        compiler_params=_cparams(2),
        name="experts",
    )(pos4, aff4, h, w_gate, w_up, w_down)


def _combine_body(pos_ref, y_ref, x_ref, gate_ref, g_ref, out_ref):
    n_e, n_slots = y_ref.shape[1:3]
    pos = pos_ref[0]
    slot = lax.broadcasted_iota(jnp.int32, (n_slots, pos.shape[1]), 0)
    acc = None
    for e in range(n_e):
        onehot = jnp.where(slot == pos[e:e + 1, :], 1.0, 0.0).astype(BF16)
        t = lax.dot_general(onehot, y_ref[0, e], (((0,), (0,)), ((), ())), preferred_element_type=F32)
        acc = t if acc is None else acc + t
    ms = jnp.mean(acc * acc, axis=-1, keepdims=True)
    y = (acc * lax.rsqrt(ms + EPS)) * g_ref[...]
    out_ref[0] = x_ref[0] + gate_ref[0, 0, 0] * y


def _combine(pos, y, xs, mod, gain, row0):
    b, n_tok, d = xs.shape
    tm = TOK_TILE
    n_e, n_slots = y.shape[1:3]
    return pl.pallas_call(
        _combine_body,
        grid=(b, n_tok // tm),
        in_specs=[pl.BlockSpec((1, n_e, tm), lambda b_, i: (b_, 0, i)),
                  pl.BlockSpec((1, n_e, n_slots, d), lambda b_, i: (b_, 0, 0, 0)),
                  pl.BlockSpec((1, tm, d), lambda b_, i: (b_, i, 0)),
                  _mod_spec(5, row0, tm),
                  pl.BlockSpec(gain.shape, lambda b_, i: (0, 0))],
        out_specs=pl.BlockSpec((1, tm, d), lambda b_, i: (b_, i, 0)),
        out_shape=jax.ShapeDtypeStruct(xs.shape, F32),
        compiler_params=_cparams(2),
        name="combine",
    )(pos, y, xs, mod, gain)


def kernel(x, c, ctx, c_ctx, w_ada, b_ada, g_pre_mix, g_post_mix, g_pre_ffn, g_post_ffn, w_in, w_out, q_gain_a, k_gain_a, lam_q1, lam_k1, lam_q2, lam_k2, subln_gain_b, sink_c, rpb_d, w_router, w_gate, w_up, w_down):
    b, n_lat, d = x.shape
    n_ctx = ctx.shape[1]
    depth = w_ada.shape[0]
    assert n_ctx == TOK_TILE and n_lat % TOK_TILE == 0 and d == D_MODEL and b + 1 <= 16

    xall = jnp.concatenate([ctx, x], axis=1)
    cc = jnp.zeros((16, d), F32).at[:b].set(c).at[b].set(c_ctx)
    mod_all = _ada(cc, w_ada, b_ada)
    ca, sa = _rope_lane_tables(n_ctx, n_lat, HEAD_DIM)
    cb, sb = _rope_lane_tables(n_ctx, n_lat, DIFF_DIM)
    row = lambda v: v.reshape(1, -1)

    for l in range(depth):
        with_ctx = l < depth - 1
        row0 = 0 if with_ctx else 1
        lam_init = 0.8 - 0.6 * math.exp(-0.3 * l)
        m = mod_all[l]
        mod = jnp.stack([jnp.broadcast_to(m[b], (b, 6 * d)), m[:b]], axis=1).reshape(b, 2, 6, 1, d)

        w_perm = _permute_w_in(w_in[l].astype(BF16))
        gq = jnp.stack([jnp.tile(q_gain_a[l, 0::2], 4), jnp.tile(q_gain_a[l, 1::2], 4)])
        gk = jnp.stack([jnp.tile(k_gain_a[l, 0::2], 4), jnp.tile(k_gain_a[l, 1::2], 4)])
        qkv = _proj(xall, mod, row(g_pre_mix[l]), w_perm, ca, sa, cb, sb, gq, gk)

        lam = (jnp.exp(jnp.sum(lam_q1[l] * lam_k1[l])) - jnp.exp(jnp.sum(lam_q2[l] * lam_k2[l]))
               + lam_init).reshape(1)
        o_mix = (
            _attn_global(qkv, 0, row0),
            _attn_global(qkv, 1, row0, lam=lam, gain=row(jnp.tile(subln_gain_b[l], N_HEADS)),
                         lam_init=lam_init),
            _attn_window(qkv, 2, row0, sink_c[l]),
            _attn_nbr(qkv, 3, row0, _neighbourhood_bias(rpb_d[l], n_ctx)),
        )
        xs = _outproj(o_mix, w_out[l].astype(BF16), xall, mod, row(g_post_mix[l]), row0)

        h, logits = _route(xs, mod, row(g_pre_ffn[l]), w_router[l].T, row0)
        cap_lat = CAPACITY_FACTOR * n_lat // N_EXPERTS
        cap_ctx = CAPACITY_FACTOR * n_ctx // N_EXPERTS
        if with_ctx:
            segments = ((0, n_ctx, cap_ctx, 0), (n_ctx, n_lat, cap_lat, cap_ctx))
            n_slots = cap_ctx + cap_lat
        else:
            segments = ((0, n_lat, cap_lat, 0),)
            n_slots = cap_lat
        pos, aff = _select(logits, segments)
        y = _experts(pos, aff, h, w_gate, w_up, w_down, l, n_slots)
        xall = _combine(pos, y, xs, mod, row(g_post_ffn[l]), row0)
    return xall
```

```python
import functools
import math

import numpy as np
import jax
import jax.numpy as jnp
from jax import lax
from jax.experimental import pallas as pl
from jax.experimental.pallas import tpu as pltpu

F32 = jnp.float32
BF16 = jnp.bfloat16

D_MODEL = 1024
GRID_W = 64
HEAD_DIM = 64
DIFF_DIM = 32
N_HEADS = 4
MIXER_W = N_HEADS * HEAD_DIM
QKV_W = 4 * 3 * MIXER_W
WINDOW = 128
NA_ROWS = 8
NA_COLS = 16
ROPE_BASE = 10000.0
N_EXPERTS = 16
CAPACITY_FACTOR = 2
EPS = 1e-6
NEG_INF = -1e30
PROJ_WIDTHS = (256, 128, 128, 256, 256, 256, 256, 128, 128, 256, 256, 256)
LANES = 128
TOK_TILE = 256
VMEM_LIMIT = 56 * 1024 * 1024
NBR_TILE_ROWS = TOK_TILE // GRID_W
NBR_STRIP_ROWS = 12
BISECT_MAX_STEPS = 192


def _cparams(n_axes):
    return pltpu.CompilerParams(dimension_semantics=("arbitrary",) * n_axes,
                                vmem_limit_bytes=VMEM_LIMIT)


def _permute_w_in(w):
    d = w.shape[0]
    offs = np.concatenate([[0], np.cumsum(PROJ_WIDTHS)])

    def pairs(a, n_slots, n_pairs, rep=1):
        a = a.reshape(d, n_slots // rep, n_pairs, 2)
        if rep > 1:
            a = jnp.repeat(a, rep, axis=1)
        return a.transpose(0, 3, 1, 2).reshape(d, MIXER_W)

    out = []
    for m in range(4):
        q, k, v = (w[:, offs[3 * m + j]:offs[3 * m + j + 1]] for j in range(3))
        if m in (0, 2):
            v = jnp.repeat(v.reshape(d, 2, HEAD_DIM), 2, axis=1).reshape(d, MIXER_W)
            out += [pairs(q, 4, 32), pairs(k, 4, 32, rep=2), v]
        elif m == 1:
            out += [pairs(q, 8, 16), pairs(k, 8, 16), v]
        else:
            out += [pairs(q, 4, 32), pairs(k, 4, 32), v]
    return jnp.concatenate(out, axis=1)


def _rope_tables(n_tokens, dim):
    t = jnp.arange(n_tokens, dtype=jnp.int32)
    rows = (t // GRID_W).astype(F32)
    cols = (t % GRID_W).astype(F32)
    n_axis = dim // 4
    inv_freq = ROPE_BASE ** (-jnp.arange(n_axis, dtype=F32) / n_axis)
    ang = jnp.concatenate([rows[:, None] * inv_freq, cols[:, None] * inv_freq], axis=-1)
    return jnp.cos(ang), jnp.sin(ang)


def _rope_lane_tables(n_ctx, n_lat, dim):
    cos, sin = _rope_tables(n_lat, dim)
    reps = LANES // cos.shape[1]
    cos = jnp.concatenate([jnp.ones((n_ctx, LANES), F32), jnp.tile(cos, (1, reps))], axis=0)
    sin = jnp.concatenate([jnp.zeros((n_ctx, LANES), F32), jnp.tile(sin, (1, reps))], axis=0)
    return cos, sin


def _nbr_strip_start(tile, n_rows):
    lo = tile * NBR_TILE_ROWS - NA_ROWS // 2
    return jnp.clip(lo, 0, n_rows - NBR_STRIP_ROWS) if isinstance(lo, jax.Array) else int(
        np.clip(lo, 0, n_rows - NBR_STRIP_ROWS))


def _nbr_pattern(tile, n_tiles):
    if isinstance(tile, jax.Array):
        return jnp.where(tile == 0, 0, jnp.where(tile == n_tiles - 1, 2, 1))
    return 0 if tile == 0 else (2 if tile == n_tiles - 1 else 1)


def _neighbourhood_bias(rpb, n_ctx, n_rows):
    n_tiles = n_rows // NBR_TILE_ROWS
    assert n_tiles >= 3 and n_rows >= NBR_STRIP_ROWS

    def rows_of(tile):
        ss = _nbr_strip_start(tile, n_rows)
        r = tile * NBR_TILE_ROWS + np.arange(NBR_TILE_ROWS)[:, None]
        rs = np.clip(r - NA_ROWS // 2, 0, n_rows - NA_ROWS)
        kr = ss + np.arange(NBR_STRIP_ROWS)[None, :]
        valid = (kr >= rs) & (kr < rs + NA_ROWS)
        return valid, np.where(valid, kr - r + NA_ROWS - 1, 0)

    reps = [rows_of(t) for t in (0, 1, n_tiles - 1)]
    for t in range(n_tiles):
        v, ri = rows_of(t)
        assert np.array_equal(v, reps[_nbr_pattern(t, n_tiles)][0])
        assert np.array_equal(ri, reps[_nbr_pattern(t, n_tiles)][1])
    row_valid = np.stack([v for v, _ in reps])
    ri = np.stack([r for _, r in reps])
    oh_r = (ri[..., None] == np.arange(2 * NA_ROWS - 1)) & row_valid[..., None]
    cq = np.arange(GRID_W)
    col_start = np.clip(cq - NA_COLS // 2, 0, GRID_W - NA_COLS)
    col_valid = (cq[None, :] >= col_start[:, None]) & (cq[None, :] < col_start[:, None] + NA_COLS)
    ci = np.clip(cq[None, :] - cq[:, None] + NA_COLS - 1, 0, 2 * NA_COLS - 2)
    oh_c = np.arange(2 * NA_COLS - 1)[:, None, None] == ci[None]
    bias = jnp.einsum('hrc,pajr,cqw->hpaqjw', rpb.astype(F32), oh_r.astype(np.float32),
                      oh_c.astype(np.float32), precision=lax.Precision.HIGHEST)
    valid = row_valid[None, :, :, None, :, None] & col_valid[None, None, None, :, None, :]
    bias = jnp.where(valid, bias, NEG_INF)
    bias = bias.reshape(bias.shape[0], 3, TOK_TILE, NBR_STRIP_ROWS * GRID_W)
    return jnp.concatenate([jnp.zeros(bias.shape[:3] + (n_ctx,), F32), bias], axis=-1)


def _block_diag_ones(n, group):
    r = lax.broadcasted_iota(jnp.int32, (n, n), 0) // group
    c = lax.broadcasted_iota(jnp.int32, (n, n), 1) // group
    return jnp.where(r == c, 1.0, 0.0).astype(BF16)


def _dot(a, b):
    return jnp.dot(a, b, preferred_element_type=F32)


def _dot_nt(a, b):
    return lax.dot_general(a, b, (((1,), (1,)), ((), ())), preferred_element_type=F32)


def _split(a):
    hi = a.astype(BF16)
    return hi, (a - hi.astype(F32)).astype(BF16)


def _dot_hilo(a, b_bf16):
    hi, lo = _split(a)
    return _dot(hi, b_bf16) + _dot(lo, b_bf16)


def _lane_mask(cond):
    return jnp.where(cond, 1.0, 0.0).astype(BF16)


def _norm_modulate(x, gain, shift, scale):
    ms = jnp.mean(x * x, axis=-1, keepdims=True)
    h = (x * lax.rsqrt(ms + EPS)) * gain
    return h * (1.0 + scale) + shift


def _mod_spec(chunk, row0, tile):
    n_ctx_tiles = TOK_TILE // tile
    return pl.BlockSpec((1, 1, 1, 1, D_MODEL),
                        lambda b, i: (b, jnp.minimum((i + row0) // n_ctx_tiles, 1), chunk, 0, 0))


def _ada_body(c_ref, w_ref, b_ref, o_ref):
    c = c_ref[...]
    cs = (c * jax.nn.sigmoid(c)).astype(BF16)
    o_ref[0] = _dot(cs, w_ref[0].astype(BF16)) + b_ref[0]


def _ada(cc, w_ada, b_ada):
    depth, d, n = w_ada.shape
    tn = 512
    return pl.pallas_call(
        _ada_body,
        grid=(depth, n // tn),
        in_specs=[pl.BlockSpec(cc.shape, lambda l, j: (0, 0)),
                  pl.BlockSpec((1, d, tn), lambda l, j: (l, 0, j)),
                  pl.BlockSpec((1, 1, tn), lambda l, j: (l, 0, j))],
        out_specs=pl.BlockSpec((1, cc.shape[0], tn), lambda l, j: (l, 0, j)),
        out_shape=jax.ShapeDtypeStruct((depth, cc.shape[0], n), F32),
        compiler_params=_cparams(2),
        name="ada",
    )(cc, w_ada, b_ada.reshape(depth, 1, n))


def _proj_body(x_ref, sh_ref, sc_ref, g_ref, w_ref, ca_ref, sa_ref, cb_ref, sb_ref, gq_ref, gk_ref,
               o_ref):
    h = _norm_modulate(x_ref[0], g_ref[...], sh_ref[0, 0, 0], sc_ref[0, 0, 0])
    acc = _dot(h.astype(BF16), w_ref[...])
    ca, sa, cb, sb = ca_ref[...], sa_ref[...], cb_ref[...], sb_ref[...]
    bd = _block_diag_ones(LANES, 32)

    def get(m, j):
        c0 = (3 * m) * MIXER_W + j * LANES
        return acc[:, c0:c0 + LANES]

    def put(m, j, v):
        c0 = (3 * m) * MIXER_W + j * LANES
        o_ref[0, :, c0:c0 + LANES] = v.astype(BF16)

    def rope(x0, x1, c, s):
        return x0 * c - x1 * s, x0 * s + x1 * c

    def head_norm(x0, x1, g_ref_):
        gs = _dot_hilo(x0 * x0 + x1 * x1, bd)
        r = lax.rsqrt(gs * (1.0 / HEAD_DIM) + EPS)
        return x0 * r * g_ref_[0:1, :], x1 * r * g_ref_[1:2, :]

    for m in range(4):
        q0, q1, k0, k1 = get(m, 0), get(m, 1), get(m, 2), get(m, 3)
        if m == 0:
            q0, q1 = head_norm(q0, q1, gq_ref)
            k0, k1 = head_norm(k0, k1, gk_ref)
        if m in (0, 2):
            q0, q1 = rope(q0, q1, ca, sa)
            k0, k1 = rope(k0, k1, ca, sa)
        elif m == 1:
            q0, q1 = rope(q0, q1, cb, sb)
            k0, k1 = rope(k0, k1, cb, sb)
        qscale = (DIFF_DIM if m == 1 else HEAD_DIM) ** -0.5
        put(m, 0, q0 * qscale)
        put(m, 1, q1 * qscale)
        put(m, 2, k0)
        put(m, 3, k1)
        put(m, 4, get(m, 4))
        put(m, 5, get(m, 5))


def _proj(xall, mod, gain, w_perm, ca, sa, cb, sb, gq, gk):
    b, lt, d = xall.shape
    tm = TOK_TILE
    tab = pl.BlockSpec((tm, LANES), lambda b_, i: (i, 0))
    full = lambda a: pl.BlockSpec(a.shape, lambda b_, i: (0,) * a.ndim)
    return pl.pallas_call(
        _proj_body,
        grid=(b, lt // tm),
        in_specs=[pl.BlockSpec((1, tm, d), lambda b_, i: (b_, i, 0)),
                  _mod_spec(0, 0, tm), _mod_spec(1, 0, tm),
                  full(gain), full(w_perm), tab, tab, tab, tab, full(gq), full(gk)],
        out_specs=pl.BlockSpec((1, tm, QKV_W), lambda b_, i: (b_, i, 0)),
        out_shape=jax.ShapeDtypeStruct((b, lt, QKV_W), BF16),
        compiler_params=_cparams(2),
        name="proj",
    )(xall, mod, mod, gain, w_perm, ca, sa, cb, sb, gq, gk)


def _softmax_parts(s, extra_logit=None):
    m = jnp.max(s, axis=-1, keepdims=True)
    if extra_logit is not None:
        m = jnp.maximum(m, extra_logit)
    p = jnp.exp(s - m)
    l = jnp.sum(p, axis=-1, keepdims=True)
    if extra_logit is not None:
        l = l + jnp.exp(extra_logit - m)
    return p, l


def _attn_global_body(*refs, row0, diff, lam_init, n_ctx):
    if diff:
        lam_ref, q_ref, k_ref, v_ref, gain_ref, o_ref = refs
    else:
        q_ref, k_ref, v_ref, o_ref = refs
    i = pl.program_id(1) + row0
    lane = lax.broadcasted_iota(jnp.int32, (1, MIXER_W), 1)

    def run(nk):
        q = q_ref[0]
        k = k_ref[0, :nk, :]
        v = v_ref[0, :nk, :]
        acc = jnp.zeros((q.shape[0], MIXER_W), F32)
        for h in range(N_HEADS):
            vm = v * _lane_mask(lane // HEAD_DIM == h)
            if diff:
                lam = lam_ref[0]
                a = None
                for c in range(2):
                    qm = q * _lane_mask((lane % LANES) // (DIFF_DIM // 2) == 2 * h + c)
                    p, l = _softmax_parts(_dot_nt(qm, k))
                    a = p * (1.0 / l) if c == 0 else a - p * (lam / l)
                acc = acc + _dot(a.astype(BF16), vm)
            else:
                qm = q * _lane_mask((lane % LANES) // (HEAD_DIM // 2) == h)
                p, l = _softmax_parts(_dot_nt(qm, k))
                acc = acc + _dot(p.astype(BF16), vm) * (1.0 / l)
        if diff:
            gs = _dot_hilo(acc * acc, _block_diag_ones(MIXER_W, HEAD_DIM))
            acc = acc * lax.rsqrt(gs * (1.0 / HEAD_DIM) + EPS) * gain_ref[...] * (1.0 - lam_init)
        o_ref[0] = acc.astype(BF16)

    if row0 == 0:
        @pl.when(i == 0)
        def _():
            run(n_ctx)

        @pl.when(i > 0)
        def _():
            run(k_ref.shape[1])
    else:
        run(k_ref.shape[1])


def _attn_global(qkv, mixer, row0, lam=None, gain=None, lam_init=0.0):
    b, lt, _ = qkv.shape
    tq = TOK_TILE
    diff = lam is not None
    kv_spec = lambda j: pl.BlockSpec((1, lt, MIXER_W), lambda b_, i: (b_, 0, 3 * mixer + j))
    in_specs = [pl.BlockSpec((1, tq, MIXER_W), lambda b_, i: (b_, i + row0, 3 * mixer)),
                kv_spec(1), kv_spec(2)]
    args = [qkv, qkv, qkv]
    if diff:
        in_specs = [pl.BlockSpec(memory_space=pltpu.SMEM)] + in_specs + [
            pl.BlockSpec(gain.shape, lambda b_, i: (0, 0))]
        args = [lam] + args + [gain]
    return pl.pallas_call(
        functools.partial(_attn_global_body, row0=row0, diff=diff, lam_init=lam_init, n_ctx=TOK_TILE),
        grid=(b, lt // tq - row0),
        in_specs=in_specs,
        out_specs=pl.BlockSpec((1, tq, MIXER_W), lambda b_, i: (b_, i, 0)),
        out_shape=jax.ShapeDtypeStruct((b, lt - row0 * tq, MIXER_W), BF16),
        compiler_params=_cparams(2),
        name="attn_diff" if diff else "attn_global",
    )(*args)


def _attn_window_body(sink_ref, q_ref, k_ref, v_ref, o_ref, *, row0, n_ctx, n_lat):
    tq = q_ref.shape[1]
    band = tq + 2 * WINDOW
    n_ctx_tiles = n_ctx // tq
    i = pl.program_id(1) + row0
    lane = lax.broadcasted_iota(jnp.int32, (1, MIXER_W), 1)

    def run(is_ctx):
        q = q_ref[0]
        kk = k_ref[0, :n_ctx, :]
        vv = v_ref[0, :n_ctx, :]
        valid = None
        if not is_ctx:
            q_start = (i - n_ctx_tiles) * tq
            k_start = jnp.clip(q_start - WINDOW, 0, n_lat - band)
            ks = pl.multiple_of(n_ctx + k_start, LANES)
            kk = jnp.concatenate([kk, k_ref[0, pl.ds(ks, band), :]], axis=0)
            vv = jnp.concatenate([vv, v_ref[0, pl.ds(ks, band), :]], axis=0)
            col = lax.broadcasted_iota(jnp.int32, (1, n_ctx + band), 1)
            row = lax.broadcasted_iota(jnp.int32, (tq, 1), 0)
            dist = (k_start + col - n_ctx) - (q_start + row)
            valid = jnp.where(col < n_ctx, 0, dist)
            valid = jnp.abs(valid) <= WINDOW
        acc = jnp.zeros((tq, MIXER_W), F32)
        for h in range(N_HEADS):
            qm = q * _lane_mask((lane % LANES) // (HEAD_DIM // 2) == h)
            s = _dot_nt(qm, kk)
            if valid is not None:
                s = jnp.where(valid, s, NEG_INF)
            p, l = _softmax_parts(s, extra_logit=sink_ref[h])
            vm = vv * _lane_mask(lane // HEAD_DIM == h)
            acc = acc + _dot(p.astype(BF16), vm) * (1.0 / l)
        o_ref[0] = acc.astype(BF16)

    if row0 == 0:
        @pl.when(i < n_ctx_tiles)
        def _():
            run(True)

        @pl.when(i >= n_ctx_tiles)
        def _():
            run(False)
    else:
        run(False)


def _attn_window(qkv, mixer, row0_tiles, sink):
    b, lt, _ = qkv.shape
    tq = TOK_TILE
    row0 = row0_tiles * (TOK_TILE // tq)
    kv_spec = lambda j: pl.BlockSpec((1, lt, MIXER_W), lambda b_, i: (b_, 0, 3 * mixer + j))
    return pl.pallas_call(
        functools.partial(_attn_window_body, row0=row0, n_ctx=TOK_TILE, n_lat=lt - TOK_TILE),
        grid=(b, lt // tq - row0),
        in_specs=[pl.BlockSpec(memory_space=pltpu.SMEM),
                  pl.BlockSpec((1, tq, MIXER_W), lambda b_, i: (b_, i + row0, 3 * mixer)),
                  kv_spec(1), kv_spec(2)],
        out_specs=pl.BlockSpec((1, tq, MIXER_W), lambda b_, i: (b_, i, 0)),
        out_shape=jax.ShapeDtypeStruct((b, lt - row0 * tq, MIXER_W), BF16),
        compiler_params=_cparams(2),
        name="attn_window",
    )(sink, qkv, qkv, qkv)


def _attn_nbr_body(q_ref, k_ref, v_ref, bias_ref, o_ref, *, row0, n_ctx, n_lat):
    tq = q_ref.shape[1]
    strip = NBR_STRIP_ROWS * GRID_W
    n_ctx_tiles = n_ctx // tq
    i = pl.program_id(1) + row0
    lane = lax.broadcasted_iota(jnp.int32, (1, MIXER_W), 1)

    def run(is_ctx):
        q = q_ref[0]
        kk = k_ref[0, :n_ctx, :]
        vv = v_ref[0, :n_ctx, :]
        if not is_ctx:
            rs = _nbr_strip_start(i - n_ctx_tiles, n_lat // GRID_W)
            ks = pl.multiple_of(n_ctx + rs * GRID_W, GRID_W)
            kk = jnp.concatenate([kk, k_ref[0, pl.ds(ks, strip), :]], axis=0)
            vv = jnp.concatenate([vv, v_ref[0, pl.ds(ks, strip), :]], axis=0)
        acc = jnp.zeros((tq, MIXER_W), F32)
        for h in range(N_HEADS):
            qm = q * _lane_mask((lane % LANES) // (HEAD_DIM // 2) == h)
            s = _dot_nt(qm, kk)
            if not is_ctx:
                s = s + bias_ref[h, 0]
            p, l = _softmax_parts(s)
            vm = vv * _lane_mask(lane // HEAD_DIM == h)
            acc = acc + _dot(p.astype(BF16), vm) * (1.0 / l)
        o_ref[0] = acc.astype(BF16)

    if row0 == 0:
        @pl.when(i < n_ctx_tiles)
        def _():
            run(True)

        @pl.when(i >= n_ctx_tiles)
        def _():
            run(False)
    else:
        run(False)


def _attn_nbr(qkv, mixer, row0_tiles, bias):
    b, lt, _ = qkv.shape
    tq = TOK_TILE
    n_ctx_tiles = TOK_TILE // tq
    row0 = row0_tiles * n_ctx_tiles
    n_tiles = (lt - TOK_TILE) // tq
    kv_spec = lambda j: pl.BlockSpec((1, lt, MIXER_W), lambda b_, i: (b_, 0, 3 * mixer + j))

    def bias_map(b_, i):
        return (0, _nbr_pattern(jnp.maximum(i + row0 - n_ctx_tiles, 0), n_tiles), 0, 0)

    return pl.pallas_call(
        functools.partial(_attn_nbr_body, row0=row0, n_ctx=TOK_TILE, n_lat=lt - TOK_TILE),
        grid=(b, lt // tq - row0),
        in_specs=[pl.BlockSpec((1, tq, MIXER_W), lambda b_, i: (b_, i + row0, 3 * mixer)),
                  kv_spec(1), kv_spec(2),
                  pl.BlockSpec((N_HEADS, 1) + bias.shape[2:], bias_map)],
        out_specs=pl.BlockSpec((1, tq, MIXER_W), lambda b_, i: (b_, i, 0)),
        out_shape=jax.ShapeDtypeStruct((b, lt - row0 * tq, MIXER_W), BF16),
        compiler_params=_cparams(2),
        name="attn_nbr",
    )(qkv, qkv, qkv, bias)


def _outproj_body(oa_ref, ob_ref, oc_ref, od_ref, w_ref, x_ref, gate_ref, g_ref, out_ref):
    acc = None
    for m, o_ref in enumerate((oa_ref, ob_ref, oc_ref, od_ref)):
        t = _dot(o_ref[0], w_ref[m * MIXER_W:(m + 1) * MIXER_W, :])
        acc = t if acc is None else acc + t
    ms = jnp.mean(acc * acc, axis=-1, keepdims=True)
    y = (acc * lax.rsqrt(ms + EPS)) * g_ref[...]
    out_ref[0] = x_ref[0] + gate_ref[0, 0, 0] * y


def _outproj(o_mix, w_out, xall, mod, gain, row0):
    b, lt, d = xall.shape
    tm = TOK_TILE
    n_tiles = lt // tm - row0
    o_spec = pl.BlockSpec((1, tm, MIXER_W), lambda b_, i: (b_, i, 0))
    return pl.pallas_call(
        _outproj_body,
        grid=(b, n_tiles),
        in_specs=[o_spec, o_spec, o_spec, o_spec,
                  pl.BlockSpec(w_out.shape, lambda b_, i: (0, 0)),
                  pl.BlockSpec((1, tm, d), lambda b_, i: (b_, i + row0, 0)),
                  _mod_spec(2, row0, tm),
                  pl.BlockSpec(gain.shape, lambda b_, i: (0, 0))],
        out_specs=pl.BlockSpec((1, tm, d), lambda b_, i: (b_, i, 0)),
        out_shape=jax.ShapeDtypeStruct((b, n_tiles * tm, d), F32),
        compiler_params=_cparams(2),
        name="outproj",
    )(*o_mix, w_out, xall, mod, gain)


def _route_body(x_ref, sh_ref, sc_ref, g_ref, wr_ref, h_ref, lg_ref):
    h = _norm_modulate(x_ref[0], g_ref[...], sh_ref[0, 0, 0], sc_ref[0, 0, 0])
    h_ref[0] = h.astype(BF16)
    w_hi, w_lo = _split(wr_ref[...])
    h_hi, h_lo = _split(h)
    lg_ref[0] = _dot_nt(w_hi, h_hi) + (_dot_nt(w_hi, h_lo) + _dot_nt(w_lo, h_hi))


def _route(xs, mod, gain, w_router_t, row0):
    b, n_tok, d = xs.shape
    tm = TOK_TILE
    n_e = w_router_t.shape[0]
    return pl.pallas_call(
        _route_body,
        grid=(b, n_tok // tm),
        in_specs=[pl.BlockSpec((1, tm, d), lambda b_, i: (b_, i, 0)),
                  _mod_spec(3, row0, tm), _mod_spec(4, row0, tm),
                  pl.BlockSpec(gain.shape, lambda b_, i: (0, 0)),
                  pl.BlockSpec(w_router_t.shape, lambda b_, i: (0, 0))],
        out_specs=[pl.BlockSpec((1, tm, d), lambda b_, i: (b_, i, 0)),
                   pl.BlockSpec((1, n_e, tm), lambda b_, i: (b_, 0, i))],
        out_shape=[jax.ShapeDtypeStruct((b, n_tok, d), BF16),
                   jax.ShapeDtypeStruct((b, n_e, n_tok), F32)],
        compiler_params=_cparams(2),
        name="route",
    )(xs, mod, mod, gain, w_router_t)


def _exclusive_cumsum(x, tri):
    off = jnp.zeros((x.shape[0], 1), F32)
    outs = []
    for c in range(x.shape[1] // LANES):
        xc = x[:, c * LANES:(c + 1) * LANES]
        inc = _dot(xc.astype(BF16), tri)
        outs.append(inc - xc + off)
        off = off + inc[:, LANES - 1:LANES]
    return jnp.concatenate(outs, axis=1)


def _select_body(lg_ref, pos_ref, aff_ref, *, segments):
    r = lax.broadcasted_iota(jnp.int32, (LANES, LANES), 0)
    c = lax.broadcasted_iota(jnp.int32, (LANES, LANES), 1)
    tri = jnp.where(r <= c, 1.0, 0.0).astype(BF16)
    for t0, t, cap, slot0 in segments:
        lg = lg_ref[0, :, t0:t0 + t]
        e = jnp.exp(lg - jnp.max(lg, axis=0, keepdims=True))
        aff = e / jnp.sum(e, axis=0, keepdims=True)

        def count_above(thr):
            return jnp.sum(jnp.where(aff > thr, 1.0, 0.0), axis=1, keepdims=True)

        def unsettled(carry):
            return jnp.logical_and(carry[2] > 0.0, carry[3] < BISECT_MAX_STEPS)

        def bisect(carry):
            lo, hi, _, step = carry
            mid = 0.5 * (lo + hi)
            cnt = count_above(mid)
            new_lo = jnp.where(cnt >= cap, mid, lo)
            new_hi = jnp.where(cnt <= cap, mid, hi)
            moving = jnp.logical_and(new_lo < new_hi, jnp.logical_and(mid > lo, mid < hi))
            return new_lo, new_hi, jnp.sum(jnp.where(moving, 1.0, 0.0)), step + 1

        lo0 = jnp.full((aff.shape[0], 1), -1.0, F32)
        hi0 = jnp.max(aff, axis=1, keepdims=True)
        lo, hi, _, _ = lax.while_loop(unsettled, bisect, (lo0, hi0, jnp.float32(1.0), jnp.int32(0)))
        gt = aff > hi
        eq = jnp.logical_and(aff > lo, aff <= hi)
        need = cap - count_above(hi)
        eq_rank = _exclusive_cumsum(jnp.where(eq, 1.0, 0.0), tri)
        sel = jnp.logical_or(gt, jnp.logical_and(eq, eq_rank < need))
        slot = _exclusive_cumsum(jnp.where(sel, 1.0, 0.0), tri) + slot0
        pos_ref[0, :, t0:t0 + t] = jnp.where(sel, slot, -1.0).astype(jnp.int32)
        aff_ref[0, :, t0:t0 + t] = aff


def _select(logits, segments):
    b, n_e, lt = logits.shape
    spec = pl.BlockSpec((1, n_e, lt), lambda b_: (b_, 0, 0))
    return pl.pallas_call(
        functools.partial(_select_body, segments=segments),
        grid=(b,),
        in_specs=[spec],
        out_specs=[spec, spec],
        out_shape=[jax.ShapeDtypeStruct((b, n_e, lt), jnp.int32),
                   jax.ShapeDtypeStruct((b, n_e, lt), F32)],
        compiler_params=_cparams(1),
        name="select",
    )(logits)


def _experts_body(pos_ref, aff_ref, h_ref, wg_ref, wu_ref, wd_ref, y_ref, wg_s, wu_s, wd_s):
    @pl.when(pl.program_id(1) == 0)
    def _():
        wg_s[...] = wg_ref[0, 0].astype(BF16)
        wu_s[...] = wu_ref[0, 0].astype(BF16)
        wd_s[...] = wd_ref[0, 0].astype(BF16)

    n_slots = y_ref.shape[2]
    pos = pos_ref[0, 0]
    hit = lax.broadcasted_iota(jnp.int32, (n_slots, pos.shape[1]), 0) == pos
    xe = _dot(jnp.where(hit, 1.0, 0.0).astype(BF16), h_ref[0]).astype(BF16)
    gate = jnp.sum(jnp.where(hit, aff_ref[0, 0], 0.0), axis=1, keepdims=True)
    a = _dot(xe, wg_s[...])
    u = _dot(xe, wu_s[...])
    act = ((a * jax.nn.sigmoid(a)) * u).astype(BF16)
    y_ref[0, 0] = (_dot(act, wd_s[...]) * gate).astype(BF16)


def _experts(pos, aff, h, w_gate, w_up, w_down, layer, n_slots):
    b, n_e, n_tok = pos.shape
    d, f = w_gate.shape[2:]
    pos4 = pos.reshape(b, n_e, 1, n_tok)
    aff4 = aff.reshape(b, n_e, 1, n_tok)
    row_spec = pl.BlockSpec((1, 1, 1, n_tok), lambda e, b_: (b_, e, 0, 0))
    return pl.pallas_call(
        _experts_body,
        grid=(n_e, b),
        in_specs=[row_spec, row_spec,
                  pl.BlockSpec((1, n_tok, d), lambda e, b_: (b_, 0, 0)),
                  pl.BlockSpec((1, 1, d, f), lambda e, b_: (layer, e, 0, 0)),
                  pl.BlockSpec((1, 1, d, f), lambda e, b_: (layer, e, 0, 0)),
                  pl.BlockSpec((1, 1, f, d), lambda e, b_: (layer, e, 0, 0))],
        out_specs=pl.BlockSpec((1, 1, n_slots, d), lambda e, b_: (b_, e, 0, 0)),
        out_shape=jax.ShapeDtypeStruct((b, n_e, n_slots, d), BF16),
        scratch_shapes=[pltpu.VMEM((d, f), BF16), pltpu.VMEM((d, f), BF16), pltpu.VMEM((f, d), BF16)],
        compiler_params=_cparams(2),
        name="experts",
    )(pos4, aff4, h, w_gate, w_up, w_down)


def _combine_body(pos_ref, y_ref, x_ref, gate_ref, g_ref, out_ref):
    n_e, n_slots = y_ref.shape[1:3]
    pos = pos_ref[0]
    slot = lax.broadcasted_iota(jnp.int32, (n_slots, pos.shape[1]), 0)
    acc = None
    for e in range(n_e):
        onehot = jnp.where(slot == pos[e:e + 1, :], 1.0, 0.0).astype(BF16)
        t = lax.dot_general(onehot, y_ref[0, e], (((0,), (0,)), ((), ())), preferred_element_type=F32)
        acc = t if acc is None else acc + t
    ms = jnp.mean(acc * acc, axis=-1, keepdims=True)
    y = (acc * lax.rsqrt(ms + EPS)) * g_ref[...]
    out_ref[0] = x_ref[0] + gate_ref[0, 0, 0] * y


def _combine(pos, y, xs, mod, gain, row0):
    b, n_tok, d = xs.shape
    tm = TOK_TILE
    n_e, n_slots = y.shape[1:3]
    return pl.pallas_call(
        _combine_body,
        grid=(b, n_tok // tm),
        in_specs=[pl.BlockSpec((1, n_e, tm), lambda b_, i: (b_, 0, i)),
                  pl.BlockSpec((1, n_e, n_slots, d), lambda b_, i: (b_, 0, 0, 0)),
                  pl.BlockSpec((1, tm, d), lambda b_, i: (b_, i, 0)),
                  _mod_spec(5, row0, tm),
                  pl.BlockSpec(gain.shape, lambda b_, i: (0, 0))],
        out_specs=pl.BlockSpec((1, tm, d), lambda b_, i: (b_, i, 0)),
        out_shape=jax.ShapeDtypeStruct(xs.shape, F32),
        compiler_params=_cparams(2),
        name="combine",
    )(pos, y, xs, mod, gain)


def kernel(x, c, ctx, c_ctx, w_ada, b_ada, g_pre_mix, g_post_mix, g_pre_ffn, g_post_ffn, w_in, w_out, q_gain_a, k_gain_a, lam_q1, lam_k1, lam_q2, lam_k2, subln_gain_b, sink_c, rpb_d, w_router, w_gate, w_up, w_down):
    b, n_lat, d = x.shape
    n_ctx = ctx.shape[1]
    depth = w_ada.shape[0]
    assert n_ctx == TOK_TILE and n_lat % TOK_TILE == 0 and d == D_MODEL and b + 1 <= 16

    xall = jnp.concatenate([ctx, x], axis=1)
    cc = jnp.zeros((16, d), F32).at[:b].set(c).at[b].set(c_ctx)
    mod_all = _ada(cc, w_ada, b_ada)
    ca, sa = _rope_lane_tables(n_ctx, n_lat, HEAD_DIM)
    cb, sb = _rope_lane_tables(n_ctx, n_lat, DIFF_DIM)
    row = lambda v: v.reshape(1, -1)

    for l in range(depth):
        with_ctx = l < depth - 1
        row0 = 0 if with_ctx else 1
        lam_init = 0.8 - 0.6 * math.exp(-0.3 * l)
        m = mod_all[l]
        mod = jnp.stack([jnp.broadcast_to(m[b], (b, 6 * d)), m[:b]], axis=1).reshape(b, 2, 6, 1, d)

        w_perm = _permute_w_in(w_in[l].astype(BF16))
        gq = jnp.stack([jnp.tile(q_gain_a[l, 0::2], 4), jnp.tile(q_gain_a[l, 1::2], 4)])
        gk = jnp.stack([jnp.tile(k_gain_a[l, 0::2], 4), jnp.tile(k_gain_a[l, 1::2], 4)])
        qkv = _proj(xall, mod, row(g_pre_mix[l]), w_perm, ca, sa, cb, sb, gq, gk)

        lam = (jnp.exp(jnp.sum(lam_q1[l] * lam_k1[l])) - jnp.exp(jnp.sum(lam_q2[l] * lam_k2[l]))
               + lam_init).reshape(1)
        o_mix = (
            _attn_global(qkv, 0, row0),
            _attn_global(qkv, 1, row0, lam=lam, gain=row(jnp.tile(subln_gain_b[l], N_HEADS)),
                         lam_init=lam_init),
            _attn_window(qkv, 2, row0, sink_c[l]),
            _attn_nbr(qkv, 3, row0, _neighbourhood_bias(rpb_d[l], n_ctx, n_lat // GRID_W)),
        )
        xs = _outproj(o_mix, w_out[l].astype(BF16), xall, mod, row(g_post_mix[l]), row0)

        h, logits = _route(xs, mod, row(g_pre_ffn[l]), w_router[l].T, row0)
        cap_lat = CAPACITY_FACTOR * n_lat // N_EXPERTS
        cap_ctx = CAPACITY_FACTOR * n_ctx // N_EXPERTS
        if with_ctx:
            segments = ((0, n_ctx, cap_ctx, 0), (n_ctx, n_lat, cap_lat, cap_ctx))
            n_slots = cap_ctx + cap_lat
        else:
            segments = ((0, n_lat, cap_lat, 0),)
            n_slots = cap_lat
        pos, aff = _select(logits, segments)
        y = _experts(pos, aff, h, w_gate, w_up, w_down, l, n_slots)
        xall = _combine(pos, y, xs, mod, row(g_post_ffn[l]), row0)
    return xall
```

```python
import functools
import math

import numpy as np
import jax
import jax.numpy as jnp
from jax import lax
from jax.experimental import pallas as pl
from jax.experimental.pallas import tpu as pltpu

F32 = jnp.float32
BF16 = jnp.bfloat16

D_MODEL = 1024
GRID_W = 64
HEAD_DIM = 64
DIFF_DIM = 32
N_HEADS = 4
MIXER_W = N_HEADS * HEAD_DIM
QKV_W = 4 * 3 * MIXER_W
WINDOW = 128
NA_ROWS = 8
NA_COLS = 16
ROPE_BASE = 10000.0
N_EXPERTS = 16
CAPACITY_FACTOR = 2
EPS = 1e-6
NEG_INF = -1e30
LOG2E = 1.4426950408889634
PROJ_WIDTHS = (256, 128, 128, 256, 256, 256, 256, 128, 128, 256, 256, 256)
LANES = 128
TOK_TILE = 256
VMEM_LIMIT = 56 * 1024 * 1024
NBR_TILE_ROWS = TOK_TILE // GRID_W
NBR_STRIP_ROWS = 12
BISECT_MAX_STEPS = 192


def _cparams(n_axes):
    return pltpu.CompilerParams(dimension_semantics=("arbitrary",) * n_axes,
                                vmem_limit_bytes=VMEM_LIMIT)


def _permute_w_in(w):
    d = w.shape[0]
    offs = np.concatenate([[0], np.cumsum(PROJ_WIDTHS)])

    def pairs(a, n_slots, n_pairs, rep=1):
        a = a.reshape(d, n_slots // rep, n_pairs, 2)
        if rep > 1:
            a = jnp.repeat(a, rep, axis=1)
        return a.transpose(0, 3, 1, 2).reshape(d, MIXER_W)

    out = []
    for m in range(4):
        q, k, v = (w[:, offs[3 * m + j]:offs[3 * m + j + 1]] for j in range(3))
        if m in (0, 2):
            v = jnp.repeat(v.reshape(d, 2, HEAD_DIM), 2, axis=1).reshape(d, MIXER_W)
            out += [pairs(q, 4, 32), pairs(k, 4, 32, rep=2), v]
        elif m == 1:
            out += [pairs(q, 8, 16), pairs(k, 8, 16), v]
        else:
            out += [pairs(q, 4, 32), pairs(k, 4, 32), v]
    return jnp.concatenate(out, axis=1)


def _rope_tables(n_tokens, dim):
    t = jnp.arange(n_tokens, dtype=jnp.int32)
    rows = (t // GRID_W).astype(F32)
    cols = (t % GRID_W).astype(F32)
    n_axis = dim // 4
    inv_freq = ROPE_BASE ** (-jnp.arange(n_axis, dtype=F32) / n_axis)
    ang = jnp.concatenate([rows[:, None] * inv_freq, cols[:, None] * inv_freq], axis=-1)
    return jnp.cos(ang), jnp.sin(ang)


def _rope_lane_tables(n_ctx, n_lat, dim):
    cos, sin = _rope_tables(n_lat, dim)
    reps = LANES // cos.shape[1]
    cos = jnp.concatenate([jnp.ones((n_ctx, LANES), F32), jnp.tile(cos, (1, reps))], axis=0)
    sin = jnp.concatenate([jnp.zeros((n_ctx, LANES), F32), jnp.tile(sin, (1, reps))], axis=0)
    return cos, sin


def _nbr_strip_start(tile, n_rows):
    lo = tile * NBR_TILE_ROWS - NA_ROWS // 2
    return jnp.clip(lo, 0, n_rows - NBR_STRIP_ROWS) if isinstance(lo, jax.Array) else int(
        np.clip(lo, 0, n_rows - NBR_STRIP_ROWS))


def _nbr_pattern(tile, n_tiles):
    if isinstance(tile, jax.Array):
        return jnp.where(tile == 0, 0, jnp.where(tile == n_tiles - 1, 2, 1))
    return 0 if tile == 0 else (2 if tile == n_tiles - 1 else 1)


def _neighbourhood_bias(rpb, n_ctx, n_rows):
    n_tiles = n_rows // NBR_TILE_ROWS
    assert n_tiles >= 3 and n_rows >= NBR_STRIP_ROWS

    def rows_of(tile):
        ss = _nbr_strip_start(tile, n_rows)
        r = tile * NBR_TILE_ROWS + np.arange(NBR_TILE_ROWS)[:, None]
        rs = np.clip(r - NA_ROWS // 2, 0, n_rows - NA_ROWS)
        kr = ss + np.arange(NBR_STRIP_ROWS)[None, :]
        valid = (kr >= rs) & (kr < rs + NA_ROWS)
        return valid, np.where(valid, kr - r + NA_ROWS - 1, 0)

    reps = [rows_of(t) for t in (0, 1, n_tiles - 1)]
    for t in range(n_tiles):
        v, ri = rows_of(t)
        assert np.array_equal(v, reps[_nbr_pattern(t, n_tiles)][0])
        assert np.array_equal(ri, reps[_nbr_pattern(t, n_tiles)][1])
    row_valid = np.stack([v for v, _ in reps])
    ri = np.stack([r for _, r in reps])
    oh_r = (ri[..., None] == np.arange(2 * NA_ROWS - 1)) & row_valid[..., None]
    cq = np.arange(GRID_W)
    col_start = np.clip(cq - NA_COLS // 2, 0, GRID_W - NA_COLS)
    col_valid = (cq[None, :] >= col_start[:, None]) & (cq[None, :] < col_start[:, None] + NA_COLS)
    ci = np.clip(cq[None, :] - cq[:, None] + NA_COLS - 1, 0, 2 * NA_COLS - 2)
    oh_c = np.arange(2 * NA_COLS - 1)[:, None, None] == ci[None]
    bias = jnp.einsum('hrc,pajr,cqw->hpaqjw', rpb.astype(F32), oh_r.astype(np.float32),
                      oh_c.astype(np.float32), precision=lax.Precision.HIGHEST)
    valid = row_valid[None, :, :, None, :, None] & col_valid[None, None, None, :, None, :]
    bias = jnp.where(valid, bias * LOG2E, NEG_INF)
    bias = bias.reshape(bias.shape[0], 3, TOK_TILE, NBR_STRIP_ROWS * GRID_W)
    return jnp.concatenate([jnp.zeros(bias.shape[:3] + (n_ctx,), F32), bias], axis=-1)


def _block_diag_ones(n, group):
    r = lax.broadcasted_iota(jnp.int32, (n, n), 0) // group
    c = lax.broadcasted_iota(jnp.int32, (n, n), 1) // group
    return jnp.where(r == c, 1.0, 0.0).astype(BF16)


def _dot(a, b):
    return jnp.dot(a, b, preferred_element_type=F32)


def _dot_nt(a, b):
    return lax.dot_general(a, b, (((1,), (1,)), ((), ())), preferred_element_type=F32)


def _split(a):
    hi = a.astype(BF16)
    return hi, (a - hi.astype(F32)).astype(BF16)


def _dot_hilo(a, b_bf16):
    hi, lo = _split(a)
    return _dot(hi, b_bf16) + _dot(lo, b_bf16)


def _lane_mask(cond):
    return jnp.where(cond, 1.0, 0.0).astype(BF16)


def _norm_modulate(x, gain, shift, scale):
    ms = jnp.mean(x * x, axis=-1, keepdims=True)
    h = (x * lax.rsqrt(ms + EPS)) * gain
    return h * (1.0 + scale) + shift


def _mod_spec(chunk, row0, tile):
    n_ctx_tiles = TOK_TILE // tile
    return pl.BlockSpec((1, 1, 1, 1, D_MODEL),
                        lambda b, i: (b, jnp.minimum((i + row0) // n_ctx_tiles, 1), chunk, 0, 0))


def _ada_body(c_ref, w_ref, b_ref, o_ref):
    c = c_ref[...]
    cs = (c * jax.nn.sigmoid(c)).astype(BF16)
    o_ref[0] = _dot(cs, w_ref[0].astype(BF16)) + b_ref[0]


def _ada(cc, w_ada, b_ada):
    depth, d, n = w_ada.shape
    tn = 512
    return pl.pallas_call(
        _ada_body,
        grid=(depth, n // tn),
        in_specs=[pl.BlockSpec(cc.shape, lambda l, j: (0, 0)),
                  pl.BlockSpec((1, d, tn), lambda l, j: (l, 0, j)),
                  pl.BlockSpec((1, 1, tn), lambda l, j: (l, 0, j))],
        out_specs=pl.BlockSpec((1, cc.shape[0], tn), lambda l, j: (l, 0, j)),
        out_shape=jax.ShapeDtypeStruct((depth, cc.shape[0], n), F32),
        compiler_params=_cparams(2),
        name="ada",
    )(cc, w_ada, b_ada.reshape(depth, 1, n))


def _proj_body(x_ref, sh_ref, sc_ref, g_ref, w_ref, ca_ref, sa_ref, cb_ref, sb_ref, gq_ref, gk_ref,
               o_ref):
    h = _norm_modulate(x_ref[0], g_ref[...], sh_ref[0, 0, 0], sc_ref[0, 0, 0])
    acc = _dot(h.astype(BF16), w_ref[...])
    ca, sa, cb, sb = ca_ref[...], sa_ref[...], cb_ref[...], sb_ref[...]
    bd = _block_diag_ones(LANES, 32)

    def get(m, j):
        c0 = (3 * m) * MIXER_W + j * LANES
        return acc[:, c0:c0 + LANES]

    def put(m, j, v):
        c0 = (3 * m) * MIXER_W + j * LANES
        o_ref[0, :, c0:c0 + LANES] = v.astype(BF16)

    def rope(x0, x1, c, s):
        return x0 * c - x1 * s, x0 * s + x1 * c

    def head_norm(x0, x1, g_ref_):
        gs = _dot_hilo(x0 * x0 + x1 * x1, bd)
        r = lax.rsqrt(gs * (1.0 / HEAD_DIM) + EPS)
        return x0 * r * g_ref_[0:1, :], x1 * r * g_ref_[1:2, :]

    for m in range(4):
        q0, q1, k0, k1 = get(m, 0), get(m, 1), get(m, 2), get(m, 3)
        if m == 0:
            q0, q1 = head_norm(q0, q1, gq_ref)
            k0, k1 = head_norm(k0, k1, gk_ref)
        if m in (0, 2):
            q0, q1 = rope(q0, q1, ca, sa)
            k0, k1 = rope(k0, k1, ca, sa)
        elif m == 1:
            q0, q1 = rope(q0, q1, cb, sb)
            k0, k1 = rope(k0, k1, cb, sb)
        qscale = LOG2E * (DIFF_DIM if m == 1 else HEAD_DIM) ** -0.5
        put(m, 0, q0 * qscale)
        put(m, 1, q1 * qscale)
        put(m, 2, k0)
        put(m, 3, k1)
        put(m, 4, get(m, 4))
        put(m, 5, get(m, 5))


def _proj(xall, mod, gain, w_perm, ca, sa, cb, sb, gq, gk):
    b, lt, d = xall.shape
    tm = TOK_TILE
    tab = pl.BlockSpec((tm, LANES), lambda b_, i: (i, 0))
    full = lambda a: pl.BlockSpec(a.shape, lambda b_, i: (0,) * a.ndim)
    return pl.pallas_call(
        _proj_body,
        grid=(b, lt // tm),
        in_specs=[pl.BlockSpec((1, tm, d), lambda b_, i: (b_, i, 0)),
                  _mod_spec(0, 0, tm), _mod_spec(1, 0, tm),
                  full(gain), full(w_perm), tab, tab, tab, tab, full(gq), full(gk)],
        out_specs=pl.BlockSpec((1, tm, QKV_W), lambda b_, i: (b_, i, 0)),
        out_shape=jax.ShapeDtypeStruct((b, lt, QKV_W), BF16),
        compiler_params=_cparams(2),
        name="proj",
    )(xall, mod, mod, gain, w_perm, ca, sa, cb, sb, gq, gk)


def _sum_lane(h):
    return (HEAD_DIM * (h + 1)) % MIXER_W


def _fill_head_values(v_ref, vm_ref):
    lane = lax.broadcasted_iota(jnp.int32, (1, MIXER_W), 1)
    v = v_ref[0]
    for h in range(N_HEADS):
        tap = jnp.where(lane == _sum_lane(h), 1.0, 0.0).astype(BF16)
        vm_ref[h] = jnp.where(lane // HEAD_DIM == h, v, tap)


def _attend(qm, kk, vm, bias=None, valid=None, extra_logit=None):
    s = _dot_nt(qm, kk)
    if bias is not None:
        s = s + bias
    if valid is not None:
        s = jnp.where(valid, s, NEG_INF)
    m = jnp.max(s, axis=-1, keepdims=True)
    if extra_logit is not None:
        m = jnp.maximum(m, extra_logit)
    return _dot(jnp.exp2(s - m).astype(BF16), vm), m


def _attn_global_body(*refs, row0, diff, lam_init, n_ctx):
    if diff:
        lam_ref, q_ref, k_ref, v_ref, gain_ref, o_ref, vm_ref = refs
    else:
        q_ref, k_ref, v_ref, o_ref, vm_ref = refs
    i = pl.program_id(1) + row0
    lane = lax.broadcasted_iota(jnp.int32, (1, MIXER_W), 1)

    @pl.when(pl.program_id(1) == 0)
    def _():
        _fill_head_values(v_ref, vm_ref)

    def run(nk):
        q = q_ref[0]
        k = k_ref[0, :nk, :]
        out = jnp.zeros((q.shape[0], MIXER_W), F32)
        for h in range(N_HEADS):
            vm = vm_ref[h, :nk, :]
            tap = _sum_lane(h)
            if diff:
                o = None
                for c in range(2):
                    qm = q * _lane_mask((lane % LANES) // (DIFF_DIM // 2) == 2 * h + c)
                    pv, _ = _attend(qm, k, vm)
                    coef = 1.0 if c == 0 else lam_ref[0]
                    t = pv * (coef / pv[:, tap:tap + 1])
                    o = t if c == 0 else o - t
            else:
                qm = q * _lane_mask((lane % LANES) // (HEAD_DIM // 2) == h)
                pv, _ = _attend(qm, k, vm)
                o = pv * (1.0 / pv[:, tap:tap + 1])
            out = jnp.where(lane // HEAD_DIM == h, o, out)
        if diff:
            gs = _dot_hilo(out * out, _block_diag_ones(MIXER_W, HEAD_DIM))
            out = out * lax.rsqrt(gs * (1.0 / HEAD_DIM) + EPS) * gain_ref[...] * (1.0 - lam_init)
        o_ref[0] = out.astype(BF16)

    if row0 == 0:
        @pl.when(i == 0)
        def _():
            run(n_ctx)

        @pl.when(i > 0)
        def _():
            run(k_ref.shape[1])
    else:
        run(k_ref.shape[1])


def _attn_global(qkv, mixer, row0, lam=None, gain=None, lam_init=0.0):
    b, lt, _ = qkv.shape
    tq = TOK_TILE
    diff = lam is not None
    kv_spec = lambda j: pl.BlockSpec((1, lt, MIXER_W), lambda b_, i: (b_, 0, 3 * mixer + j))
    in_specs = [pl.BlockSpec((1, tq, MIXER_W), lambda b_, i: (b_, i + row0, 3 * mixer)),
                kv_spec(1), kv_spec(2)]
    args = [qkv, qkv, qkv]
    if diff:
        in_specs = [pl.BlockSpec(memory_space=pltpu.SMEM)] + in_specs + [
            pl.BlockSpec(gain.shape, lambda b_, i: (0, 0))]
        args = [lam] + args + [gain]
    return pl.pallas_call(
        functools.partial(_attn_global_body, row0=row0, diff=diff, lam_init=lam_init, n_ctx=TOK_TILE),
        grid=(b, lt // tq - row0),
        in_specs=in_specs,
        out_specs=pl.BlockSpec((1, tq, MIXER_W), lambda b_, i: (b_, i, 0)),
        out_shape=jax.ShapeDtypeStruct((b, lt - row0 * tq, MIXER_W), BF16),
        scratch_shapes=[pltpu.VMEM((N_HEADS, lt, MIXER_W), BF16)],
        compiler_params=_cparams(2),
        name="attn_diff" if diff else "attn_global",
    )(*args)


def _attn_window_body(sink_ref, q_ref, k_ref, v_ref, o_ref, vm_ref, *, row0, n_ctx, n_lat):
    tq = q_ref.shape[1]
    band = tq + 2 * WINDOW
    n_ctx_tiles = n_ctx // tq
    i = pl.program_id(1) + row0
    lane = lax.broadcasted_iota(jnp.int32, (1, MIXER_W), 1)

    @pl.when(pl.program_id(1) == 0)
    def _():
        _fill_head_values(v_ref, vm_ref)

    def run(is_ctx):
        q = q_ref[0]
        kk = k_ref[0, :n_ctx, :]
        valid = None
        if not is_ctx:
            q_start = (i - n_ctx_tiles) * tq
            k_start = jnp.clip(q_start - WINDOW, 0, n_lat - band)
            ks = pl.multiple_of(n_ctx + k_start, LANES)
            kk = jnp.concatenate([kk, k_ref[0, pl.ds(ks, band), :]], axis=0)
            col = lax.broadcasted_iota(jnp.int32, (1, n_ctx + band), 1)
            row = lax.broadcasted_iota(jnp.int32, (tq, 1), 0)
            dist = (k_start + col - n_ctx) - (q_start + row)
            valid = jnp.where(col < n_ctx, 0, dist)
            valid = jnp.abs(valid) <= WINDOW
        out = jnp.zeros((tq, MIXER_W), F32)
        for h in range(N_HEADS):
            vm = vm_ref[h, :n_ctx, :]
            if not is_ctx:
                vm = jnp.concatenate([vm, vm_ref[h, pl.ds(ks, band), :]], axis=0)
            qm = q * _lane_mask((lane % LANES) // (HEAD_DIM // 2) == h)
            sink = sink_ref[h] * LOG2E
            pv, m = _attend(qm, kk, vm, valid=valid, extra_logit=sink)
            tap = _sum_lane(h)
            o = pv * (1.0 / (pv[:, tap:tap + 1] + jnp.exp2(sink - m)))
            out = jnp.where(lane // HEAD_DIM == h, o, out)
        o_ref[0] = out.astype(BF16)

    if row0 == 0:
        @pl.when(i < n_ctx_tiles)
        def _():
            run(True)

        @pl.when(i >= n_ctx_tiles)
        def _():
            run(False)
    else:
        run(False)


def _attn_window(qkv, mixer, row0_tiles, sink):
    b, lt, _ = qkv.shape
    tq = TOK_TILE
    row0 = row0_tiles * (TOK_TILE // tq)
    kv_spec = lambda j: pl.BlockSpec((1, lt, MIXER_W), lambda b_, i: (b_, 0, 3 * mixer + j))
    return pl.pallas_call(
        functools.partial(_attn_window_body, row0=row0, n_ctx=TOK_TILE, n_lat=lt - TOK_TILE),
        grid=(b, lt // tq - row0),
        in_specs=[pl.BlockSpec(memory_space=pltpu.SMEM),
                  pl.BlockSpec((1, tq, MIXER_W), lambda b_, i: (b_, i + row0, 3 * mixer)),
                  kv_spec(1), kv_spec(2)],
        out_specs=pl.BlockSpec((1, tq, MIXER_W), lambda b_, i: (b_, i, 0)),
        out_shape=jax.ShapeDtypeStruct((b, lt - row0 * tq, MIXER_W), BF16),
        scratch_shapes=[pltpu.VMEM((N_HEADS, lt, MIXER_W), BF16)],
        compiler_params=_cparams(2),
        name="attn_window",
    )(sink, qkv, qkv, qkv)


def _attn_nbr_body(q_ref, k_ref, v_ref, bias_ref, o_ref, vm_ref, *, row0, n_ctx, n_lat):
    tq = q_ref.shape[1]
    strip = NBR_STRIP_ROWS * GRID_W
    n_ctx_tiles = n_ctx // tq
    i = pl.program_id(1) + row0
    lane = lax.broadcasted_iota(jnp.int32, (1, MIXER_W), 1)

    @pl.when(pl.program_id(1) == 0)
    def _():
        _fill_head_values(v_ref, vm_ref)

    def run(is_ctx):
        q = q_ref[0]
        kk = k_ref[0, :n_ctx, :]
        if not is_ctx:
            rs = _nbr_strip_start(i - n_ctx_tiles, n_lat // GRID_W)
            ks = pl.multiple_of(n_ctx + rs * GRID_W, GRID_W)
            kk = jnp.concatenate([kk, k_ref[0, pl.ds(ks, strip), :]], axis=0)
        out = jnp.zeros((tq, MIXER_W), F32)
        for h in range(N_HEADS):
            vm = vm_ref[h, :n_ctx, :]
            if not is_ctx:
                vm = jnp.concatenate([vm, vm_ref[h, pl.ds(ks, strip), :]], axis=0)
            qm = q * _lane_mask((lane % LANES) // (HEAD_DIM // 2) == h)
            pv, _ = _attend(qm, kk, vm, bias=None if is_ctx else bias_ref[h, 0])
            tap = _sum_lane(h)
            out = jnp.where(lane // HEAD_DIM == h, pv * (1.0 / pv[:, tap:tap + 1]), out)
        o_ref[0] = out.astype(BF16)

    if row0 == 0:
        @pl.when(i < n_ctx_tiles)
        def _():
            run(True)

        @pl.when(i >= n_ctx_tiles)
        def _():
            run(False)
    else:
        run(False)


def _attn_nbr(qkv, mixer, row0_tiles, bias):
    b, lt, _ = qkv.shape
    tq = TOK_TILE
    n_ctx_tiles = TOK_TILE // tq
    row0 = row0_tiles * n_ctx_tiles
    n_tiles = (lt - TOK_TILE) // tq
    kv_spec = lambda j: pl.BlockSpec((1, lt, MIXER_W), lambda b_, i: (b_, 0, 3 * mixer + j))

    def bias_map(b_, i):
        return (0, _nbr_pattern(jnp.maximum(i + row0 - n_ctx_tiles, 0), n_tiles), 0, 0)

    return pl.pallas_call(
        functools.partial(_attn_nbr_body, row0=row0, n_ctx=TOK_TILE, n_lat=lt - TOK_TILE),
        grid=(b, lt // tq - row0),
        in_specs=[pl.BlockSpec((1, tq, MIXER_W), lambda b_, i: (b_, i + row0, 3 * mixer)),
                  kv_spec(1), kv_spec(2),
                  pl.BlockSpec((N_HEADS, 1) + bias.shape[2:], bias_map)],
        out_specs=pl.BlockSpec((1, tq, MIXER_W), lambda b_, i: (b_, i, 0)),
        out_shape=jax.ShapeDtypeStruct((b, lt - row0 * tq, MIXER_W), BF16),
        scratch_shapes=[pltpu.VMEM((N_HEADS, lt, MIXER_W), BF16)],
        compiler_params=_cparams(2),
        name="attn_nbr",
    )(qkv, qkv, qkv, bias)


def _outproj_body(oa_ref, ob_ref, oc_ref, od_ref, w_ref, x_ref, gate_ref, g_ref, out_ref):
    acc = None
    for m, o_ref in enumerate((oa_ref, ob_ref, oc_ref, od_ref)):
        t = _dot(o_ref[0], w_ref[m * MIXER_W:(m + 1) * MIXER_W, :])
        acc = t if acc is None else acc + t
    ms = jnp.mean(acc * acc, axis=-1, keepdims=True)
    y = (acc * lax.rsqrt(ms + EPS)) * g_ref[...]
    out_ref[0] = x_ref[0] + gate_ref[0, 0, 0] * y


def _outproj(o_mix, w_out, xall, mod, gain, row0):
    b, lt, d = xall.shape
    tm = TOK_TILE
    n_tiles = lt // tm - row0
    o_spec = pl.BlockSpec((1, tm, MIXER_W), lambda b_, i: (b_, i, 0))
    return pl.pallas_call(
        _outproj_body,
        grid=(b, n_tiles),
        in_specs=[o_spec, o_spec, o_spec, o_spec,
                  pl.BlockSpec(w_out.shape, lambda b_, i: (0, 0)),
                  pl.BlockSpec((1, tm, d), lambda b_, i: (b_, i + row0, 0)),
                  _mod_spec(2, row0, tm),
                  pl.BlockSpec(gain.shape, lambda b_, i: (0, 0))],
        out_specs=pl.BlockSpec((1, tm, d), lambda b_, i: (b_, i, 0)),
        out_shape=jax.ShapeDtypeStruct((b, n_tiles * tm, d), F32),
        compiler_params=_cparams(2),
        name="outproj",
    )(*o_mix, w_out, xall, mod, gain)


def _route_body(x_ref, sh_ref, sc_ref, g_ref, wr_ref, h_ref, lg_ref):
    h = _norm_modulate(x_ref[0], g_ref[...], sh_ref[0, 0, 0], sc_ref[0, 0, 0])
    h_ref[0] = h.astype(BF16)
    w_hi, w_lo = _split(wr_ref[...])
    h_hi, h_lo = _split(h)
    lg_ref[0] = _dot_nt(w_hi, h_hi) + (_dot_nt(w_hi, h_lo) + _dot_nt(w_lo, h_hi))


def _route(xs, mod, gain, w_router_t, row0):
    b, n_tok, d = xs.shape
    tm = TOK_TILE
    n_e = w_router_t.shape[0]
    return pl.pallas_call(
        _route_body,
        grid=(b, n_tok // tm),
        in_specs=[pl.BlockSpec((1, tm, d), lambda b_, i: (b_, i, 0)),
                  _mod_spec(3, row0, tm), _mod_spec(4, row0, tm),
                  pl.BlockSpec(gain.shape, lambda b_, i: (0, 0)),
                  pl.BlockSpec(w_router_t.shape, lambda b_, i: (0, 0))],
        out_specs=[pl.BlockSpec((1, tm, d), lambda b_, i: (b_, i, 0)),
                   pl.BlockSpec((1, n_e, tm), lambda b_, i: (b_, 0, i))],
        out_shape=[jax.ShapeDtypeStruct((b, n_tok, d), BF16),
                   jax.ShapeDtypeStruct((b, n_e, n_tok), F32)],
        compiler_params=_cparams(2),
        name="route",
    )(xs, mod, mod, gain, w_router_t)


def _exclusive_cumsum(x, tri):
    off = jnp.zeros((x.shape[0], 1), F32)
    outs = []
    for c in range(x.shape[1] // LANES):
        xc = x[:, c * LANES:(c + 1) * LANES]
        inc = _dot(xc.astype(BF16), tri)
        outs.append(inc - xc + off)
        off = off + inc[:, LANES - 1:LANES]
    return jnp.concatenate(outs, axis=1)


def _select_body(lg_ref, pos_ref, aff_ref, *, segments):
    r = lax.broadcasted_iota(jnp.int32, (LANES, LANES), 0)
    c = lax.broadcasted_iota(jnp.int32, (LANES, LANES), 1)
    tri = jnp.where(r <= c, 1.0, 0.0).astype(BF16)
    for t0, t, cap, slot0 in segments:
        lg = lg_ref[0, :, t0:t0 + t]
        e = jnp.exp(lg - jnp.max(lg, axis=0, keepdims=True))
        aff = e / jnp.sum(e, axis=0, keepdims=True)

        def count_above(thr):
            return jnp.sum(jnp.where(aff > thr, 1.0, 0.0), axis=1, keepdims=True)

        def unsettled(carry):
            return jnp.logical_and(carry[2] > 0.0, carry[3] < BISECT_MAX_STEPS)

        def bisect(carry):
            lo, hi, _, step = carry
            mid = 0.5 * (lo + hi)
            cnt = count_above(mid)
            new_lo = jnp.where(cnt >= cap, mid, lo)
            new_hi = jnp.where(cnt <= cap, mid, hi)
            moving = jnp.logical_and(new_lo < new_hi, jnp.logical_and(mid > lo, mid < hi))
            return new_lo, new_hi, jnp.sum(jnp.where(moving, 1.0, 0.0)), step + 1

        lo0 = jnp.full((aff.shape[0], 1), -1.0, F32)
        hi0 = jnp.max(aff, axis=1, keepdims=True)
        lo, hi, _, _ = lax.while_loop(unsettled, bisect, (lo0, hi0, jnp.float32(1.0), jnp.int32(0)))
        gt = aff > hi
        eq = jnp.logical_and(aff > lo, aff <= hi)
        need = cap - count_above(hi)
        eq_rank = _exclusive_cumsum(jnp.where(eq, 1.0, 0.0), tri)
        sel = jnp.logical_or(gt, jnp.logical_and(eq, eq_rank < need))
        slot = _exclusive_cumsum(jnp.where(sel, 1.0, 0.0), tri) + slot0
        pos_ref[0, :, t0:t0 + t] = jnp.where(sel, slot, -1.0).astype(jnp.int32)
        aff_ref[0, :, t0:t0 + t] = aff


def _select(logits, segments):
    b, n_e, lt = logits.shape
    spec = pl.BlockSpec((1, n_e, lt), lambda b_: (b_, 0, 0))
    return pl.pallas_call(
        functools.partial(_select_body, segments=segments),
        grid=(b,),
        in_specs=[spec],
        out_specs=[spec, spec],
        out_shape=[jax.ShapeDtypeStruct((b, n_e, lt), jnp.int32),
                   jax.ShapeDtypeStruct((b, n_e, lt), F32)],
        compiler_params=_cparams(1),
        name="select",
    )(logits)


def _experts_body(pos_ref, aff_ref, h_ref, wg_ref, wu_ref, wd_ref, y_ref, wg_s, wu_s, wd_s):
    @pl.when(pl.program_id(1) == 0)
    def _():
        wg_s[...] = wg_ref[0, 0].astype(BF16)
        wu_s[...] = wu_ref[0, 0].astype(BF16)
        wd_s[...] = wd_ref[0, 0].astype(BF16)

    n_slots = y_ref.shape[2]
    pos = pos_ref[0, 0]
    hit = lax.broadcasted_iota(jnp.int32, (n_slots, pos.shape[1]), 0) == pos
    xe = _dot(jnp.where(hit, 1.0, 0.0).astype(BF16), h_ref[0]).astype(BF16)
    gate = jnp.sum(jnp.where(hit, aff_ref[0, 0], 0.0), axis=1, keepdims=True)
    a = _dot(xe, wg_s[...])
    u = _dot(xe, wu_s[...])
    act = ((a * jax.nn.sigmoid(a)) * u).astype(BF16)
    y_ref[0, 0] = (_dot(act, wd_s[...]) * gate).astype(BF16)


def _experts(pos, aff, h, w_gate, w_up, w_down, layer, n_slots):
    b, n_e, n_tok = pos.shape
    d, f = w_gate.shape[2:]
    pos4 = pos.reshape(b, n_e, 1, n_tok)
    aff4 = aff.reshape(b, n_e, 1, n_tok)
    row_spec = pl.BlockSpec((1, 1, 1, n_tok), lambda e, b_: (b_, e, 0, 0))
    return pl.pallas_call(
        _experts_body,
        grid=(n_e, b),
        in_specs=[row_spec, row_spec,
                  pl.BlockSpec((1, n_tok, d), lambda e, b_: (b_, 0, 0)),
                  pl.BlockSpec((1, 1, d, f), lambda e, b_: (layer, e, 0, 0)),
                  pl.BlockSpec((1, 1, d, f), lambda e, b_: (layer, e, 0, 0)),
                  pl.BlockSpec((1, 1, f, d), lambda e, b_: (layer, e, 0, 0))],
        out_specs=pl.BlockSpec((1, 1, n_slots, d), lambda e, b_: (b_, e, 0, 0)),
        out_shape=jax.ShapeDtypeStruct((b, n_e, n_slots, d), BF16),
        scratch_shapes=[pltpu.VMEM((d, f), BF16), pltpu.VMEM((d, f), BF16), pltpu.VMEM((f, d), BF16)],
        compiler_params=_cparams(2),
        name="experts",
    )(pos4, aff4, h, w_gate, w_up, w_down)


def _combine_body(pos_ref, y_ref, x_ref, gate_ref, g_ref, out_ref):
    n_e, n_slots = y_ref.shape[1:3]
    pos = pos_ref[0]
    slot = lax.broadcasted_iota(jnp.int32, (n_slots, pos.shape[1]), 0)
    onehot = jnp.concatenate(
        [jnp.where(slot == pos[e:e + 1, :], 1.0, 0.0).astype(BF16) for e in range(n_e)], axis=0)
    acc = lax.dot_general(onehot, y_ref[0].reshape(n_e * n_slots, y_ref.shape[3]),
                          (((0,), (0,)), ((), ())), preferred_element_type=F32)
    ms = jnp.mean(acc * acc, axis=-1, keepdims=True)
    y = (acc * lax.rsqrt(ms + EPS)) * g_ref[...]
    out_ref[0] = x_ref[0] + gate_ref[0, 0, 0] * y


def _combine(pos, y, xs, mod, gain, row0):
    b, n_tok, d = xs.shape
    tm = TOK_TILE
    n_e, n_slots = y.shape[1:3]
    return pl.pallas_call(
        _combine_body,
        grid=(b, n_tok // tm),
        in_specs=[pl.BlockSpec((1, n_e, tm), lambda b_, i: (b_, 0, i)),
                  pl.BlockSpec((1, n_e, n_slots, d), lambda b_, i: (b_, 0, 0, 0)),
                  pl.BlockSpec((1, tm, d), lambda b_, i: (b_, i, 0)),
                  _mod_spec(5, row0, tm),
                  pl.BlockSpec(gain.shape, lambda b_, i: (0, 0))],
        out_specs=pl.BlockSpec((1, tm, d), lambda b_, i: (b_, i, 0)),
        out_shape=jax.ShapeDtypeStruct(xs.shape, F32),
        compiler_params=_cparams(2),
        name="combine",
    )(pos, y, xs, mod, gain)


def kernel(x, c, ctx, c_ctx, w_ada, b_ada, g_pre_mix, g_post_mix, g_pre_ffn, g_post_ffn, w_in, w_out, q_gain_a, k_gain_a, lam_q1, lam_k1, lam_q2, lam_k2, subln_gain_b, sink_c, rpb_d, w_router, w_gate, w_up, w_down):
    b, n_lat, d = x.shape
    n_ctx = ctx.shape[1]
    depth = w_ada.shape[0]
    assert n_ctx == TOK_TILE and n_lat % TOK_TILE == 0 and d == D_MODEL and b + 1 <= 16

    xall = jnp.concatenate([ctx, x], axis=1)
    cc = jnp.zeros((16, d), F32).at[:b].set(c).at[b].set(c_ctx)
    mod_all = _ada(cc, w_ada, b_ada)
    ca, sa = _rope_lane_tables(n_ctx, n_lat, HEAD_DIM)
    cb, sb = _rope_lane_tables(n_ctx, n_lat, DIFF_DIM)
    row = lambda v: v.reshape(1, -1)

    for l in range(depth):
        with_ctx = l < depth - 1
        row0 = 0 if with_ctx else 1
        lam_init = 0.8 - 0.6 * math.exp(-0.3 * l)
        m = mod_all[l]
        mod = jnp.stack([jnp.broadcast_to(m[b], (b, 6 * d)), m[:b]], axis=1).reshape(b, 2, 6, 1, d)

        w_perm = _permute_w_in(w_in[l].astype(BF16))
        gq = jnp.stack([jnp.tile(q_gain_a[l, 0::2], 4), jnp.tile(q_gain_a[l, 1::2], 4)])
        gk = jnp.stack([jnp.tile(k_gain_a[l, 0::2], 4), jnp.tile(k_gain_a[l, 1::2], 4)])
        qkv = _proj(xall, mod, row(g_pre_mix[l]), w_perm, ca, sa, cb, sb, gq, gk)

        lam = (jnp.exp(jnp.sum(lam_q1[l] * lam_k1[l])) - jnp.exp(jnp.sum(lam_q2[l] * lam_k2[l]))
               + lam_init).reshape(1)
        o_mix = (
            _attn_global(qkv, 0, row0),
            _attn_global(qkv, 1, row0, lam=lam, gain=row(jnp.tile(subln_gain_b[l], N_HEADS)),
                         lam_init=lam_init),
            _attn_window(qkv, 2, row0, sink_c[l]),
            _attn_nbr(qkv, 3, row0, _neighbourhood_bias(rpb_d[l], n_ctx, n_lat // GRID_W)),
        )
        xs = _outproj(o_mix, w_out[l].astype(BF16), xall, mod, row(g_post_mix[l]), row0)

        h, logits = _route(xs, mod, row(g_pre_ffn[l]), w_router[l].T, row0)
        cap_lat = CAPACITY_FACTOR * n_lat // N_EXPERTS
        cap_ctx = CAPACITY_FACTOR * n_ctx // N_EXPERTS
        if with_ctx:
            segments = ((0, n_ctx, cap_ctx, 0), (n_ctx, n_lat, cap_lat, cap_ctx))
            n_slots = cap_ctx + cap_lat
        else:
            segments = ((0, n_lat, cap_lat, 0),)
            n_slots = cap_lat
        pos, aff = _select(logits, segments)
        y = _experts(pos, aff, h, w_gate, w_up, w_down, l, n_slots)
        xall = _combine(pos, y, xs, mod, row(g_post_ffn[l]), row0)
    return xall
```

```python
import functools
import math

import numpy as np
import jax
import jax.numpy as jnp
from jax import lax
from jax.experimental import pallas as pl
from jax.experimental.pallas import tpu as pltpu

F32 = jnp.float32
BF16 = jnp.bfloat16

D_MODEL = 1024
GRID_W = 64
HEAD_DIM = 64
DIFF_DIM = 32
N_HEADS = 4
MIXER_W = N_HEADS * HEAD_DIM
QKV_W = 4 * 3 * MIXER_W
WINDOW = 128
NA_ROWS = 8
NA_COLS = 16
ROPE_BASE = 10000.0
N_EXPERTS = 16
CAPACITY_FACTOR = 2
EPS = 1e-6
NEG_INF = -1e30
LOG2E = 1.4426950408889634
PROJ_WIDTHS = (256, 128, 128, 256, 256, 256, 256, 128, 128, 256, 256, 256)
LANES = 128
TOK_TILE = 256
VMEM_LIMIT = 56 * 1024 * 1024
NBR_TILE_ROWS = TOK_TILE // GRID_W
NBR_STRIP_ROWS = 12
BISECT_MAX_STEPS = 192


def _cparams(n_axes):
    return pltpu.CompilerParams(dimension_semantics=("arbitrary",) * n_axes,
                                vmem_limit_bytes=VMEM_LIMIT)


def _proj_source_columns():
    offs = np.concatenate([[0], np.cumsum(PROJ_WIDTHS)])

    def pairs(base, n_slots, slot_src, n_pairs):
        x0 = [base + slot_src(j) + 2 * i for j in range(n_slots) for i in range(n_pairs)]
        return x0 + [c + 1 for c in x0]

    cols = []
    for m in range(4):
        q0, k0, v0 = offs[3 * m], offs[3 * m + 1], offs[3 * m + 2]
        if m in (0, 2):
            cols += pairs(q0, 4, lambda j: j * HEAD_DIM, 32)
            cols += pairs(k0, 4, lambda j: (j // 2) * HEAD_DIM, 32)
            cols += [v0 + (j // 2) * HEAD_DIM + d for j in range(4) for d in range(HEAD_DIM)]
        elif m == 1:
            cols += pairs(q0, 8, lambda j: j * DIFF_DIM, 16)
            cols += pairs(k0, 8, lambda j: j * DIFF_DIM, 16)
            cols += [v0 + c for c in range(MIXER_W)]
        else:
            cols += pairs(q0, 4, lambda j: j * HEAD_DIM, 32)
            cols += pairs(k0, 4, lambda j: j * HEAD_DIM, 32)
            cols += [v0 + c for c in range(MIXER_W)]
    return np.asarray(cols, dtype=np.int32)


def _permute_w_in(w_in):
    src = _proj_source_columns()
    pick = (jnp.arange(w_in.shape[-1], dtype=jnp.int32)[:, None] == src[None, :]).astype(BF16)
    return jnp.einsum('ldk,kn->ldn', w_in.astype(BF16), pick, preferred_element_type=BF16)


def _rope_tables(n_tokens, dim):
    t = jnp.arange(n_tokens, dtype=jnp.int32)
    rows = (t // GRID_W).astype(F32)
    cols = (t % GRID_W).astype(F32)
    n_axis = dim // 4
    inv_freq = ROPE_BASE ** (-jnp.arange(n_axis, dtype=F32) / n_axis)
    ang = jnp.concatenate([rows[:, None] * inv_freq, cols[:, None] * inv_freq], axis=-1)
    return jnp.cos(ang), jnp.sin(ang)


def _rope_lane_tables(n_ctx, n_lat, dim):
    cos, sin = _rope_tables(n_lat, dim)
    reps = LANES // cos.shape[1]
    cos = jnp.concatenate([jnp.ones((n_ctx, LANES), F32), jnp.tile(cos, (1, reps))], axis=0)
    sin = jnp.concatenate([jnp.zeros((n_ctx, LANES), F32), jnp.tile(sin, (1, reps))], axis=0)
    return cos, sin


def _nbr_strip_start(tile, n_rows):
    lo = tile * NBR_TILE_ROWS - NA_ROWS // 2
    return jnp.clip(lo, 0, n_rows - NBR_STRIP_ROWS) if isinstance(lo, jax.Array) else int(
        np.clip(lo, 0, n_rows - NBR_STRIP_ROWS))


def _nbr_pattern(tile, n_tiles):
    if isinstance(tile, jax.Array):
        return jnp.where(tile == 0, 0, jnp.where(tile == n_tiles - 1, 2, 1))
    return 0 if tile == 0 else (2 if tile == n_tiles - 1 else 1)


def _neighbourhood_bias(rpb, n_ctx, n_rows):
    n_tiles = n_rows // NBR_TILE_ROWS
    assert n_tiles >= 3 and n_rows >= NBR_STRIP_ROWS

    def rows_of(tile):
        ss = _nbr_strip_start(tile, n_rows)
        r = tile * NBR_TILE_ROWS + np.arange(NBR_TILE_ROWS)[:, None]
        rs = np.clip(r - NA_ROWS // 2, 0, n_rows - NA_ROWS)
        kr = ss + np.arange(NBR_STRIP_ROWS)[None, :]
        valid = (kr >= rs) & (kr < rs + NA_ROWS)
        return valid, np.where(valid, kr - r + NA_ROWS - 1, 0)

    reps = [rows_of(t) for t in (0, 1, n_tiles - 1)]
    for t in range(n_tiles):
        v, ri = rows_of(t)
        assert np.array_equal(v, reps[_nbr_pattern(t, n_tiles)][0])
        assert np.array_equal(ri, reps[_nbr_pattern(t, n_tiles)][1])
    row_valid = np.stack([v for v, _ in reps])
    ri = np.stack([r for _, r in reps])
    oh_r = (ri[..., None] == np.arange(2 * NA_ROWS - 1)) & row_valid[..., None]
    cq = np.arange(GRID_W)
    col_start = np.clip(cq - NA_COLS // 2, 0, GRID_W - NA_COLS)
    col_valid = (cq[None, :] >= col_start[:, None]) & (cq[None, :] < col_start[:, None] + NA_COLS)
    ci = np.clip(cq[None, :] - cq[:, None] + NA_COLS - 1, 0, 2 * NA_COLS - 2)
    oh_c = np.arange(2 * NA_COLS - 1)[:, None, None] == ci[None]
    bias = jnp.einsum('lhrc,pajr,cqw->lhpaqjw', rpb.astype(F32), oh_r.astype(np.float32),
                      oh_c.astype(np.float32), precision=lax.Precision.HIGHEST)
    valid = row_valid[None, None, :, :, None, :, None] & col_valid[None, None, None, None, :, None, :]
    bias = jnp.where(valid, bias * LOG2E, NEG_INF)
    bias = bias.reshape(bias.shape[:2] + (3, TOK_TILE, NBR_STRIP_ROWS * GRID_W))
    return jnp.concatenate([jnp.zeros(bias.shape[:4] + (n_ctx,), F32), bias], axis=-1)


def _block_diag_ones(n, group):
    r = lax.broadcasted_iota(jnp.int32, (n, n), 0) // group
    c = lax.broadcasted_iota(jnp.int32, (n, n), 1) // group
    return jnp.where(r == c, 1.0, 0.0).astype(BF16)


def _dot(a, b):
    return jnp.dot(a, b, preferred_element_type=F32)


def _dot_nt(a, b):
    return lax.dot_general(a, b, (((1,), (1,)), ((), ())), preferred_element_type=F32)


def _split(a):
    hi = a.astype(BF16)
    return hi, (a - hi.astype(F32)).astype(BF16)


def _dot_hilo(a, b_bf16):
    hi, lo = _split(a)
    return _dot(hi, b_bf16) + _dot(lo, b_bf16)


def _lane_mask(cond):
    return jnp.where(cond, 1.0, 0.0).astype(BF16)


def _norm_modulate(x, gain, shift, scale):
    ms = jnp.mean(x * x, axis=-1, keepdims=True)
    h = (x * lax.rsqrt(ms + EPS)) * gain
    return h * (1.0 + scale) + shift


def _mod_spec(chunk, row0, tile):
    n_ctx_tiles = TOK_TILE // tile
    return pl.BlockSpec((1, 1, 1, 1, D_MODEL),
                        lambda b, i: (b, jnp.minimum((i + row0) // n_ctx_tiles, 1), chunk, 0, 0))


def _ada_body(c_ref, w_ref, b_ref, o_ref):
    c = c_ref[...]
    cs = (c * jax.nn.sigmoid(c)).astype(BF16)
    o_ref[0] = _dot(cs, w_ref[0].astype(BF16)) + b_ref[0]


def _ada(cc, w_ada, b_ada):
    depth, d, n = w_ada.shape
    tn = 512
    return pl.pallas_call(
        _ada_body,
        grid=(depth, n // tn),
        in_specs=[pl.BlockSpec(cc.shape, lambda l, j: (0, 0)),
                  pl.BlockSpec((1, d, tn), lambda l, j: (l, 0, j)),
                  pl.BlockSpec((1, 1, tn), lambda l, j: (l, 0, j))],
        out_specs=pl.BlockSpec((1, cc.shape[0], tn), lambda l, j: (l, 0, j)),
        out_shape=jax.ShapeDtypeStruct((depth, cc.shape[0], n), F32),
        compiler_params=_cparams(2),
        name="ada",
    )(cc, w_ada, b_ada.reshape(depth, 1, n))


def _stream_specs(stream, row0):
    ctx_arr, lat_arr = stream if isinstance(stream, tuple) else (stream, stream)
    lat_tile0 = 0 if isinstance(stream, tuple) else 1
    blk = (1, TOK_TILE, lat_arr.shape[-1])
    lat_spec = pl.BlockSpec(blk, lambda b_, i: (b_, jnp.maximum(i + row0 - 1, 0) + lat_tile0, 0))
    if row0 > 0:
        return [lat_spec], [lat_arr]
    return [pl.BlockSpec(blk, lambda b_, i: (b_, 0, 0)), lat_spec], [ctx_arr, lat_arr]


def _stream_tile(x_refs):
    if len(x_refs) == 1:
        return x_refs[0][0]
    return jnp.where(pl.program_id(1) == 0, x_refs[0][0], x_refs[1][0])


def _proj_body(xc_ref, xl_ref, sh_ref, sc_ref, g_ref, w_ref, ca_ref, sa_ref, cb_ref, sb_ref, gq_ref,
               gk_ref, o_ref):
    h = _norm_modulate(_stream_tile((xc_ref, xl_ref)), g_ref[...], sh_ref[0, 0, 0], sc_ref[0, 0, 0])
    acc = _dot(h.astype(BF16), w_ref[...])
    ca, sa, cb, sb = ca_ref[...], sa_ref[...], cb_ref[...], sb_ref[...]
    bd = _block_diag_ones(LANES, 32)

    def get(m, j):
        c0 = (3 * m) * MIXER_W + j * LANES
        return acc[:, c0:c0 + LANES]

    def put(m, j, v):
        c0 = (3 * m) * MIXER_W + j * LANES
        o_ref[0, :, c0:c0 + LANES] = v.astype(BF16)

    def rope(x0, x1, c, s):
        return x0 * c - x1 * s, x0 * s + x1 * c

    def head_norm(x0, x1, g_ref_):
        gs = _dot_hilo(x0 * x0 + x1 * x1, bd)
        r = lax.rsqrt(gs * (1.0 / HEAD_DIM) + EPS)
        return x0 * r * g_ref_[0:1, :], x1 * r * g_ref_[1:2, :]

    for m in range(4):
        q0, q1, k0, k1 = get(m, 0), get(m, 1), get(m, 2), get(m, 3)
        if m == 0:
            q0, q1 = head_norm(q0, q1, gq_ref)
            k0, k1 = head_norm(k0, k1, gk_ref)
        if m in (0, 2):
            q0, q1 = rope(q0, q1, ca, sa)
            k0, k1 = rope(k0, k1, ca, sa)
        elif m == 1:
            q0, q1 = rope(q0, q1, cb, sb)
            k0, k1 = rope(k0, k1, cb, sb)
        qscale = LOG2E * (DIFF_DIM if m == 1 else HEAD_DIM) ** -0.5
        put(m, 0, q0 * qscale)
        put(m, 1, q1 * qscale)
        put(m, 2, k0)
        put(m, 3, k1)
        put(m, 4, get(m, 4))
        put(m, 5, get(m, 5))


def _proj(stream, mod, gain, w_perm, ca, sa, cb, sb, gq, gk):
    x_specs, x_args = _stream_specs(stream, 0)
    b = x_args[0].shape[0]
    lt = ca.shape[0]
    tm = TOK_TILE
    tab = pl.BlockSpec((tm, LANES), lambda b_, i: (i, 0))
    full = lambda a: pl.BlockSpec(a.shape, lambda b_, i: (0,) * a.ndim)
    return pl.pallas_call(
        _proj_body,
        grid=(b, lt // tm),
        in_specs=x_specs + [_mod_spec(0, 0, tm), _mod_spec(1, 0, tm),
                            full(gain), full(w_perm), tab, tab, tab, tab, full(gq), full(gk)],
        out_specs=pl.BlockSpec((1, tm, QKV_W), lambda b_, i: (b_, i, 0)),
        out_shape=jax.ShapeDtypeStruct((b, lt, QKV_W), BF16),
        compiler_params=_cparams(2),
        name="proj",
    )(*x_args, mod, mod, gain, w_perm, ca, sa, cb, sb, gq, gk)


def _sum_lane(h):
    return (HEAD_DIM * (h + 1)) % MIXER_W


def _fill_head_values(v_ref, vm_ref):
    lane = lax.broadcasted_iota(jnp.int32, (1, MIXER_W), 1)
    v = v_ref[0]
    for h in range(N_HEADS):
        tap = jnp.where(lane == _sum_lane(h), 1.0, 0.0).astype(BF16)
        vm_ref[h] = jnp.where(lane // HEAD_DIM == h, v, tap)


def _attend(qm, kk, vm, bias=None, valid=None, extra_logit=None):
    s = _dot_nt(qm, kk)
    if bias is not None:
        s = s + bias
    if valid is not None:
        s = jnp.where(valid, s, NEG_INF)
    m = jnp.max(s, axis=-1, keepdims=True)
    if extra_logit is not None:
        m = jnp.maximum(m, extra_logit)
    return _dot(jnp.exp2(s - m).astype(BF16), vm), m


def _attn_global_body(*refs, row0, diff, lam_init, n_ctx):
    if diff:
        lam_ref, q_ref, k_ref, v_ref, gain_ref, o_ref, vm_ref = refs
    else:
        q_ref, k_ref, v_ref, o_ref, vm_ref = refs
    i = pl.program_id(1) + row0
    lane = lax.broadcasted_iota(jnp.int32, (1, MIXER_W), 1)

    @pl.when(pl.program_id(1) == 0)
    def _():
        _fill_head_values(v_ref, vm_ref)

    def run(nk):
        q = q_ref[0]
        k = k_ref[0, :nk, :]
        out = jnp.zeros((q.shape[0], MIXER_W), F32)
        for h in range(N_HEADS):
            vm = vm_ref[h, :nk, :]
            tap = _sum_lane(h)
            if diff:
                o = None
                for c in range(2):
                    qm = q * _lane_mask((lane % LANES) // (DIFF_DIM // 2) == 2 * h + c)
                    pv, _ = _attend(qm, k, vm)
                    coef = 1.0 if c == 0 else lam_ref[0]
                    t = pv * (coef / pv[:, tap:tap + 1])
                    o = t if c == 0 else o - t
            else:
                qm = q * _lane_mask((lane % LANES) // (HEAD_DIM // 2) == h)
                pv, _ = _attend(qm, k, vm)
                o = pv * (1.0 / pv[:, tap:tap + 1])
            out = jnp.where(lane // HEAD_DIM == h, o, out)
        if diff:
            gs = _dot_hilo(out * out, _block_diag_ones(MIXER_W, HEAD_DIM))
            out = out * lax.rsqrt(gs * (1.0 / HEAD_DIM) + EPS) * gain_ref[...] * (1.0 - lam_init)
        o_ref[0] = out.astype(BF16)

    if row0 == 0:
        @pl.when(i == 0)
        def _():
            run(n_ctx)

        @pl.when(i > 0)
        def _():
            run(k_ref.shape[1])
    else:
        run(k_ref.shape[1])


def _attn_global(qkv, mixer, row0, lam=None, gain=None, lam_init=0.0):
    b, lt, _ = qkv.shape
    tq = TOK_TILE
    diff = lam is not None
    kv_spec = lambda j: pl.BlockSpec((1, lt, MIXER_W), lambda b_, i: (b_, 0, 3 * mixer + j))
    in_specs = [pl.BlockSpec((1, tq, MIXER_W), lambda b_, i: (b_, i + row0, 3 * mixer)),
                kv_spec(1), kv_spec(2)]
    args = [qkv, qkv, qkv]
    if diff:
        in_specs = [pl.BlockSpec(memory_space=pltpu.SMEM)] + in_specs + [
            pl.BlockSpec(gain.shape, lambda b_, i: (0, 0))]
        args = [lam] + args + [gain]
    return pl.pallas_call(
        functools.partial(_attn_global_body, row0=row0, diff=diff, lam_init=lam_init, n_ctx=TOK_TILE),
        grid=(b, lt // tq - row0),
        in_specs=in_specs,
        out_specs=pl.BlockSpec((1, tq, MIXER_W), lambda b_, i: (b_, i, 0)),
        out_shape=jax.ShapeDtypeStruct((b, lt - row0 * tq, MIXER_W), BF16),
        scratch_shapes=[pltpu.VMEM((N_HEADS, lt, MIXER_W), BF16)],
        compiler_params=_cparams(2),
        name="attn_diff" if diff else "attn_global",
    )(*args)


def _attn_window_body(sink_ref, q_ref, k_ref, v_ref, o_ref, vm_ref, *, row0, n_ctx, n_lat):
    tq = q_ref.shape[1]
    band = tq + 2 * WINDOW
    n_ctx_tiles = n_ctx // tq
    i = pl.program_id(1) + row0
    lane = lax.broadcasted_iota(jnp.int32, (1, MIXER_W), 1)

    @pl.when(pl.program_id(1) == 0)
    def _():
        _fill_head_values(v_ref, vm_ref)

    def run(is_ctx):
        q = q_ref[0]
        kk = k_ref[0, :n_ctx, :]
        valid = None
        if not is_ctx:
            q_start = (i - n_ctx_tiles) * tq
            k_start = jnp.clip(q_start - WINDOW, 0, n_lat - band)
            ks = pl.multiple_of(n_ctx + k_start, LANES)
            kk = jnp.concatenate([kk, k_ref[0, pl.ds(ks, band), :]], axis=0)
            col = lax.broadcasted_iota(jnp.int32, (1, n_ctx + band), 1)
            row = lax.broadcasted_iota(jnp.int32, (tq, 1), 0)
            dist = (k_start + col - n_ctx) - (q_start + row)
            valid = jnp.where(col < n_ctx, 0, dist)
            valid = jnp.abs(valid) <= WINDOW
        out = jnp.zeros((tq, MIXER_W), F32)
        for h in range(N_HEADS):
            vm = vm_ref[h, :n_ctx, :]
            if not is_ctx:
                vm = jnp.concatenate([vm, vm_ref[h, pl.ds(ks, band), :]], axis=0)
            qm = q * _lane_mask((lane % LANES) // (HEAD_DIM // 2) == h)
            sink = sink_ref[h] * LOG2E
            pv, m = _attend(qm, kk, vm, valid=valid, extra_logit=sink)
            tap = _sum_lane(h)
            o = pv * (1.0 / (pv[:, tap:tap + 1] + jnp.exp2(sink - m)))
            out = jnp.where(lane // HEAD_DIM == h, o, out)
        o_ref[0] = out.astype(BF16)

    if row0 == 0:
        @pl.when(i < n_ctx_tiles)
        def _():
            run(True)

        @pl.when(i >= n_ctx_tiles)
        def _():
            run(False)
    else:
        run(False)


def _attn_window(qkv, mixer, row0_tiles, sink):
    b, lt, _ = qkv.shape
    tq = TOK_TILE
    row0 = row0_tiles * (TOK_TILE // tq)
    kv_spec = lambda j: pl.BlockSpec((1, lt, MIXER_W), lambda b_, i: (b_, 0, 3 * mixer + j))
    return pl.pallas_call(
        functools.partial(_attn_window_body, row0=row0, n_ctx=TOK_TILE, n_lat=lt - TOK_TILE),
        grid=(b, lt // tq - row0),
        in_specs=[pl.BlockSpec(memory_space=pltpu.SMEM),
                  pl.BlockSpec((1, tq, MIXER_W), lambda b_, i: (b_, i + row0, 3 * mixer)),
                  kv_spec(1), kv_spec(2)],
        out_specs=pl.BlockSpec((1, tq, MIXER_W), lambda b_, i: (b_, i, 0)),
        out_shape=jax.ShapeDtypeStruct((b, lt - row0 * tq, MIXER_W), BF16),
        scratch_shapes=[pltpu.VMEM((N_HEADS, lt, MIXER_W), BF16)],
        compiler_params=_cparams(2),
        name="attn_window",
    )(sink, qkv, qkv, qkv)


def _attn_nbr_body(q_ref, k_ref, v_ref, bias_ref, o_ref, vm_ref, *, row0, n_ctx, n_lat):
    tq = q_ref.shape[1]
    strip = NBR_STRIP_ROWS * GRID_W
    n_ctx_tiles = n_ctx // tq
    i = pl.program_id(1) + row0
    lane = lax.broadcasted_iota(jnp.int32, (1, MIXER_W), 1)

    @pl.when(pl.program_id(1) == 0)
    def _():
        _fill_head_values(v_ref, vm_ref)

    def run(is_ctx):
        q = q_ref[0]
        kk = k_ref[0, :n_ctx, :]
        if not is_ctx:
            rs = _nbr_strip_start(i - n_ctx_tiles, n_lat // GRID_W)
            ks = pl.multiple_of(n_ctx + rs * GRID_W, GRID_W)
            kk = jnp.concatenate([kk, k_ref[0, pl.ds(ks, strip), :]], axis=0)
        out = jnp.zeros((tq, MIXER_W), F32)
        for h in range(N_HEADS):
            vm = vm_ref[h, :n_ctx, :]
            if not is_ctx:
                vm = jnp.concatenate([vm, vm_ref[h, pl.ds(ks, strip), :]], axis=0)
            qm = q * _lane_mask((lane % LANES) // (HEAD_DIM // 2) == h)
            pv, _ = _attend(qm, kk, vm, bias=None if is_ctx else bias_ref[0, h, 0])
            tap = _sum_lane(h)
            out = jnp.where(lane // HEAD_DIM == h, pv * (1.0 / pv[:, tap:tap + 1]), out)
        o_ref[0] = out.astype(BF16)

    if row0 == 0:
        @pl.when(i < n_ctx_tiles)
        def _():
            run(True)

        @pl.when(i >= n_ctx_tiles)
        def _():
            run(False)
    else:
        run(False)


def _attn_nbr(qkv, mixer, row0_tiles, bias, layer):
    b, lt, _ = qkv.shape
    tq = TOK_TILE
    n_ctx_tiles = TOK_TILE // tq
    row0 = row0_tiles * n_ctx_tiles
    n_tiles = (lt - TOK_TILE) // tq
    kv_spec = lambda j: pl.BlockSpec((1, lt, MIXER_W), lambda b_, i: (b_, 0, 3 * mixer + j))

    def bias_map(b_, i):
        return (layer, 0, _nbr_pattern(jnp.maximum(i + row0 - n_ctx_tiles, 0), n_tiles), 0, 0)

    return pl.pallas_call(
        functools.partial(_attn_nbr_body, row0=row0, n_ctx=TOK_TILE, n_lat=lt - TOK_TILE),
        grid=(b, lt // tq - row0),
        in_specs=[pl.BlockSpec((1, tq, MIXER_W), lambda b_, i: (b_, i + row0, 3 * mixer)),
                  kv_spec(1), kv_spec(2),
                  pl.BlockSpec((1, N_HEADS, 1) + bias.shape[3:], bias_map)],
        out_specs=pl.BlockSpec((1, tq, MIXER_W), lambda b_, i: (b_, i, 0)),
        out_shape=jax.ShapeDtypeStruct((b, lt - row0 * tq, MIXER_W), BF16),
        scratch_shapes=[pltpu.VMEM((N_HEADS, lt, MIXER_W), BF16)],
        compiler_params=_cparams(2),
        name="attn_nbr",
    )(qkv, qkv, qkv, bias)


def _outproj_body(oa_ref, ob_ref, oc_ref, od_ref, w_ref, gate_ref, g_ref, *rest):
    x_refs, out_ref = rest[:-1], rest[-1]
    acc = None
    for m, o_ref in enumerate((oa_ref, ob_ref, oc_ref, od_ref)):
        t = _dot(o_ref[0], w_ref[m * MIXER_W:(m + 1) * MIXER_W, :])
        acc = t if acc is None else acc + t
    ms = jnp.mean(acc * acc, axis=-1, keepdims=True)
    y = (acc * lax.rsqrt(ms + EPS)) * g_ref[...]
    out_ref[0] = _stream_tile(x_refs) + gate_ref[0, 0, 0] * y


def _outproj(o_mix, w_out, stream, mod, gain, row0):
    x_specs, x_args = _stream_specs(stream, row0)
    b, n_tok, _ = o_mix[0].shape
    d = x_args[0].shape[-1]
    tm = TOK_TILE
    o_spec = pl.BlockSpec((1, tm, MIXER_W), lambda b_, i: (b_, i, 0))
    return pl.pallas_call(
        _outproj_body,
        grid=(b, n_tok // tm),
        in_specs=[o_spec, o_spec, o_spec, o_spec,
                  pl.BlockSpec(w_out.shape, lambda b_, i: (0, 0)),
                  _mod_spec(2, row0, tm),
                  pl.BlockSpec(gain.shape, lambda b_, i: (0, 0))] + x_specs,
        out_specs=pl.BlockSpec((1, tm, d), lambda b_, i: (b_, i, 0)),
        out_shape=jax.ShapeDtypeStruct((b, n_tok, d), F32),
        compiler_params=_cparams(2),
        name="outproj",
    )(*o_mix, w_out, mod, gain, *x_args)


def _route_body(x_ref, sh_ref, sc_ref, g_ref, wr_ref, h_ref, lg_ref):
    h = _norm_modulate(x_ref[0], g_ref[...], sh_ref[0, 0, 0], sc_ref[0, 0, 0])
    h_ref[0] = h.astype(BF16)
    w_hi, w_lo = _split(wr_ref[...])
    h_hi, h_lo = _split(h)
    lg_ref[0] = _dot_nt(w_hi, h_hi) + (_dot_nt(w_hi, h_lo) + _dot_nt(w_lo, h_hi))


def _route(xs, mod, gain, w_router_t, row0):
    b, n_tok, d = xs.shape
    tm = TOK_TILE
    n_e = w_router_t.shape[0]
    return pl.pallas_call(
        _route_body,
        grid=(b, n_tok // tm),
        in_specs=[pl.BlockSpec((1, tm, d), lambda b_, i: (b_, i, 0)),
                  _mod_spec(3, row0, tm), _mod_spec(4, row0, tm),
                  pl.BlockSpec(gain.shape, lambda b_, i: (0, 0)),
                  pl.BlockSpec(w_router_t.shape, lambda b_, i: (0, 0))],
        out_specs=[pl.BlockSpec((1, tm, d), lambda b_, i: (b_, i, 0)),
                   pl.BlockSpec((1, n_e, tm), lambda b_, i: (b_, 0, i))],
        out_shape=[jax.ShapeDtypeStruct((b, n_tok, d), BF16),
                   jax.ShapeDtypeStruct((b, n_e, n_tok), F32)],
        compiler_params=_cparams(2),
        name="route",
    )(xs, mod, mod, gain, w_router_t)


def _exclusive_cumsum(x, tri):
    off = jnp.zeros((x.shape[0], 1), F32)
    outs = []
    for c in range(x.shape[1] // LANES):
        xc = x[:, c * LANES:(c + 1) * LANES]
        inc = _dot(xc.astype(BF16), tri)
        outs.append(inc - xc + off)
        off = off + inc[:, LANES - 1:LANES]
    return jnp.concatenate(outs, axis=1)


def _select_body(lg_ref, pos_ref, aff_ref, *, segments):
    r = lax.broadcasted_iota(jnp.int32, (LANES, LANES), 0)
    c = lax.broadcasted_iota(jnp.int32, (LANES, LANES), 1)
    tri = jnp.where(r <= c, 1.0, 0.0).astype(BF16)
    for t0, t, cap, slot0 in segments:
        lg = lg_ref[0, :, t0:t0 + t]
        e = jnp.exp(lg - jnp.max(lg, axis=0, keepdims=True))
        aff = e / jnp.sum(e, axis=0, keepdims=True)

        def count_above(thr):
            return jnp.sum(jnp.where(aff > thr, 1.0, 0.0), axis=1, keepdims=True)

        def unsettled(carry):
            return jnp.logical_and(carry[2] > 0.0, carry[3] < BISECT_MAX_STEPS)

        def bisect(carry):
            lo, hi, _, step = carry
            mid = 0.5 * (lo + hi)
            cnt = count_above(mid)
            new_lo = jnp.where(cnt >= cap, mid, lo)
            new_hi = jnp.where(cnt <= cap, mid, hi)
            moving = jnp.logical_and(new_lo < new_hi, jnp.logical_and(mid > lo, mid < hi))
            return new_lo, new_hi, jnp.sum(jnp.where(moving, 1.0, 0.0)), step + 1

        lo0 = jnp.full((aff.shape[0], 1), -1.0, F32)
        hi0 = jnp.max(aff, axis=1, keepdims=True)
        lo, hi, _, _ = lax.while_loop(unsettled, bisect, (lo0, hi0, jnp.float32(1.0), jnp.int32(0)))
        gt = aff > hi
        eq = jnp.logical_and(aff > lo, aff <= hi)
        need = cap - count_above(hi)
        eq_rank = _exclusive_cumsum(jnp.where(eq, 1.0, 0.0), tri)
        sel = jnp.logical_or(gt, jnp.logical_and(eq, eq_rank < need))
        slot = _exclusive_cumsum(jnp.where(sel, 1.0, 0.0), tri) + slot0
        pos_ref[0, :, t0:t0 + t] = jnp.where(sel, slot, -1.0).astype(jnp.int32)
        aff_ref[0, :, t0:t0 + t] = aff


def _select(logits, segments):
    b, n_e, lt = logits.shape
    spec = pl.BlockSpec((1, n_e, lt), lambda b_: (b_, 0, 0))
    return pl.pallas_call(
        functools.partial(_select_body, segments=segments),
        grid=(b,),
        in_specs=[spec],
        out_specs=[spec, spec],
        out_shape=[jax.ShapeDtypeStruct((b, n_e, lt), jnp.int32),
                   jax.ShapeDtypeStruct((b, n_e, lt), F32)],
        compiler_params=_cparams(1),
        name="select",
    )(logits)


def _experts_body(pos_ref, aff_ref, h_ref, wg_ref, wu_ref, wd_ref, y_ref, wg_s, wu_s, wd_s):
    @pl.when(pl.program_id(1) == 0)
    def _():
        wg_s[...] = wg_ref[0, 0].astype(BF16)
        wu_s[...] = wu_ref[0, 0].astype(BF16)
        wd_s[...] = wd_ref[0, 0].astype(BF16)

    n_slots = y_ref.shape[2]
    e = pl.program_id(0)
    pos = pos_ref[0, pl.ds(e, 1), :]
    hit = lax.broadcasted_iota(jnp.int32, (n_slots, pos.shape[1]), 0) == pos
    xe = _dot(jnp.where(hit, 1.0, 0.0).astype(BF16), h_ref[0]).astype(BF16)
    gate = jnp.sum(jnp.where(hit, aff_ref[0, pl.ds(e, 1), :], 0.0), axis=1, keepdims=True)
    a = _dot(xe, wg_s[...])
    u = _dot(xe, wu_s[...])
    act = ((a * jax.nn.sigmoid(a)) * u).astype(BF16)
    y_ref[0, 0] = (_dot(act, wd_s[...]) * gate).astype(BF16)


def _experts(pos, aff, h, w_gate, w_up, w_down, layer, n_slots):
    b, n_e, n_tok = pos.shape
    d, f = w_gate.shape[2:]
    row_spec = pl.BlockSpec((1, n_e, n_tok), lambda e, b_: (b_, 0, 0))
    return pl.pallas_call(
        _experts_body,
        grid=(n_e, b),
        in_specs=[row_spec, row_spec,
                  pl.BlockSpec((1, n_tok, d), lambda e, b_: (b_, 0, 0)),
                  pl.BlockSpec((1, 1, d, f), lambda e, b_: (layer, e, 0, 0)),
                  pl.BlockSpec((1, 1, d, f), lambda e, b_: (layer, e, 0, 0)),
                  pl.BlockSpec((1, 1, f, d), lambda e, b_: (layer, e, 0, 0))],
        out_specs=pl.BlockSpec((1, 1, n_slots, d), lambda e, b_: (b_, e, 0, 0)),
        out_shape=jax.ShapeDtypeStruct((b, n_e, n_slots, d), BF16),
        scratch_shapes=[pltpu.VMEM((d, f), BF16), pltpu.VMEM((d, f), BF16), pltpu.VMEM((f, d), BF16)],
        compiler_params=_cparams(2),
        name="experts",
    )(pos, aff, h, w_gate, w_up, w_down)


def _combine_body(pos_ref, y_ref, x_ref, gate_ref, g_ref, out_ref):
    n_e, n_slots = y_ref.shape[1:3]
    pos = pos_ref[0]
    slot = lax.broadcasted_iota(jnp.int32, (n_slots, pos.shape[1]), 0)
    onehot = jnp.concatenate(
        [jnp.where(slot == pos[e:e + 1, :], 1.0, 0.0).astype(BF16) for e in range(n_e)], axis=0)
    acc = lax.dot_general(onehot, y_ref[0].reshape(n_e * n_slots, y_ref.shape[3]),
                          (((0,), (0,)), ((), ())), preferred_element_type=F32)
    ms = jnp.mean(acc * acc, axis=-1, keepdims=True)
    y = (acc * lax.rsqrt(ms + EPS)) * g_ref[...]
    out_ref[0] = x_ref[0] + gate_ref[0, 0, 0] * y


def _combine(pos, y, xs, mod, gain, row0):
    b, n_tok, d = xs.shape
    tm = TOK_TILE
    n_e, n_slots = y.shape[1:3]
    return pl.pallas_call(
        _combine_body,
        grid=(b, n_tok // tm),
        in_specs=[pl.BlockSpec((1, n_e, tm), lambda b_, i: (b_, 0, i)),
                  pl.BlockSpec((1, n_e, n_slots, d), lambda b_, i: (b_, 0, 0, 0)),
                  pl.BlockSpec((1, tm, d), lambda b_, i: (b_, i, 0)),
                  _mod_spec(5, row0, tm),
                  pl.BlockSpec(gain.shape, lambda b_, i: (0, 0))],
        out_specs=pl.BlockSpec((1, tm, d), lambda b_, i: (b_, i, 0)),
        out_shape=jax.ShapeDtypeStruct(xs.shape, F32),
        compiler_params=_cparams(2),
        name="combine",
    )(pos, y, xs, mod, gain)


def kernel(x, c, ctx, c_ctx, w_ada, b_ada, g_pre_mix, g_post_mix, g_pre_ffn, g_post_ffn, w_in, w_out, q_gain_a, k_gain_a, lam_q1, lam_k1, lam_q2, lam_k2, subln_gain_b, sink_c, rpb_d, w_router, w_gate, w_up, w_down):
    b, n_lat, d = x.shape
    n_ctx = ctx.shape[1]
    depth = w_ada.shape[0]
    assert n_ctx == TOK_TILE and n_lat % TOK_TILE == 0 and d == D_MODEL and b + 1 <= 16

    stream = (ctx, x)
    cc = jnp.zeros((16, d), F32).at[:b].set(c).at[b].set(c_ctx)
    mod_all = _ada(cc, w_ada, b_ada)
    mod_all = jnp.stack([jnp.broadcast_to(mod_all[:, b:b + 1], (depth, b, 6 * d)), mod_all[:, :b]],
                        axis=2).reshape(depth, b, 2, 6, 1, d)
    ca, sa = _rope_lane_tables(n_ctx, n_lat, HEAD_DIM)
    cb, sb = _rope_lane_tables(n_ctx, n_lat, DIFF_DIM)
    w_perm_all = _permute_w_in(w_in)
    w_out_all = w_out.astype(BF16)
    w_router_t = jnp.swapaxes(w_router, 1, 2)
    nbr_bias = _neighbourhood_bias(rpb_d, n_ctx, n_lat // GRID_W)
    pair_gain = lambda g: jnp.stack([jnp.tile(g[:, 0::2], (1, 4)), jnp.tile(g[:, 1::2], (1, 4))], axis=1)
    gq_all, gk_all = pair_gain(q_gain_a), pair_gain(k_gain_a)
    row = lambda v: v.reshape(1, -1)

    for l in range(depth):
        with_ctx = l < depth - 1
        row0 = 0 if with_ctx else 1
        lam_init = 0.8 - 0.6 * math.exp(-0.3 * l)
        mod = mod_all[l]
        qkv = _proj(stream, mod, row(g_pre_mix[l]), w_perm_all[l], ca, sa, cb, sb, gq_all[l], gk_all[l])

        lam = (jnp.exp(jnp.sum(lam_q1[l] * lam_k1[l])) - jnp.exp(jnp.sum(lam_q2[l] * lam_k2[l]))
               + lam_init).reshape(1)
        o_mix = (
            _attn_global(qkv, 0, row0),
            _attn_global(qkv, 1, row0, lam=lam, gain=row(jnp.tile(subln_gain_b[l], N_HEADS)),
                         lam_init=lam_init),
            _attn_window(qkv, 2, row0, sink_c[l]),
            _attn_nbr(qkv, 3, row0, nbr_bias, l),
        )
        xs = _outproj(o_mix, w_out_all[l], stream, mod, row(g_post_mix[l]), row0)

        h, logits = _route(xs, mod, row(g_pre_ffn[l]), w_router_t[l], row0)
        cap_lat = CAPACITY_FACTOR * n_lat // N_EXPERTS
        cap_ctx = CAPACITY_FACTOR * n_ctx // N_EXPERTS
        if with_ctx:
            segments = ((0, n_ctx, cap_ctx, 0), (n_ctx, n_lat, cap_lat, cap_ctx))
            n_slots = cap_ctx + cap_lat
        else:
            segments = ((0, n_lat, cap_lat, 0),)
            n_slots = cap_lat
        pos, aff = _select(logits, segments)
        y = _experts(pos, aff, h, w_gate, w_up, w_down, l, n_slots)
        stream = _combine(pos, y, xs, mod, row(g_post_ffn[l]), row0)
    return stream
```

```python
import functools
import math

import numpy as np
import jax
import jax.numpy as jnp
from jax import lax
from jax.experimental import pallas as pl
from jax.experimental.pallas import tpu as pltpu

F32 = jnp.float32
BF16 = jnp.bfloat16

D_MODEL = 1024
GRID_W = 64
HEAD_DIM = 64
DIFF_DIM = 32
N_HEADS = 4
MIXER_W = N_HEADS * HEAD_DIM
QKV_W = 4 * 3 * MIXER_W
WINDOW = 128
NA_ROWS = 8
NA_COLS = 16
ROPE_BASE = 10000.0
N_EXPERTS = 16
CAPACITY_FACTOR = 2
EPS = 1e-6
NEG_INF = -1e30
LOG2E = 1.4426950408889634
PROJ_WIDTHS = (256, 128, 128, 256, 256, 256, 256, 128, 128, 256, 256, 256)
LANES = 128
TOK_TILE = 256
VMEM_LIMIT = 56 * 1024 * 1024
ATTN_STEP_ROWS = 512
NBR_TILE_ROWS = TOK_TILE // GRID_W
NBR_STRIP_ROWS = 12
BISECT_MAX_STEPS = 192


def _cparams(n_axes):
    return pltpu.CompilerParams(dimension_semantics=("arbitrary",) * n_axes,
                                vmem_limit_bytes=VMEM_LIMIT)


def _proj_source_columns():
    offs = np.concatenate([[0], np.cumsum(PROJ_WIDTHS)])

    def pairs(base, n_slots, slot_src, n_pairs):
        x0 = [base + slot_src(j) + 2 * i for j in range(n_slots) for i in range(n_pairs)]
        return x0 + [c + 1 for c in x0]

    cols = []
    for m in range(4):
        q0, k0, v0 = offs[3 * m], offs[3 * m + 1], offs[3 * m + 2]
        if m in (0, 2):
            cols += pairs(q0, 4, lambda j: j * HEAD_DIM, 32)
            cols += pairs(k0, 4, lambda j: (j // 2) * HEAD_DIM, 32)
            cols += [v0 + (j // 2) * HEAD_DIM + d for j in range(4) for d in range(HEAD_DIM)]
        elif m == 1:
            cols += pairs(q0, 8, lambda j: j * DIFF_DIM, 16)
            cols += pairs(k0, 8, lambda j: j * DIFF_DIM, 16)
            cols += [v0 + c for c in range(MIXER_W)]
        else:
            cols += pairs(q0, 4, lambda j: j * HEAD_DIM, 32)
            cols += pairs(k0, 4, lambda j: j * HEAD_DIM, 32)
            cols += [v0 + c for c in range(MIXER_W)]
    return np.asarray(cols, dtype=np.int32)


def _permute_w_in(w_in):
    src = _proj_source_columns()
    pick = (jnp.arange(w_in.shape[-1], dtype=jnp.int32)[:, None] == src[None, :]).astype(BF16)
    return jnp.einsum('ldk,kn->ldn', w_in.astype(BF16), pick, preferred_element_type=BF16)


def _rope_tables(n_tokens, dim):
    t = jnp.arange(n_tokens, dtype=jnp.int32)
    rows = (t // GRID_W).astype(F32)
    cols = (t % GRID_W).astype(F32)
    n_axis = dim // 4
    inv_freq = ROPE_BASE ** (-jnp.arange(n_axis, dtype=F32) / n_axis)
    ang = jnp.concatenate([rows[:, None] * inv_freq, cols[:, None] * inv_freq], axis=-1)
    return jnp.cos(ang), jnp.sin(ang)


def _rope_lane_tables(n_ctx, n_lat, dim):
    cos, sin = _rope_tables(n_lat, dim)
    reps = LANES // cos.shape[1]
    cos = jnp.concatenate([jnp.ones((n_ctx, LANES), F32), jnp.tile(cos, (1, reps))], axis=0)
    sin = jnp.concatenate([jnp.zeros((n_ctx, LANES), F32), jnp.tile(sin, (1, reps))], axis=0)
    return cos, sin


def _nbr_strip_start(tile, n_rows):
    lo = tile * NBR_TILE_ROWS - NA_ROWS // 2
    return jnp.clip(lo, 0, n_rows - NBR_STRIP_ROWS) if isinstance(lo, jax.Array) else int(
        np.clip(lo, 0, n_rows - NBR_STRIP_ROWS))


def _nbr_pattern(tile, n_tiles):
    if isinstance(tile, jax.Array):
        return jnp.where(tile == 0, 0, jnp.where(tile == n_tiles - 1, 2, 1))
    return 0 if tile == 0 else (2 if tile == n_tiles - 1 else 1)


def _neighbourhood_bias(rpb, n_ctx, n_rows):
    n_tiles = n_rows // NBR_TILE_ROWS
    assert n_tiles >= 3 and n_rows >= NBR_STRIP_ROWS

    def rows_of(tile):
        ss = _nbr_strip_start(tile, n_rows)
        r = tile * NBR_TILE_ROWS + np.arange(NBR_TILE_ROWS)[:, None]
        rs = np.clip(r - NA_ROWS // 2, 0, n_rows - NA_ROWS)
        kr = ss + np.arange(NBR_STRIP_ROWS)[None, :]
        valid = (kr >= rs) & (kr < rs + NA_ROWS)
        return valid, np.where(valid, kr - r + NA_ROWS - 1, 0)

    reps = [rows_of(t) for t in (0, 1, n_tiles - 1)]
    for t in range(n_tiles):
        v, ri = rows_of(t)
        assert np.array_equal(v, reps[_nbr_pattern(t, n_tiles)][0])
        assert np.array_equal(ri, reps[_nbr_pattern(t, n_tiles)][1])
    row_valid = np.stack([v for v, _ in reps])
    ri = np.stack([r for _, r in reps])
    oh_r = (ri[..., None] == np.arange(2 * NA_ROWS - 1)) & row_valid[..., None]
    cq = np.arange(GRID_W)
    col_start = np.clip(cq - NA_COLS // 2, 0, GRID_W - NA_COLS)
    col_valid = (cq[None, :] >= col_start[:, None]) & (cq[None, :] < col_start[:, None] + NA_COLS)
    ci = np.clip(cq[None, :] - cq[:, None] + NA_COLS - 1, 0, 2 * NA_COLS - 2)
    oh_c = np.arange(2 * NA_COLS - 1)[:, None, None] == ci[None]
    bias = jnp.einsum('lhrc,pajr,cqw->lhpaqjw', rpb.astype(F32), oh_r.astype(np.float32),
                      oh_c.astype(np.float32), precision=lax.Precision.HIGHEST)
    valid = row_valid[None, None, :, :, None, :, None] & col_valid[None, None, None, None, :, None, :]
    bias = jnp.where(valid, bias * LOG2E, NEG_INF)
    bias = bias.reshape(bias.shape[:2] + (3, TOK_TILE, NBR_STRIP_ROWS * GRID_W))
    return jnp.concatenate([jnp.zeros(bias.shape[:4] + (n_ctx,), F32), bias], axis=-1)


def _block_diag_ones(n, group):
    r = lax.broadcasted_iota(jnp.int32, (n, n), 0) // group
    c = lax.broadcasted_iota(jnp.int32, (n, n), 1) // group
    return jnp.where(r == c, 1.0, 0.0).astype(BF16)


def _dot(a, b):
    return jnp.dot(a, b, preferred_element_type=F32)


def _dot_nt(a, b):
    return lax.dot_general(a, b, (((1,), (1,)), ((), ())), preferred_element_type=F32)


def _split(a):
    hi = a.astype(BF16)
    return hi, (a - hi.astype(F32)).astype(BF16)


def _dot_hilo(a, b_bf16):
    hi, lo = _split(a)
    return _dot(hi, b_bf16) + _dot(lo, b_bf16)


def _lane_mask(cond):
    return jnp.where(cond, 1.0, 0.0).astype(BF16)


def _norm_modulate(x, gain, shift, scale):
    ms = jnp.mean(x * x, axis=-1, keepdims=True)
    h = (x * lax.rsqrt(ms + EPS)) * gain
    return h * (1.0 + scale) + shift


def _mod_spec(chunk, row0, tile):
    n_ctx_tiles = TOK_TILE // tile
    return pl.BlockSpec((1, 1, 1, 1, D_MODEL),
                        lambda b, i: (b, jnp.minimum((i + row0) // n_ctx_tiles, 1), chunk, 0, 0))


def _ada_body(c_ref, w_ref, b_ref, o_ref):
    c = c_ref[...]
    cs = (c * jax.nn.sigmoid(c)).astype(BF16)
    o_ref[0] = _dot(cs, w_ref[0].astype(BF16)) + b_ref[0]


def _ada(cc, w_ada, b_ada):
    depth, d, n = w_ada.shape
    tn = 512
    return pl.pallas_call(
        _ada_body,
        grid=(depth, n // tn),
        in_specs=[pl.BlockSpec(cc.shape, lambda l, j: (0, 0)),
                  pl.BlockSpec((1, d, tn), lambda l, j: (l, 0, j)),
                  pl.BlockSpec((1, 1, tn), lambda l, j: (l, 0, j))],
        out_specs=pl.BlockSpec((1, cc.shape[0], tn), lambda l, j: (l, 0, j)),
        out_shape=jax.ShapeDtypeStruct((depth, cc.shape[0], n), F32),
        compiler_params=_cparams(2),
        name="ada",
    )(cc, w_ada, b_ada.reshape(depth, 1, n))


def _stream_specs(stream, row0):
    ctx_arr, lat_arr = stream if isinstance(stream, tuple) else (stream, stream)
    lat_tile0 = 0 if isinstance(stream, tuple) else 1
    blk = (1, TOK_TILE, lat_arr.shape[-1])
    lat_spec = pl.BlockSpec(blk, lambda b_, i: (b_, jnp.maximum(i + row0 - 1, 0) + lat_tile0, 0))
    if row0 > 0:
        return [lat_spec], [lat_arr]
    return [pl.BlockSpec(blk, lambda b_, i: (b_, 0, 0)), lat_spec], [ctx_arr, lat_arr]


def _stream_tile(x_refs):
    if len(x_refs) == 1:
        return x_refs[0][0]
    return jnp.where(pl.program_id(1) == 0, x_refs[0][0], x_refs[1][0])


def _proj_body(xc_ref, xl_ref, sh_ref, sc_ref, g_ref, w_ref, ca_ref, sa_ref, cb_ref, sb_ref, gq_ref,
               gk_ref, o_ref):
    h = _norm_modulate(_stream_tile((xc_ref, xl_ref)), g_ref[...], sh_ref[0, 0, 0], sc_ref[0, 0, 0])
    acc = _dot(h.astype(BF16), w_ref[...])
    ca, sa, cb, sb = ca_ref[...], sa_ref[...], cb_ref[...], sb_ref[...]
    bd = _block_diag_ones(LANES, 32)

    def get(m, j):
        c0 = (3 * m) * MIXER_W + j * LANES
        return acc[:, c0:c0 + LANES]

    def put(m, j, v):
        c0 = (3 * m) * MIXER_W + j * LANES
        o_ref[0, :, c0:c0 + LANES] = v.astype(BF16)

    def rope(x0, x1, c, s):
        return x0 * c - x1 * s, x0 * s + x1 * c

    def head_norm(x0, x1, g_ref_):
        gs = _dot_hilo(x0 * x0 + x1 * x1, bd)
        r = lax.rsqrt(gs * (1.0 / HEAD_DIM) + EPS)
        return x0 * r * g_ref_[0:1, :], x1 * r * g_ref_[1:2, :]

    for m in range(4):
        q0, q1, k0, k1 = get(m, 0), get(m, 1), get(m, 2), get(m, 3)
        if m == 0:
            q0, q1 = head_norm(q0, q1, gq_ref)
            k0, k1 = head_norm(k0, k1, gk_ref)
        if m in (0, 2):
            q0, q1 = rope(q0, q1, ca, sa)
            k0, k1 = rope(k0, k1, ca, sa)
        elif m == 1:
            q0, q1 = rope(q0, q1, cb, sb)
            k0, k1 = rope(k0, k1, cb, sb)
        qscale = LOG2E * (DIFF_DIM if m == 1 else HEAD_DIM) ** -0.5
        put(m, 0, q0 * qscale)
        put(m, 1, q1 * qscale)
        put(m, 2, k0)
        put(m, 3, k1)
        put(m, 4, get(m, 4))
        put(m, 5, get(m, 5))


def _proj(stream, mod, gain, w_perm, ca, sa, cb, sb, gq, gk):
    x_specs, x_args = _stream_specs(stream, 0)
    b = x_args[0].shape[0]
    lt = ca.shape[0]
    tm = TOK_TILE
    tab = pl.BlockSpec((tm, LANES), lambda b_, i: (i, 0))
    full = lambda a: pl.BlockSpec(a.shape, lambda b_, i: (0,) * a.ndim)
    return pl.pallas_call(
        _proj_body,
        grid=(b, lt // tm),
        in_specs=x_specs + [_mod_spec(0, 0, tm), _mod_spec(1, 0, tm),
                            full(gain), full(w_perm), tab, tab, tab, tab, full(gq), full(gk)],
        out_specs=pl.BlockSpec((1, tm, QKV_W), lambda b_, i: (b_, i, 0)),
        out_shape=jax.ShapeDtypeStruct((b, lt, QKV_W), BF16),
        compiler_params=_cparams(2),
        name="proj",
    )(*x_args, mod, mod, gain, w_perm, ca, sa, cb, sb, gq, gk)


def _sum_lane(h):
    return (HEAD_DIM * (h + 1)) % MIXER_W


def _fill_head_values(v_ref, vm_ref):
    lane = lax.broadcasted_iota(jnp.int32, (1, MIXER_W), 1)
    v = v_ref[0]
    for h in range(N_HEADS):
        tap = jnp.where(lane == _sum_lane(h), 1.0, 0.0).astype(BF16)
        vm_ref[h] = jnp.where(lane // HEAD_DIM == h, v, tap)


def _attend(qm, kk, vm, bias=None, valid=None, extra_logit=None):
    s = _dot_nt(qm, kk)
    if bias is not None:
        s = s + bias
    if valid is not None:
        s = jnp.where(valid, s, NEG_INF)
    m = jnp.max(s, axis=-1, keepdims=True)
    if extra_logit is not None:
        m = jnp.maximum(m, extra_logit)
    return _dot(jnp.exp2(s - m).astype(BF16), vm), m


def _attn_body(*refs, kind, with_ctx, lam_init, n_ctx, n_lat):
    refs = list(refs)
    scalar_ref = refs.pop(0) if kind in ("diff", "window") else None
    q_ref, k_ref, v_ref = refs[:3]
    refs = refs[3:]
    gain_ref = refs.pop(0) if kind == "diff" else None
    bias_refs = (refs.pop(0), refs.pop(0)) if kind == "nbr" else None
    o_ref, vm_ref = refs
    step = pl.program_id(1)
    lane = lax.broadcasted_iota(jnp.int32, (1, MIXER_W), 1)
    sub = ATTN_STEP_ROWS if kind in ("global", "diff") else TOK_TILE
    band = sub + 2 * WINDOW
    strip = NBR_STRIP_ROWS * GRID_W
    out0 = n_ctx if with_ctx else 0

    def rows_of(ref, ranges, *lead):
        parts = [ref[lead + (pl.ds(start, size), slice(None))] for start, size in ranges]
        return parts[0] if len(parts) == 1 else jnp.concatenate(parts, axis=0)

    def tile(q_row, n_rows, out_row, t=None, bias_ref=None):
        q = q_ref[0, pl.ds(q_row, n_rows), :]
        ranges, valid = [(0, n_ctx)], None
        if t is not None:
            if kind in ("global", "diff"):
                ranges = [(0, n_ctx + n_lat)]
            elif kind == "window":
                k_start = jnp.clip(t * sub - WINDOW, 0, n_lat - band)
                ranges.append((pl.multiple_of(n_ctx + k_start, LANES), band))
                col = lax.broadcasted_iota(jnp.int32, (1, n_ctx + band), 1)
                row = lax.broadcasted_iota(jnp.int32, (n_rows, 1), 0)
                dist = (k_start + col - n_ctx) - (t * sub + row)
                valid = jnp.abs(jnp.where(col < n_ctx, 0, dist)) <= WINDOW
            else:
                rs = _nbr_strip_start(t, n_lat // GRID_W)
                ranges.append((pl.multiple_of(n_ctx + rs * GRID_W, GRID_W), strip))
        kk = rows_of(k_ref, ranges, 0)
        out = jnp.zeros((n_rows, MIXER_W), F32)
        for h in range(N_HEADS):
            vm = rows_of(vm_ref, ranges, h)
            tap = _sum_lane(h)
            if kind == "diff":
                o = None
                for c in range(2):
                    qm = q * _lane_mask((lane % LANES) // (DIFF_DIM // 2) == 2 * h + c)
                    pv, _ = _attend(qm, kk, vm)
                    coef = 1.0 if c == 0 else scalar_ref[0]
                    term = pv * (coef / pv[:, tap:tap + 1])
                    o = term if c == 0 else o - term
            else:
                qm = q * _lane_mask((lane % LANES) // (HEAD_DIM // 2) == h)
                if kind == "window":
                    sink = scalar_ref[h] * LOG2E
                    pv, m = _attend(qm, kk, vm, valid=valid, extra_logit=sink)
                    o = pv * (1.0 / (pv[:, tap:tap + 1] + jnp.exp2(sink - m)))
                else:
                    bias = bias_ref[0, h, 0] if (kind == "nbr" and t is not None) else None
                    pv, _ = _attend(qm, kk, vm, bias=bias)
                    o = pv * (1.0 / pv[:, tap:tap + 1])
            out = jnp.where(lane // HEAD_DIM == h, o, out)
        if kind == "diff":
            gs = _dot_hilo(out * out, _block_diag_ones(MIXER_W, HEAD_DIM))
            out = out * lax.rsqrt(gs * (1.0 / HEAD_DIM) + EPS) * gain_ref[...] * (1.0 - lam_init)
        o_ref[0, pl.ds(out_row, n_rows), :] = out.astype(BF16)

    @pl.when(step == 0)
    def _():
        _fill_head_values(v_ref, vm_ref)
        if with_ctx:
            tile(0, n_ctx, 0)

    for j in range(ATTN_STEP_ROWS // sub):
        t = step * (ATTN_STEP_ROWS // sub) + j
        row = pl.multiple_of(t * sub, sub)
        tile(n_ctx + row, sub, out0 + row, t=t, bias_ref=bias_refs[j] if bias_refs else None)


def _attention(qkv, mixer, kind, with_ctx, *, scalars=None, gain=None, bias=None, layer=0, lam_init=0.0):
    b, lt, _ = qkv.shape
    n_lat = lt - TOK_TILE
    n_out = lt if with_ctx else n_lat
    col = lambda j: pl.BlockSpec((1, lt, MIXER_W), lambda b_, i: (b_, 0, 3 * mixer + j))
    in_specs, args = [col(0), col(1), col(2)], [qkv, qkv, qkv]
    if scalars is not None:
        in_specs, args = [pl.BlockSpec(memory_space=pltpu.SMEM)] + in_specs, [scalars] + args
    if gain is not None:
        in_specs.append(pl.BlockSpec(gain.shape, lambda b_, i: (0, 0)))
        args.append(gain)
    if bias is not None:
        n_sub = ATTN_STEP_ROWS // TOK_TILE
        n_tiles = n_lat // TOK_TILE
        for j in range(n_sub):
            in_specs.append(pl.BlockSpec(
                (1, N_HEADS, 1) + bias.shape[3:],
                lambda b_, i, j=j: (layer, 0, _nbr_pattern(i * n_sub + j, n_tiles), 0, 0)))
            args.append(bias)
    return pl.pallas_call(
        functools.partial(_attn_body, kind=kind, with_ctx=with_ctx, lam_init=lam_init,
                          n_ctx=TOK_TILE, n_lat=n_lat),
        grid=(b, n_lat // ATTN_STEP_ROWS),
        in_specs=in_specs,
        out_specs=pl.BlockSpec((1, n_out, MIXER_W), lambda b_, i: (b_, 0, 0)),
        out_shape=jax.ShapeDtypeStruct((b, n_out, MIXER_W), BF16),
        scratch_shapes=[pltpu.VMEM((N_HEADS, lt, MIXER_W), BF16)],
        compiler_params=_cparams(2),
        name="attn_" + kind,
    )(*args)


def _outproj_body(oa_ref, ob_ref, oc_ref, od_ref, w_ref, gate_ref, g_ref, *rest):
    x_refs, out_ref = rest[:-1], rest[-1]
    acc = None
    for m, o_ref in enumerate((oa_ref, ob_ref, oc_ref, od_ref)):
        t = _dot(o_ref[0], w_ref[m * MIXER_W:(m + 1) * MIXER_W, :])
        acc = t if acc is None else acc + t
    ms = jnp.mean(acc * acc, axis=-1, keepdims=True)
    y = (acc * lax.rsqrt(ms + EPS)) * g_ref[...]
    out_ref[0] = _stream_tile(x_refs) + gate_ref[0, 0, 0] * y


def _outproj(o_mix, w_out, stream, mod, gain, row0):
    x_specs, x_args = _stream_specs(stream, row0)
    b, n_tok, _ = o_mix[0].shape
    d = x_args[0].shape[-1]
    tm = TOK_TILE
    o_spec = pl.BlockSpec((1, tm, MIXER_W), lambda b_, i: (b_, i, 0))
    return pl.pallas_call(
        _outproj_body,
        grid=(b, n_tok // tm),
        in_specs=[o_spec, o_spec, o_spec, o_spec,
                  pl.BlockSpec(w_out.shape, lambda b_, i: (0, 0)),
                  _mod_spec(2, row0, tm),
                  pl.BlockSpec(gain.shape, lambda b_, i: (0, 0))] + x_specs,
        out_specs=pl.BlockSpec((1, tm, d), lambda b_, i: (b_, i, 0)),
        out_shape=jax.ShapeDtypeStruct((b, n_tok, d), F32),
        compiler_params=_cparams(2),
        name="outproj",
    )(*o_mix, w_out, mod, gain, *x_args)


def _route_body(x_ref, sh_ref, sc_ref, g_ref, wr_ref, h_ref, lg_ref):
    h = _norm_modulate(x_ref[0], g_ref[...], sh_ref[0, 0, 0], sc_ref[0, 0, 0])
    h_ref[0] = h.astype(BF16)
    w_hi, w_lo = _split(wr_ref[...])
    h_hi, h_lo = _split(h)
    lg_ref[0] = _dot_nt(w_hi, h_hi) + (_dot_nt(w_hi, h_lo) + _dot_nt(w_lo, h_hi))


def _route(xs, mod, gain, w_router_t, row0):
    b, n_tok, d = xs.shape
    tm = TOK_TILE
    n_e = w_router_t.shape[0]
    return pl.pallas_call(
        _route_body,
        grid=(b, n_tok // tm),
        in_specs=[pl.BlockSpec((1, tm, d), lambda b_, i: (b_, i, 0)),
                  _mod_spec(3, row0, tm), _mod_spec(4, row0, tm),
                  pl.BlockSpec(gain.shape, lambda b_, i: (0, 0)),
                  pl.BlockSpec(w_router_t.shape, lambda b_, i: (0, 0))],
        out_specs=[pl.BlockSpec((1, tm, d), lambda b_, i: (b_, i, 0)),
                   pl.BlockSpec((1, n_e, tm), lambda b_, i: (b_, 0, i))],
        out_shape=[jax.ShapeDtypeStruct((b, n_tok, d), BF16),
                   jax.ShapeDtypeStruct((b, n_e, n_tok), F32)],
        compiler_params=_cparams(2),
        name="route",
    )(xs, mod, mod, gain, w_router_t)


def _exclusive_cumsum(x, tri):
    off = jnp.zeros((x.shape[0], 1), F32)
    outs = []
    for c in range(x.shape[1] // LANES):
        xc = x[:, c * LANES:(c + 1) * LANES]
        inc = _dot(xc.astype(BF16), tri)
        outs.append(inc - xc + off)
        off = off + inc[:, LANES - 1:LANES]
    return jnp.concatenate(outs, axis=1)


def _select_body(lg_ref, pos_ref, aff_ref, *, segments):
    r = lax.broadcasted_iota(jnp.int32, (LANES, LANES), 0)
    c = lax.broadcasted_iota(jnp.int32, (LANES, LANES), 1)
    tri = jnp.where(r <= c, 1.0, 0.0).astype(BF16)
    for t0, t, cap, slot0 in segments:
        lg = lg_ref[0, :, t0:t0 + t]
        e = jnp.exp(lg - jnp.max(lg, axis=0, keepdims=True))
        aff = e / jnp.sum(e, axis=0, keepdims=True)

        def count_above(thr):
            return jnp.sum(jnp.where(aff > thr, 1.0, 0.0), axis=1, keepdims=True)

        def unsettled(carry):
            return jnp.logical_and(carry[2] > 0.0, carry[3] < BISECT_MAX_STEPS)

        def bisect(carry):
            lo, hi, _, step = carry
            mid = 0.5 * (lo + hi)
            cnt = count_above(mid)
            new_lo = jnp.where(cnt >= cap, mid, lo)
            new_hi = jnp.where(cnt <= cap, mid, hi)
            moving = jnp.logical_and(new_lo < new_hi, jnp.logical_and(mid > lo, mid < hi))
            return new_lo, new_hi, jnp.sum(jnp.where(moving, 1.0, 0.0)), step + 1

        lo0 = jnp.full((aff.shape[0], 1), -1.0, F32)
        hi0 = jnp.max(aff, axis=1, keepdims=True)
        lo, hi, _, _ = lax.while_loop(unsettled, bisect, (lo0, hi0, jnp.float32(1.0), jnp.int32(0)))
        gt = aff > hi
        eq = jnp.logical_and(aff > lo, aff <= hi)
        need = cap - count_above(hi)
        eq_rank = _exclusive_cumsum(jnp.where(eq, 1.0, 0.0), tri)
        sel = jnp.logical_or(gt, jnp.logical_and(eq, eq_rank < need))
        slot = _exclusive_cumsum(jnp.where(sel, 1.0, 0.0), tri) + slot0
        pos_ref[0, :, t0:t0 + t] = jnp.where(sel, slot, -1.0).astype(jnp.int32)
        aff_ref[0, :, t0:t0 + t] = aff


def _select(logits, segments):
    b, n_e, lt = logits.shape
    spec = pl.BlockSpec((1, n_e, lt), lambda b_: (b_, 0, 0))
    return pl.pallas_call(
        functools.partial(_select_body, segments=segments),
        grid=(b,),
        in_specs=[spec],
        out_specs=[spec, spec],
        out_shape=[jax.ShapeDtypeStruct((b, n_e, lt), jnp.int32),
                   jax.ShapeDtypeStruct((b, n_e, lt), F32)],
        compiler_params=_cparams(1),
        name="select",
    )(logits)


def _experts_body(pos_ref, aff_ref, h_ref, wg_ref, wu_ref, wd_ref, y_ref, wg_s, wu_s, wd_s):
    @pl.when(pl.program_id(1) == 0)
    def _():
        wg_s[...] = wg_ref[0, 0].astype(BF16)
        wu_s[...] = wu_ref[0, 0].astype(BF16)
        wd_s[...] = wd_ref[0, 0].astype(BF16)

    n_slots = y_ref.shape[2]
    e = pl.program_id(0)
    pos = pos_ref[0, pl.ds(e, 1), :]
    hit = lax.broadcasted_iota(jnp.int32, (n_slots, pos.shape[1]), 0) == pos
    xe = _dot(jnp.where(hit, 1.0, 0.0).astype(BF16), h_ref[0]).astype(BF16)
    gate = jnp.sum(jnp.where(hit, aff_ref[0, pl.ds(e, 1), :], 0.0), axis=1, keepdims=True)
    a = _dot(xe, wg_s[...])
    u = _dot(xe, wu_s[...])
    act = ((a * jax.nn.sigmoid(a)) * u).astype(BF16)
    y_ref[0, 0] = (_dot(act, wd_s[...]) * gate).astype(BF16)


def _experts(pos, aff, h, w_gate, w_up, w_down, layer, n_slots):
    b, n_e, n_tok = pos.shape
    d, f = w_gate.shape[2:]
    row_spec = pl.BlockSpec((1, n_e, n_tok), lambda e, b_: (b_, 0, 0))
    return pl.pallas_call(
        _experts_body,
        grid=(n_e, b),
        in_specs=[row_spec, row_spec,
                  pl.BlockSpec((1, n_tok, d), lambda e, b_: (b_, 0, 0)),
                  pl.BlockSpec((1, 1, d, f), lambda e, b_: (layer, e, 0, 0)),
                  pl.BlockSpec((1, 1, d, f), lambda e, b_: (layer, e, 0, 0)),
                  pl.BlockSpec((1, 1, f, d), lambda e, b_: (layer, e, 0, 0))],
        out_specs=pl.BlockSpec((1, 1, n_slots, d), lambda e, b_: (b_, e, 0, 0)),
        out_shape=jax.ShapeDtypeStruct((b, n_e, n_slots, d), BF16),
        scratch_shapes=[pltpu.VMEM((d, f), BF16), pltpu.VMEM((d, f), BF16), pltpu.VMEM((f, d), BF16)],
        compiler_params=_cparams(2),
        name="experts",
    )(pos, aff, h, w_gate, w_up, w_down)


def _combine_body(pos_ref, y_ref, x_ref, gate_ref, g_ref, out_ref):
    n_e, n_slots = y_ref.shape[1:3]
    pos = pos_ref[0]
    slot = lax.broadcasted_iota(jnp.int32, (n_slots, pos.shape[1]), 0)
    onehot = jnp.concatenate(
        [jnp.where(slot == pos[e:e + 1, :], 1.0, 0.0).astype(BF16) for e in range(n_e)], axis=0)
    acc = lax.dot_general(onehot, y_ref[0].reshape(n_e * n_slots, y_ref.shape[3]),
                          (((0,), (0,)), ((), ())), preferred_element_type=F32)
    ms = jnp.mean(acc * acc, axis=-1, keepdims=True)
    y = (acc * lax.rsqrt(ms + EPS)) * g_ref[...]
    out_ref[0] = x_ref[0] + gate_ref[0, 0, 0] * y


def _combine(pos, y, xs, mod, gain, row0):
    b, n_tok, d = xs.shape
    tm = TOK_TILE
    n_e, n_slots = y.shape[1:3]
    return pl.pallas_call(
        _combine_body,
        grid=(b, n_tok // tm),
        in_specs=[pl.BlockSpec((1, n_e, tm), lambda b_, i: (b_, 0, i)),
                  pl.BlockSpec((1, n_e, n_slots, d), lambda b_, i: (b_, 0, 0, 0)),
                  pl.BlockSpec((1, tm, d), lambda b_, i: (b_, i, 0)),
                  _mod_spec(5, row0, tm),
                  pl.BlockSpec(gain.shape, lambda b_, i: (0, 0))],
        out_specs=pl.BlockSpec((1, tm, d), lambda b_, i: (b_, i, 0)),
        out_shape=jax.ShapeDtypeStruct(xs.shape, F32),
        compiler_params=_cparams(2),
        name="combine",
    )(pos, y, xs, mod, gain)


def kernel(x, c, ctx, c_ctx, w_ada, b_ada, g_pre_mix, g_post_mix, g_pre_ffn, g_post_ffn, w_in, w_out, q_gain_a, k_gain_a, lam_q1, lam_k1, lam_q2, lam_k2, subln_gain_b, sink_c, rpb_d, w_router, w_gate, w_up, w_down):
    b, n_lat, d = x.shape
    n_ctx = ctx.shape[1]
    depth = w_ada.shape[0]
    assert n_ctx == TOK_TILE and n_lat % ATTN_STEP_ROWS == 0 and d == D_MODEL and b + 1 <= 16

    stream = (ctx, x)
    cc = jnp.zeros((16, d), F32).at[:b].set(c).at[b].set(c_ctx)
    mod_all = _ada(cc, w_ada, b_ada)
    mod_all = jnp.stack([jnp.broadcast_to(mod_all[:, b:b + 1], (depth, b, 6 * d)), mod_all[:, :b]],
                        axis=2).reshape(depth, b, 2, 6, 1, d)
    ca, sa = _rope_lane_tables(n_ctx, n_lat, HEAD_DIM)
    cb, sb = _rope_lane_tables(n_ctx, n_lat, DIFF_DIM)
    w_perm_all = _permute_w_in(w_in)
    w_out_all = w_out.astype(BF16)
    w_router_t = jnp.swapaxes(w_router, 1, 2)
    nbr_bias = _neighbourhood_bias(rpb_d, n_ctx, n_lat // GRID_W)
    pair_gain = lambda g: jnp.stack([jnp.tile(g[:, 0::2], (1, 4)), jnp.tile(g[:, 1::2], (1, 4))], axis=1)
    gq_all, gk_all = pair_gain(q_gain_a), pair_gain(k_gain_a)
    row = lambda v: v.reshape(1, -1)

    for l in range(depth):
        with_ctx = l < depth - 1
        row0 = 0 if with_ctx else 1
        lam_init = 0.8 - 0.6 * math.exp(-0.3 * l)
        mod = mod_all[l]
        qkv = _proj(stream, mod, row(g_pre_mix[l]), w_perm_all[l], ca, sa, cb, sb, gq_all[l], gk_all[l])

        lam = (jnp.exp(jnp.sum(lam_q1[l] * lam_k1[l])) - jnp.exp(jnp.sum(lam_q2[l] * lam_k2[l]))
               + lam_init).reshape(1)
        o_mix = (
            _attention(qkv, 0, "global", with_ctx),
            _attention(qkv, 1, "diff", with_ctx, scalars=lam, lam_init=lam_init,
                       gain=row(jnp.tile(subln_gain_b[l], N_HEADS))),
            _attention(qkv, 2, "window", with_ctx, scalars=sink_c[l]),
            _attention(qkv, 3, "nbr", with_ctx, bias=nbr_bias, layer=l),
        )
        xs = _outproj(o_mix, w_out_all[l], stream, mod, row(g_post_mix[l]), row0)

        h, logits = _route(xs, mod, row(g_pre_ffn[l]), w_router_t[l], row0)
        cap_lat = CAPACITY_FACTOR * n_lat // N_EXPERTS
        cap_ctx = CAPACITY_FACTOR * n_ctx // N_EXPERTS
        if with_ctx:
            segments = ((0, n_ctx, cap_ctx, 0), (n_ctx, n_lat, cap_lat, cap_ctx))
            n_slots = cap_ctx + cap_lat
        else:
            segments = ((0, n_lat, cap_lat, 0),)
            n_slots = cap_lat
        pos, aff = _select(logits, segments)
        y = _experts(pos, aff, h, w_gate, w_up, w_down, l, n_slots)
        stream = _combine(pos, y, xs, mod, row(g_post_ffn[l]), row0)
    return stream
```

```python
import functools
import math

import numpy as np
import jax
import jax.numpy as jnp
from jax import lax
from jax.experimental import pallas as pl
from jax.experimental.pallas import tpu as pltpu

F32 = jnp.float32
BF16 = jnp.bfloat16

D_MODEL = 1024
GRID_W = 64
HEAD_DIM = 64
DIFF_DIM = 32
N_HEADS = 4
MIXER_W = N_HEADS * HEAD_DIM
QKV_W = 4 * 3 * MIXER_W
WINDOW = 128
NA_ROWS = 8
NA_COLS = 16
ROPE_BASE = 10000.0
N_EXPERTS = 16
CAPACITY_FACTOR = 2
EPS = 1e-6
NEG_INF = -1e30
LOG2E = 1.4426950408889634
PROJ_WIDTHS = (256, 128, 128, 256, 256, 256, 256, 128, 128, 256, 256, 256)
LANES = 128
TOK_TILE = 256
VMEM_LIMIT = 56 * 1024 * 1024
ATTN_STEP_ROWS = 512
NBR_TILE_ROWS = TOK_TILE // GRID_W
NBR_STRIP_ROWS = 12
GATHER_EXPERTS = 4
FFN_BATCHES = 4
FFN_CHUNK = 256
BISECT_MAX_STEPS = 192


def _cparams(n_axes):
    return pltpu.CompilerParams(dimension_semantics=("arbitrary",) * n_axes,
                                vmem_limit_bytes=VMEM_LIMIT)


def _proj_source_columns():
    offs = np.concatenate([[0], np.cumsum(PROJ_WIDTHS)])

    def pairs(base, n_slots, slot_src, n_pairs):
        x0 = [base + slot_src(j) + 2 * i for j in range(n_slots) for i in range(n_pairs)]
        return x0 + [c + 1 for c in x0]

    cols = []
    for m in range(4):
        q0, k0, v0 = offs[3 * m], offs[3 * m + 1], offs[3 * m + 2]
        if m in (0, 2):
            cols += pairs(q0, 4, lambda j: j * HEAD_DIM, 32)
            cols += pairs(k0, 4, lambda j: (j // 2) * HEAD_DIM, 32)
            cols += [v0 + (j // 2) * HEAD_DIM + d for j in range(4) for d in range(HEAD_DIM)]
        elif m == 1:
            cols += pairs(q0, 8, lambda j: j * DIFF_DIM, 16)
            cols += pairs(k0, 8, lambda j: j * DIFF_DIM, 16)
            cols += [v0 + c for c in range(MIXER_W)]
        else:
            cols += pairs(q0, 4, lambda j: j * HEAD_DIM, 32)
            cols += pairs(k0, 4, lambda j: j * HEAD_DIM, 32)
            cols += [v0 + c for c in range(MIXER_W)]
    return np.asarray(cols, dtype=np.int32)


def _permute_w_in(w_in):
    src = _proj_source_columns()
    pick = (jnp.arange(w_in.shape[-1], dtype=jnp.int32)[:, None] == src[None, :]).astype(BF16)
    return jnp.einsum('ldk,kn->ldn', w_in.astype(BF16), pick, preferred_element_type=BF16)


def _rope_tables(n_tokens, dim):
    t = jnp.arange(n_tokens, dtype=jnp.int32)
    rows = (t // GRID_W).astype(F32)
    cols = (t % GRID_W).astype(F32)
    n_axis = dim // 4
    inv_freq = ROPE_BASE ** (-jnp.arange(n_axis, dtype=F32) / n_axis)
    ang = jnp.concatenate([rows[:, None] * inv_freq, cols[:, None] * inv_freq], axis=-1)
    return jnp.cos(ang), jnp.sin(ang)


def _rope_lane_tables(n_ctx, n_lat, dim):
    cos, sin = _rope_tables(n_lat, dim)
    reps = LANES // cos.shape[1]
    cos = jnp.concatenate([jnp.ones((n_ctx, LANES), F32), jnp.tile(cos, (1, reps))], axis=0)
    sin = jnp.concatenate([jnp.zeros((n_ctx, LANES), F32), jnp.tile(sin, (1, reps))], axis=0)
    return cos, sin


def _nbr_strip_start(tile, n_rows):
    lo = tile * NBR_TILE_ROWS - NA_ROWS // 2
    return jnp.clip(lo, 0, n_rows - NBR_STRIP_ROWS) if isinstance(lo, jax.Array) else int(
        np.clip(lo, 0, n_rows - NBR_STRIP_ROWS))


def _nbr_pattern(tile, n_tiles):
    if isinstance(tile, jax.Array):
        return jnp.where(tile == 0, 0, jnp.where(tile == n_tiles - 1, 2, 1))
    return 0 if tile == 0 else (2 if tile == n_tiles - 1 else 1)


def _neighbourhood_bias(rpb, n_ctx, n_rows):
    n_tiles = n_rows // NBR_TILE_ROWS
    assert n_tiles >= 3 and n_rows >= NBR_STRIP_ROWS

    def rows_of(tile):
        ss = _nbr_strip_start(tile, n_rows)
        r = tile * NBR_TILE_ROWS + np.arange(NBR_TILE_ROWS)[:, None]
        rs = np.clip(r - NA_ROWS // 2, 0, n_rows - NA_ROWS)
        kr = ss + np.arange(NBR_STRIP_ROWS)[None, :]
        valid = (kr >= rs) & (kr < rs + NA_ROWS)
        return valid, np.where(valid, kr - r + NA_ROWS - 1, 0)

    reps = [rows_of(t) for t in (0, 1, n_tiles - 1)]
    for t in range(n_tiles):
        v, ri = rows_of(t)
        assert np.array_equal(v, reps[_nbr_pattern(t, n_tiles)][0])
        assert np.array_equal(ri, reps[_nbr_pattern(t, n_tiles)][1])
    row_valid = np.stack([v for v, _ in reps])
    ri = np.stack([r for _, r in reps])
    oh_r = (ri[..., None] == np.arange(2 * NA_ROWS - 1)) & row_valid[..., None]
    cq = np.arange(GRID_W)
    col_start = np.clip(cq - NA_COLS // 2, 0, GRID_W - NA_COLS)
    col_valid = (cq[None, :] >= col_start[:, None]) & (cq[None, :] < col_start[:, None] + NA_COLS)
    ci = np.clip(cq[None, :] - cq[:, None] + NA_COLS - 1, 0, 2 * NA_COLS - 2)
    oh_c = np.arange(2 * NA_COLS - 1)[:, None, None] == ci[None]
    bias = jnp.einsum('lhrc,pajr,cqw->lhpaqjw', rpb.astype(F32), oh_r.astype(np.float32),
                      oh_c.astype(np.float32), precision=lax.Precision.HIGHEST)
    valid = row_valid[None, None, :, :, None, :, None] & col_valid[None, None, None, None, :, None, :]
    bias = jnp.where(valid, bias * LOG2E, NEG_INF)
    bias = bias.reshape(bias.shape[:2] + (3, TOK_TILE, NBR_STRIP_ROWS * GRID_W))
    return jnp.concatenate([jnp.zeros(bias.shape[:4] + (n_ctx,), F32), bias], axis=-1)


def _block_diag_ones(n, group):
    r = lax.broadcasted_iota(jnp.int32, (n, n), 0) // group
    c = lax.broadcasted_iota(jnp.int32, (n, n), 1) // group
    return jnp.where(r == c, 1.0, 0.0).astype(BF16)


def _dot(a, b):
    return jnp.dot(a, b, preferred_element_type=F32)


def _dot_nt(a, b):
    return lax.dot_general(a, b, (((1,), (1,)), ((), ())), preferred_element_type=F32)


def _split(a):
    hi = a.astype(BF16)
    return hi, (a - hi.astype(F32)).astype(BF16)


def _dot_hilo(a, b_bf16):
    hi, lo = _split(a)
    return _dot(hi, b_bf16) + _dot(lo, b_bf16)


def _lane_mask(cond):
    return jnp.where(cond, 1.0, 0.0).astype(BF16)


def _norm_modulate(x, gain, shift, scale):
    ms = jnp.mean(x * x, axis=-1, keepdims=True)
    h = (x * lax.rsqrt(ms + EPS)) * gain
    return h * (1.0 + scale) + shift


def _mod_spec(chunk, row0, tile):
    n_ctx_tiles = TOK_TILE // tile
    return pl.BlockSpec((1, 1, 1, 1, D_MODEL),
                        lambda b, i: (b, jnp.minimum((i + row0) // n_ctx_tiles, 1), chunk, 0, 0))


def _ada_body(c_ref, w_ref, b_ref, o_ref):
    c = c_ref[...]
    cs = (c * jax.nn.sigmoid(c)).astype(BF16)
    o_ref[0] = _dot(cs, w_ref[0].astype(BF16)) + b_ref[0]


def _ada(cc, w_ada, b_ada):
    depth, d, n = w_ada.shape
    tn = 512
    return pl.pallas_call(
        _ada_body,
        grid=(depth, n // tn),
        in_specs=[pl.BlockSpec(cc.shape, lambda l, j: (0, 0)),
                  pl.BlockSpec((1, d, tn), lambda l, j: (l, 0, j)),
                  pl.BlockSpec((1, 1, tn), lambda l, j: (l, 0, j))],
        out_specs=pl.BlockSpec((1, cc.shape[0], tn), lambda l, j: (l, 0, j)),
        out_shape=jax.ShapeDtypeStruct((depth, cc.shape[0], n), F32),
        compiler_params=_cparams(2),
        name="ada",
    )(cc, w_ada, b_ada.reshape(depth, 1, n))


def _stream_specs(stream, row0):
    ctx_arr, lat_arr = stream if isinstance(stream, tuple) else (stream, stream)
    lat_tile0 = 0 if isinstance(stream, tuple) else 1
    blk = (1, TOK_TILE, lat_arr.shape[-1])
    lat_spec = pl.BlockSpec(blk, lambda b_, i: (b_, jnp.maximum(i + row0 - 1, 0) + lat_tile0, 0))
    if row0 > 0:
        return [lat_spec], [lat_arr]
    return [pl.BlockSpec(blk, lambda b_, i: (b_, 0, 0)), lat_spec], [ctx_arr, lat_arr]


def _stream_tile(x_refs):
    if len(x_refs) == 1:
        return x_refs[0][0]
    return jnp.where(pl.program_id(1) == 0, x_refs[0][0], x_refs[1][0])


def _proj_body(xc_ref, xl_ref, sh_ref, sc_ref, g_ref, w_ref, ca_ref, sa_ref, cb_ref, sb_ref, gq_ref,
               gk_ref, o_ref):
    h = _norm_modulate(_stream_tile((xc_ref, xl_ref)), g_ref[...], sh_ref[0, 0, 0], sc_ref[0, 0, 0])
    acc = _dot(h.astype(BF16), w_ref[...])
    ca, sa, cb, sb = ca_ref[...], sa_ref[...], cb_ref[...], sb_ref[...]
    bd = _block_diag_ones(LANES, 32)

    def get(m, j):
        c0 = (3 * m) * MIXER_W + j * LANES
        return acc[:, c0:c0 + LANES]

    def put(m, j, v):
        c0 = (3 * m) * MIXER_W + j * LANES
        o_ref[0, :, c0:c0 + LANES] = v.astype(BF16)

    def rope(x0, x1, c, s):
        return x0 * c - x1 * s, x0 * s + x1 * c

    def head_norm(x0, x1, g_ref_):
        gs = _dot_hilo(x0 * x0 + x1 * x1, bd)
        r = lax.rsqrt(gs * (1.0 / HEAD_DIM) + EPS)
        return x0 * r * g_ref_[0:1, :], x1 * r * g_ref_[1:2, :]

    for m in range(4):
        q0, q1, k0, k1 = get(m, 0), get(m, 1), get(m, 2), get(m, 3)
        if m == 0:
            q0, q1 = head_norm(q0, q1, gq_ref)
            k0, k1 = head_norm(k0, k1, gk_ref)
        if m in (0, 2):
            q0, q1 = rope(q0, q1, ca, sa)
            k0, k1 = rope(k0, k1, ca, sa)
        elif m == 1:
            q0, q1 = rope(q0, q1, cb, sb)
            k0, k1 = rope(k0, k1, cb, sb)
        qscale = LOG2E * (DIFF_DIM if m == 1 else HEAD_DIM) ** -0.5
        put(m, 0, q0 * qscale)
        put(m, 1, q1 * qscale)
        put(m, 2, k0)
        put(m, 3, k1)
        put(m, 4, get(m, 4))
        put(m, 5, get(m, 5))


def _proj(stream, mod, gain, w_perm, ca, sa, cb, sb, gq, gk):
    x_specs, x_args = _stream_specs(stream, 0)
    b = x_args[0].shape[0]
    lt = ca.shape[0]
    tm = TOK_TILE
    tab = pl.BlockSpec((tm, LANES), lambda b_, i: (i, 0))
    full = lambda a: pl.BlockSpec(a.shape, lambda b_, i: (0,) * a.ndim)
    return pl.pallas_call(
        _proj_body,
        grid=(b, lt // tm),
        in_specs=x_specs + [_mod_spec(0, 0, tm), _mod_spec(1, 0, tm),
                            full(gain), full(w_perm), tab, tab, tab, tab, full(gq), full(gk)],
        out_specs=pl.BlockSpec((1, tm, QKV_W), lambda b_, i: (b_, i, 0)),
        out_shape=jax.ShapeDtypeStruct((b, lt, QKV_W), BF16),
        compiler_params=_cparams(2),
        name="proj",
    )(*x_args, mod, mod, gain, w_perm, ca, sa, cb, sb, gq, gk)


def _sum_lane(h):
    return (HEAD_DIM * (h + 1)) % MIXER_W


def _fill_head_values(v_ref, vm_ref):
    lane = lax.broadcasted_iota(jnp.int32, (1, MIXER_W), 1)
    v = v_ref[0]
    for h in range(N_HEADS):
        tap = jnp.where(lane == _sum_lane(h), 1.0, 0.0).astype(BF16)
        vm_ref[h] = jnp.where(lane // HEAD_DIM == h, v, tap)


def _attend(qm, kk, vm, bias=None, valid=None, extra_logit=None):
    s = _dot_nt(qm, kk)
    if bias is not None:
        s = s + bias
    if valid is not None:
        s = jnp.where(valid, s, NEG_INF)
    m = jnp.max(s, axis=-1, keepdims=True)
    if extra_logit is not None:
        m = jnp.maximum(m, extra_logit)
    return _dot(jnp.exp2(s - m).astype(BF16), vm), m


def _attn_body(*refs, kind, with_ctx, lam_init, n_ctx, n_lat):
    refs = list(refs)
    scalar_ref = refs.pop(0) if kind in ("diff", "window") else None
    q_ref, k_ref, v_ref = refs[:3]
    refs = refs[3:]
    gain_ref = refs.pop(0) if kind == "diff" else None
    bias_refs = (refs.pop(0), refs.pop(0)) if kind == "nbr" else None
    o_ref, vm_ref = refs
    step = pl.program_id(1)
    lane = lax.broadcasted_iota(jnp.int32, (1, MIXER_W), 1)
    sub = ATTN_STEP_ROWS if kind in ("global", "diff") else TOK_TILE
    band = sub + 2 * WINDOW
    strip = NBR_STRIP_ROWS * GRID_W
    out0 = n_ctx if with_ctx else 0

    def rows_of(ref, ranges, *lead):
        parts = [ref[lead + (pl.ds(start, size), slice(None))] for start, size in ranges]
        return parts[0] if len(parts) == 1 else jnp.concatenate(parts, axis=0)

    def tile(q_row, n_rows, out_row, t=None, bias_ref=None):
        q = q_ref[0, pl.ds(q_row, n_rows), :]
        ranges, valid = [(0, n_ctx)], None
        if t is not None:
            if kind in ("global", "diff"):
                ranges = [(0, n_ctx + n_lat)]
            elif kind == "window":
                k_start = jnp.clip(t * sub - WINDOW, 0, n_lat - band)
                ranges.append((pl.multiple_of(n_ctx + k_start, LANES), band))
                col = lax.broadcasted_iota(jnp.int32, (1, n_ctx + band), 1)
                row = lax.broadcasted_iota(jnp.int32, (n_rows, 1), 0)
                dist = (k_start + col - n_ctx) - (t * sub + row)
                valid = jnp.abs(jnp.where(col < n_ctx, 0, dist)) <= WINDOW
            else:
                rs = _nbr_strip_start(t, n_lat // GRID_W)
                ranges.append((pl.multiple_of(n_ctx + rs * GRID_W, GRID_W), strip))
        kk = rows_of(k_ref, ranges, 0)
        out = jnp.zeros((n_rows, MIXER_W), F32)
        for h in range(N_HEADS):
            vm = rows_of(vm_ref, ranges, h)
            tap = _sum_lane(h)
            if kind == "diff":
                o = None
                for c in range(2):
                    qm = q * _lane_mask((lane % LANES) // (DIFF_DIM // 2) == 2 * h + c)
                    pv, _ = _attend(qm, kk, vm)
                    coef = 1.0 if c == 0 else scalar_ref[0]
                    term = pv * (coef / pv[:, tap:tap + 1])
                    o = term if c == 0 else o - term
            else:
                qm = q * _lane_mask((lane % LANES) // (HEAD_DIM // 2) == h)
                if kind == "window":
                    sink = scalar_ref[h] * LOG2E
                    pv, m = _attend(qm, kk, vm, valid=valid, extra_logit=sink)
                    o = pv * (1.0 / (pv[:, tap:tap + 1] + jnp.exp2(sink - m)))
                else:
                    bias = bias_ref[0, h, 0] if (kind == "nbr" and t is not None) else None
                    pv, _ = _attend(qm, kk, vm, bias=bias)
                    o = pv * (1.0 / pv[:, tap:tap + 1])
            out = jnp.where(lane // HEAD_DIM == h, o, out)
        if kind == "diff":
            gs = _dot_hilo(out * out, _block_diag_ones(MIXER_W, HEAD_DIM))
            out = out * lax.rsqrt(gs * (1.0 / HEAD_DIM) + EPS) * gain_ref[...] * (1.0 - lam_init)
        o_ref[0, pl.ds(out_row, n_rows), :] = out.astype(BF16)

    @pl.when(step == 0)
    def _():
        _fill_head_values(v_ref, vm_ref)
        if with_ctx:
            tile(0, n_ctx, 0)

    for j in range(ATTN_STEP_ROWS // sub):
        t = step * (ATTN_STEP_ROWS // sub) + j
        row = pl.multiple_of(t * sub, sub)
        tile(n_ctx + row, sub, out0 + row, t=t, bias_ref=bias_refs[j] if bias_refs else None)


def _attention(qkv, mixer, kind, with_ctx, *, scalars=None, gain=None, bias=None, layer=0, lam_init=0.0):
    b, lt, _ = qkv.shape
    n_lat = lt - TOK_TILE
    n_out = lt if with_ctx else n_lat
    col = lambda j: pl.BlockSpec((1, lt, MIXER_W), lambda b_, i: (b_, 0, 3 * mixer + j))
    in_specs, args = [col(0), col(1), col(2)], [qkv, qkv, qkv]
    if scalars is not None:
        in_specs, args = [pl.BlockSpec(memory_space=pltpu.SMEM)] + in_specs, [scalars] + args
    if gain is not None:
        in_specs.append(pl.BlockSpec(gain.shape, lambda b_, i: (0, 0)))
        args.append(gain)
    if bias is not None:
        n_sub = ATTN_STEP_ROWS // TOK_TILE
        n_tiles = n_lat // TOK_TILE
        for j in range(n_sub):
            in_specs.append(pl.BlockSpec(
                (1, N_HEADS, 1) + bias.shape[3:],
                lambda b_, i, j=j: (layer, 0, _nbr_pattern(i * n_sub + j, n_tiles), 0, 0)))
            args.append(bias)
    return pl.pallas_call(
        functools.partial(_attn_body, kind=kind, with_ctx=with_ctx, lam_init=lam_init,
                          n_ctx=TOK_TILE, n_lat=n_lat),
        grid=(b, n_lat // ATTN_STEP_ROWS),
        in_specs=in_specs,
        out_specs=pl.BlockSpec((1, n_out, MIXER_W), lambda b_, i: (b_, 0, 0)),
        out_shape=jax.ShapeDtypeStruct((b, n_out, MIXER_W), BF16),
        scratch_shapes=[pltpu.VMEM((N_HEADS, lt, MIXER_W), BF16)],
        compiler_params=_cparams(2),
        name="attn_" + kind,
    )(*args)


def _outproj_body(oa_ref, ob_ref, oc_ref, od_ref, w_ref, gate_ref, g_ref, *rest):
    x_refs, out_ref = rest[:-1], rest[-1]
    acc = None
    for m, o_ref in enumerate((oa_ref, ob_ref, oc_ref, od_ref)):
        t = _dot(o_ref[0], w_ref[m * MIXER_W:(m + 1) * MIXER_W, :])
        acc = t if acc is None else acc + t
    ms = jnp.mean(acc * acc, axis=-1, keepdims=True)
    y = (acc * lax.rsqrt(ms + EPS)) * g_ref[...]
    out_ref[0] = _stream_tile(x_refs) + gate_ref[0, 0, 0] * y


def _outproj(o_mix, w_out, stream, mod, gain, row0):
    x_specs, x_args = _stream_specs(stream, row0)
    b, n_tok, _ = o_mix[0].shape
    d = x_args[0].shape[-1]
    tm = TOK_TILE
    o_spec = pl.BlockSpec((1, tm, MIXER_W), lambda b_, i: (b_, i, 0))
    return pl.pallas_call(
        _outproj_body,
        grid=(b, n_tok // tm),
        in_specs=[o_spec, o_spec, o_spec, o_spec,
                  pl.BlockSpec(w_out.shape, lambda b_, i: (0, 0)),
                  _mod_spec(2, row0, tm),
                  pl.BlockSpec(gain.shape, lambda b_, i: (0, 0))] + x_specs,
        out_specs=pl.BlockSpec((1, tm, d), lambda b_, i: (b_, i, 0)),
        out_shape=jax.ShapeDtypeStruct((b, n_tok, d), F32),
        compiler_params=_cparams(2),
        name="outproj",
    )(*o_mix, w_out, mod, gain, *x_args)


def _route_body(x_ref, sh_ref, sc_ref, g_ref, wr_ref, h_ref, lg_ref):
    h = _norm_modulate(x_ref[0], g_ref[...], sh_ref[0, 0, 0], sc_ref[0, 0, 0])
    h_ref[0] = h.astype(BF16)
    w_hi, w_lo = _split(wr_ref[...])
    h_hi, h_lo = _split(h)
    lg_ref[0] = _dot_nt(w_hi, h_hi) + (_dot_nt(w_hi, h_lo) + _dot_nt(w_lo, h_hi))


def _route(xs, mod, gain, w_router_t, row0):
    b, n_tok, d = xs.shape
    tm = TOK_TILE
    n_e = w_router_t.shape[0]
    return pl.pallas_call(
        _route_body,
        grid=(b, n_tok // tm),
        in_specs=[pl.BlockSpec((1, tm, d), lambda b_, i: (b_, i, 0)),
                  _mod_spec(3, row0, tm), _mod_spec(4, row0, tm),
                  pl.BlockSpec(gain.shape, lambda b_, i: (0, 0)),
                  pl.BlockSpec(w_router_t.shape, lambda b_, i: (0, 0))],
        out_specs=[pl.BlockSpec((1, tm, d), lambda b_, i: (b_, i, 0)),
                   pl.BlockSpec((1, n_e, tm), lambda b_, i: (b_, 0, i))],
        out_shape=[jax.ShapeDtypeStruct((b, n_tok, d), BF16),
                   jax.ShapeDtypeStruct((b, n_e, n_tok), F32)],
        compiler_params=_cparams(2),
        name="route",
    )(xs, mod, mod, gain, w_router_t)


def _exclusive_cumsum(x, tri):
    off = jnp.zeros((x.shape[0], 1), F32)
    outs = []
    for c in range(x.shape[1] // LANES):
        xc = x[:, c * LANES:(c + 1) * LANES]
        inc = _dot(xc.astype(BF16), tri)
        outs.append(inc - xc + off)
        off = off + inc[:, LANES - 1:LANES]
    return jnp.concatenate(outs, axis=1)


def _select_body(lg_ref, pos_ref, aff_ref, *, segments):
    r = lax.broadcasted_iota(jnp.int32, (LANES, LANES), 0)
    c = lax.broadcasted_iota(jnp.int32, (LANES, LANES), 1)
    tri = jnp.where(r <= c, 1.0, 0.0).astype(BF16)
    for t0, t, cap, slot0 in segments:
        lg = lg_ref[0, :, t0:t0 + t]
        e = jnp.exp(lg - jnp.max(lg, axis=0, keepdims=True))
        aff = e / jnp.sum(e, axis=0, keepdims=True)

        def count_above(thr):
            return jnp.sum(jnp.where(aff > thr, 1.0, 0.0), axis=1, keepdims=True)

        def unsettled(carry):
            return jnp.logical_and(carry[2] > 0.0, carry[3] < BISECT_MAX_STEPS)

        def bisect(carry):
            lo, hi, _, step = carry
            mid = 0.5 * (lo + hi)
            cnt = count_above(mid)
            new_lo = jnp.where(cnt >= cap, mid, lo)
            new_hi = jnp.where(cnt <= cap, mid, hi)
            moving = jnp.logical_and(new_lo < new_hi, jnp.logical_and(mid > lo, mid < hi))
            return new_lo, new_hi, jnp.sum(jnp.where(moving, 1.0, 0.0)), step + 1

        lo0 = jnp.full((aff.shape[0], 1), -1.0, F32)
        hi0 = jnp.max(aff, axis=1, keepdims=True)
        lo, hi, _, _ = lax.while_loop(unsettled, bisect, (lo0, hi0, jnp.float32(1.0), jnp.int32(0)))
        gt = aff > hi
        eq = jnp.logical_and(aff > lo, aff <= hi)
        need = cap - count_above(hi)
        eq_rank = _exclusive_cumsum(jnp.where(eq, 1.0, 0.0), tri)
        sel = jnp.logical_or(gt, jnp.logical_and(eq, eq_rank < need))
        slot = _exclusive_cumsum(jnp.where(sel, 1.0, 0.0), tri) + slot0
        pos_ref[0, :, t0:t0 + t] = jnp.where(sel, slot, -1.0).astype(jnp.int32)
        aff_ref[0, :, t0:t0 + t] = aff


def _select(logits, segments):
    b, n_e, lt = logits.shape
    spec = pl.BlockSpec((1, n_e, lt), lambda b_: (b_, 0, 0))
    return pl.pallas_call(
        functools.partial(_select_body, segments=segments),
        grid=(b,),
        in_specs=[spec],
        out_specs=[spec, spec],
        out_shape=[jax.ShapeDtypeStruct((b, n_e, lt), jnp.int32),
                   jax.ShapeDtypeStruct((b, n_e, lt), F32)],
        compiler_params=_cparams(1),
        name="select",
    )(logits)


def _slot_hits(pos_row, n_slots):
    return lax.broadcasted_iota(jnp.int32, (n_slots, pos_row.shape[1]), 0) == pos_row


def _gather_body(pos_ref, h_ref, xe_ref):
    n_grp, n_slots = xe_ref.shape[1:3]
    e0 = pl.program_id(1) * n_grp
    onehot = jnp.concatenate(
        [jnp.where(_slot_hits(pos_ref[0, pl.ds(e0 + j, 1), :], n_slots), 1.0, 0.0).astype(BF16)
         for j in range(n_grp)], axis=0)
    xe = _dot(onehot, h_ref[0]).astype(BF16)
    xe_ref[0] = xe.reshape(n_grp, n_slots, xe.shape[1])


def _gather(pos, h, n_slots):
    b, n_e, n_tok = pos.shape
    d = h.shape[2]
    return pl.pallas_call(
        _gather_body,
        grid=(b, n_e // GATHER_EXPERTS),
        in_specs=[pl.BlockSpec((1, n_e, n_tok), lambda b_, g: (b_, 0, 0)),
                  pl.BlockSpec((1, n_tok, d), lambda b_, g: (b_, 0, 0))],
        out_specs=pl.BlockSpec((1, GATHER_EXPERTS, n_slots, d), lambda b_, g: (b_, g, 0, 0)),
        out_shape=jax.ShapeDtypeStruct((b, n_e, n_slots, d), BF16),
        compiler_params=_cparams(2),
        name="gather",
    )(pos, h)


def _experts_body(pos_ref, aff_ref, xe_ref, wg_ref, wu_ref, wd_ref, y_ref, wg_s, wu_s, wd_s):
    @pl.when(pl.program_id(1) == 0)
    def _():
        wg_s[...] = wg_ref[0, 0].astype(BF16)
        wu_s[...] = wu_ref[0, 0].astype(BF16)
        wd_s[...] = wd_ref[0, 0].astype(BF16)

    n_b, _, n_slots, d = xe_ref.shape
    e = pl.program_id(0)
    x = xe_ref[...].reshape(n_b * n_slots, d)
    gate = jnp.concatenate(
        [jnp.sum(jnp.where(_slot_hits(pos_ref[j, pl.ds(e, 1), :], n_slots),
                           aff_ref[j, pl.ds(e, 1), :], 0.0), axis=1, keepdims=True)
         for j in range(n_b)], axis=0)
    y = None
    for c0 in range(0, wg_s.shape[1], FFN_CHUNK):
        a = _dot(x, wg_s[:, c0:c0 + FFN_CHUNK])
        u = _dot(x, wu_s[:, c0:c0 + FFN_CHUNK])
        act = ((a * jax.nn.sigmoid(a)) * u).astype(BF16)
        t = _dot(act, wd_s[c0:c0 + FFN_CHUNK, :])
        y = t if y is None else y + t
    y_ref[...] = (y * gate).astype(BF16).reshape(n_b, 1, n_slots, d)


def _experts(pos, aff, xe, w_gate, w_up, w_down, layer):
    b, n_e, n_tok = pos.shape
    n_slots = xe.shape[2]
    d, f = w_gate.shape[2:]
    nb = FFN_BATCHES if b % FFN_BATCHES == 0 else 1
    row_spec = pl.BlockSpec((nb, n_e, n_tok), lambda e, g: (g, 0, 0))
    slot_spec = pl.BlockSpec((nb, 1, n_slots, d), lambda e, g: (g, e, 0, 0))
    return pl.pallas_call(
        _experts_body,
        grid=(n_e, b // nb),
        in_specs=[row_spec, row_spec, slot_spec,
                  pl.BlockSpec((1, 1, d, f), lambda e, g: (layer, e, 0, 0)),
                  pl.BlockSpec((1, 1, d, f), lambda e, g: (layer, e, 0, 0)),
                  pl.BlockSpec((1, 1, f, d), lambda e, g: (layer, e, 0, 0))],
        out_specs=slot_spec,
        out_shape=jax.ShapeDtypeStruct(xe.shape, BF16),
        scratch_shapes=[pltpu.VMEM((d, f), BF16), pltpu.VMEM((d, f), BF16), pltpu.VMEM((f, d), BF16)],
        compiler_params=_cparams(2),
        name="experts",
    )(pos, aff, xe, w_gate, w_up, w_down)


def _combine_body(pos_ref, y_ref, x_ref, gate_ref, g_ref, out_ref):
    n_e, n_slots = y_ref.shape[1:3]
    pos = pos_ref[0]
    slot = lax.broadcasted_iota(jnp.int32, (n_slots, pos.shape[1]), 0)
    onehot = jnp.concatenate(
        [jnp.where(slot == pos[e:e + 1, :], 1.0, 0.0).astype(BF16) for e in range(n_e)], axis=0)
    acc = lax.dot_general(onehot, y_ref[0].reshape(n_e * n_slots, y_ref.shape[3]),
                          (((0,), (0,)), ((), ())), preferred_element_type=F32)
    ms = jnp.mean(acc * acc, axis=-1, keepdims=True)
    y = (acc * lax.rsqrt(ms + EPS)) * g_ref[...]
    out_ref[0] = x_ref[0] + gate_ref[0, 0, 0] * y


def _combine(pos, y, xs, mod, gain, row0):
    b, n_tok, d = xs.shape
    tm = TOK_TILE
    n_e, n_slots = y.shape[1:3]
    return pl.pallas_call(
        _combine_body,
        grid=(b, n_tok // tm),
        in_specs=[pl.BlockSpec((1, n_e, tm), lambda b_, i: (b_, 0, i)),
                  pl.BlockSpec((1, n_e, n_slots, d), lambda b_, i: (b_, 0, 0, 0)),
                  pl.BlockSpec((1, tm, d), lambda b_, i: (b_, i, 0)),
                  _mod_spec(5, row0, tm),
                  pl.BlockSpec(gain.shape, lambda b_, i: (0, 0))],
        out_specs=pl.BlockSpec((1, tm, d), lambda b_, i: (b_, i, 0)),
        out_shape=jax.ShapeDtypeStruct(xs.shape, F32),
        compiler_params=_cparams(2),
        name="combine",
    )(pos, y, xs, mod, gain)


def kernel(x, c, ctx, c_ctx, w_ada, b_ada, g_pre_mix, g_post_mix, g_pre_ffn, g_post_ffn, w_in, w_out, q_gain_a, k_gain_a, lam_q1, lam_k1, lam_q2, lam_k2, subln_gain_b, sink_c, rpb_d, w_router, w_gate, w_up, w_down):
    b, n_lat, d = x.shape
    n_ctx = ctx.shape[1]
    depth = w_ada.shape[0]
    assert n_ctx == TOK_TILE and n_lat % ATTN_STEP_ROWS == 0 and d == D_MODEL and b + 1 <= 16

    stream = (ctx, x)
    cc = jnp.zeros((16, d), F32).at[:b].set(c).at[b].set(c_ctx)
    mod_all = _ada(cc, w_ada, b_ada)
    mod_all = jnp.stack([jnp.broadcast_to(mod_all[:, b:b + 1], (depth, b, 6 * d)), mod_all[:, :b]],
                        axis=2).reshape(depth, b, 2, 6, 1, d)
    ca, sa = _rope_lane_tables(n_ctx, n_lat, HEAD_DIM)
    cb, sb = _rope_lane_tables(n_ctx, n_lat, DIFF_DIM)
    w_perm_all = _permute_w_in(w_in)
    w_out_all = w_out.astype(BF16)
    w_router_t = jnp.swapaxes(w_router, 1, 2)
    nbr_bias = _neighbourhood_bias(rpb_d, n_ctx, n_lat // GRID_W)
    pair_gain = lambda g: jnp.stack([jnp.tile(g[:, 0::2], (1, 4)), jnp.tile(g[:, 1::2], (1, 4))], axis=1)
    gq_all, gk_all = pair_gain(q_gain_a), pair_gain(k_gain_a)
    row = lambda v: v.reshape(1, -1)

    for l in range(depth):
        with_ctx = l < depth - 1
        row0 = 0 if with_ctx else 1
        lam_init = 0.8 - 0.6 * math.exp(-0.3 * l)
        mod = mod_all[l]
        qkv = _proj(stream, mod, row(g_pre_mix[l]), w_perm_all[l], ca, sa, cb, sb, gq_all[l], gk_all[l])

        lam = (jnp.exp(jnp.sum(lam_q1[l] * lam_k1[l])) - jnp.exp(jnp.sum(lam_q2[l] * lam_k2[l]))
               + lam_init).reshape(1)
        o_mix = (
            _attention(qkv, 0, "global", with_ctx),
            _attention(qkv, 1, "diff", with_ctx, scalars=lam, lam_init=lam_init,
                       gain=row(jnp.tile(subln_gain_b[l], N_HEADS))),
            _attention(qkv, 2, "window", with_ctx, scalars=sink_c[l]),
            _attention(qkv, 3, "nbr", with_ctx, bias=nbr_bias, layer=l),
        )
        xs = _outproj(o_mix, w_out_all[l], stream, mod, row(g_post_mix[l]), row0)

        h, logits = _route(xs, mod, row(g_pre_ffn[l]), w_router_t[l], row0)
        cap_lat = CAPACITY_FACTOR * n_lat // N_EXPERTS
        cap_ctx = CAPACITY_FACTOR * n_ctx // N_EXPERTS
        if with_ctx:
            segments = ((0, n_ctx, cap_ctx, 0), (n_ctx, n_lat, cap_lat, cap_ctx))
            n_slots = cap_ctx + cap_lat
        else:
            segments = ((0, n_lat, cap_lat, 0),)
            n_slots = cap_lat
        pos, aff = _select(logits, segments)
        y = _experts(pos, aff, _gather(pos, h, n_slots), w_gate, w_up, w_down, l)
        stream = _combine(pos, y, xs, mod, row(g_post_ffn[l]), row0)
    return stream
```

```python
import functools
import math

import numpy as np
import jax
import jax.numpy as jnp
from jax import lax
from jax.experimental import pallas as pl
from jax.experimental.pallas import tpu as pltpu

F32 = jnp.float32
BF16 = jnp.bfloat16

D_MODEL = 1024
GRID_W = 64
HEAD_DIM = 64
DIFF_DIM = 32
N_HEADS = 4
MIXER_W = N_HEADS * HEAD_DIM
QKV_W = 4 * 3 * MIXER_W
WINDOW = 128
NA_ROWS = 8
NA_COLS = 16
ROPE_BASE = 10000.0
N_EXPERTS = 16
CAPACITY_FACTOR = 2
EPS = 1e-6
NEG_INF = -1e30
LOG2E = 1.4426950408889634
PROJ_WIDTHS = (256, 128, 128, 256, 256, 256, 256, 128, 128, 256, 256, 256)
LANES = 128
TOK_TILE = 256
VMEM_LIMIT = 56 * 1024 * 1024
ATTN_STEP_ROWS = 512
NBR_TILE_ROWS = TOK_TILE // GRID_W
NBR_STRIP_ROWS = 12
GATHER_EXPERTS = 4
FFN_BATCHES = 4
FFN_CHUNK = 256
BISECT_MAX_STEPS = 192


def _cparams(n_axes):
    return pltpu.CompilerParams(dimension_semantics=("arbitrary",) * n_axes,
                                vmem_limit_bytes=VMEM_LIMIT)


def _proj_source_columns():
    offs = np.concatenate([[0], np.cumsum(PROJ_WIDTHS)])

    def pairs(base, n_slots, slot_src, n_pairs):
        x0 = [base + slot_src(j) + 2 * i for j in range(n_slots) for i in range(n_pairs)]
        return x0 + [c + 1 for c in x0]

    cols = []
    for m in range(4):
        q0, k0, v0 = offs[3 * m], offs[3 * m + 1], offs[3 * m + 2]
        if m in (0, 2):
            cols += pairs(q0, 4, lambda j: j * HEAD_DIM, 32)
            cols += pairs(k0, 4, lambda j: (j // 2) * HEAD_DIM, 32)
            cols += [v0 + (j // 2) * HEAD_DIM + d for j in range(4) for d in range(HEAD_DIM)]
        elif m == 1:
            cols += pairs(q0, 8, lambda j: j * DIFF_DIM, 16)
            cols += pairs(k0, 8, lambda j: j * DIFF_DIM, 16)
            cols += [v0 + c for c in range(MIXER_W)]
        else:
            cols += pairs(q0, 4, lambda j: j * HEAD_DIM, 32)
            cols += pairs(k0, 4, lambda j: j * HEAD_DIM, 32)
            cols += [v0 + c for c in range(MIXER_W)]
    return np.asarray(cols, dtype=np.int32)


def _permute_w_in(w_in):
    src = _proj_source_columns()
    pick = (jnp.arange(w_in.shape[-1], dtype=jnp.int32)[:, None] == src[None, :]).astype(BF16)
    return jnp.einsum('ldk,kn->ldn', w_in.astype(BF16), pick, preferred_element_type=BF16)


def _rope_tables(n_tokens, dim):
    t = jnp.arange(n_tokens, dtype=jnp.int32)
    rows = (t // GRID_W).astype(F32)
    cols = (t % GRID_W).astype(F32)
    n_axis = dim // 4
    inv_freq = ROPE_BASE ** (-jnp.arange(n_axis, dtype=F32) / n_axis)
    ang = jnp.concatenate([rows[:, None] * inv_freq, cols[:, None] * inv_freq], axis=-1)
    return jnp.cos(ang), jnp.sin(ang)


def _rope_lane_tables(n_ctx, n_lat, dim):
    cos, sin = _rope_tables(n_lat, dim)
    reps = LANES // cos.shape[1]
    cos = jnp.concatenate([jnp.ones((n_ctx, LANES), F32), jnp.tile(cos, (1, reps))], axis=0)
    sin = jnp.concatenate([jnp.zeros((n_ctx, LANES), F32), jnp.tile(sin, (1, reps))], axis=0)
    return cos, sin


def _nbr_strip_start(tile, n_rows):
    lo = tile * NBR_TILE_ROWS - NA_ROWS // 2
    return jnp.clip(lo, 0, n_rows - NBR_STRIP_ROWS) if isinstance(lo, jax.Array) else int(
        np.clip(lo, 0, n_rows - NBR_STRIP_ROWS))


def _nbr_pattern(tile, n_tiles):
    if isinstance(tile, jax.Array):
        return jnp.where(tile == 0, 0, jnp.where(tile == n_tiles - 1, 2, 1))
    return 0 if tile == 0 else (2 if tile == n_tiles - 1 else 1)


def _neighbourhood_bias(rpb, n_ctx, n_rows):
    n_tiles = n_rows // NBR_TILE_ROWS
    assert n_tiles >= 3 and n_rows >= NBR_STRIP_ROWS

    def rows_of(tile):
        ss = _nbr_strip_start(tile, n_rows)
        r = tile * NBR_TILE_ROWS + np.arange(NBR_TILE_ROWS)[:, None]
        rs = np.clip(r - NA_ROWS // 2, 0, n_rows - NA_ROWS)
        kr = ss + np.arange(NBR_STRIP_ROWS)[None, :]
        valid = (kr >= rs) & (kr < rs + NA_ROWS)
        return valid, np.where(valid, kr - r + NA_ROWS - 1, 0)

    reps = [rows_of(t) for t in (0, 1, n_tiles - 1)]
    for t in range(n_tiles):
        v, ri = rows_of(t)
        assert np.array_equal(v, reps[_nbr_pattern(t, n_tiles)][0])
        assert np.array_equal(ri, reps[_nbr_pattern(t, n_tiles)][1])
    row_valid = np.stack([v for v, _ in reps])
    ri = np.stack([r for _, r in reps])
    oh_r = (ri[..., None] == np.arange(2 * NA_ROWS - 1)) & row_valid[..., None]
    cq = np.arange(GRID_W)
    col_start = np.clip(cq - NA_COLS // 2, 0, GRID_W - NA_COLS)
    col_valid = (cq[None, :] >= col_start[:, None]) & (cq[None, :] < col_start[:, None] + NA_COLS)
    ci = np.clip(cq[None, :] - cq[:, None] + NA_COLS - 1, 0, 2 * NA_COLS - 2)
    oh_c = np.arange(2 * NA_COLS - 1)[:, None, None] == ci[None]
    bias = jnp.einsum('lhrc,pajr,cqw->lhpaqjw', rpb.astype(F32), oh_r.astype(np.float32),
                      oh_c.astype(np.float32), precision=lax.Precision.HIGHEST)
    valid = row_valid[None, None, :, :, None, :, None] & col_valid[None, None, None, None, :, None, :]
    bias = jnp.where(valid, bias * LOG2E, NEG_INF)
    bias = bias.reshape(bias.shape[:2] + (3, TOK_TILE, NBR_STRIP_ROWS * GRID_W))
    return jnp.concatenate([jnp.zeros(bias.shape[:4] + (n_ctx,), F32), bias], axis=-1)


def _block_diag_ones(n, group):
    r = lax.broadcasted_iota(jnp.int32, (n, n), 0) // group
    c = lax.broadcasted_iota(jnp.int32, (n, n), 1) // group
    return jnp.where(r == c, 1.0, 0.0).astype(BF16)


def _dot(a, b):
    return jnp.dot(a, b, preferred_element_type=F32)


def _dot_nt(a, b):
    return lax.dot_general(a, b, (((1,), (1,)), ((), ())), preferred_element_type=F32)


def _split(a):
    hi = a.astype(BF16)
    return hi, (a - hi.astype(F32)).astype(BF16)


def _dot_hilo(a, b_bf16):
    hi, lo = _split(a)
    return _dot(hi, b_bf16) + _dot(lo, b_bf16)


def _lane_mask(cond):
    return jnp.where(cond, 1.0, 0.0).astype(BF16)


def _norm_modulate(x, gain, shift, scale):
    ms = jnp.mean(x * x, axis=-1, keepdims=True)
    h = (x * lax.rsqrt(ms + EPS)) * gain
    return h * (1.0 + scale) + shift


def _token_tile(n_tok):
    return next(t for t in (768, 512, TOK_TILE) if n_tok % t == 0)


def _mod_specs(chunk):
    return [pl.BlockSpec((1, 1, 1, 1, D_MODEL), lambda b, i, r=r: (b, r, chunk, 0, 0)) for r in (0, 1)]


def _tile_mod(mc_ref, ml_ref, n_rows, row0):
    lat = ml_ref[0, 0, 0]
    if row0 > 0:
        return lat
    row = lax.broadcasted_iota(jnp.int32, (n_rows, 1), 0)
    is_ctx = jnp.logical_and(row < TOK_TILE, pl.program_id(1) == 0)
    return jnp.where(is_ctx, mc_ref[0, 0, 0], lat)


def _ada_body(c_ref, w_ref, b_ref, o_ref):
    c = c_ref[...]
    cs = (c * jax.nn.sigmoid(c)).astype(BF16)
    o_ref[0] = _dot(cs, w_ref[0].astype(BF16)) + b_ref[0]


def _ada(cc, w_ada, b_ada):
    depth, d, n = w_ada.shape
    tn = 512
    return pl.pallas_call(
        _ada_body,
        grid=(depth, n // tn),
        in_specs=[pl.BlockSpec(cc.shape, lambda l, j: (0, 0)),
                  pl.BlockSpec((1, d, tn), lambda l, j: (l, 0, j)),
                  pl.BlockSpec((1, 1, tn), lambda l, j: (l, 0, j))],
        out_specs=pl.BlockSpec((1, cc.shape[0], tn), lambda l, j: (l, 0, j)),
        out_shape=jax.ShapeDtypeStruct((depth, cc.shape[0], n), F32),
        compiler_params=_cparams(2),
        name="ada",
    )(cc, w_ada, b_ada.reshape(depth, 1, n))


def _stream_specs(stream, row0, tm):
    ctx_arr, lat_arr = stream if isinstance(stream, tuple) else (stream, stream)
    lat_tile0 = 0 if isinstance(stream, tuple) else 1
    k = tm // TOK_TILE
    blk = (1, TOK_TILE, lat_arr.shape[-1])
    specs = [pl.BlockSpec(blk, lambda b_, i, j=j: (b_, jnp.maximum(i * k + row0 + j - 1, 0) + lat_tile0, 0))
             for j in range(k)]
    args = [lat_arr] * k
    if row0 == 0:
        specs, args = [pl.BlockSpec(blk, lambda b_, i: (b_, 0, 0))] + specs, [ctx_arr] + args
    return specs, args


def _stream_tile(x_refs, row0):
    parts = [r[0] for r in x_refs]
    if row0 == 0:
        parts = [jnp.where(pl.program_id(1) == 0, parts[0], parts[1])] + parts[2:]
    return parts[0] if len(parts) == 1 else jnp.concatenate(parts, axis=0)


def _proj_body(*refs):
    (shc_ref, shl_ref, scc_ref, scl_ref, g_ref, w_ref, ca_ref, sa_ref, cb_ref, sb_ref, gq_ref, gk_ref,
     o_ref) = refs[-13:]
    x = _stream_tile(refs[:-13], 0)
    tm = x.shape[0]
    h = _norm_modulate(x, g_ref[...], _tile_mod(shc_ref, shl_ref, tm, 0), _tile_mod(scc_ref, scl_ref, tm, 0))
    acc = _dot(h.astype(BF16), w_ref[...])
    ca, sa, cb, sb = ca_ref[...], sa_ref[...], cb_ref[...], sb_ref[...]
    bd = _block_diag_ones(LANES, 32)

    def get(m, j):
        c0 = (3 * m) * MIXER_W + j * LANES
        return acc[:, c0:c0 + LANES]

    def put(m, j, v):
        c0 = (3 * m) * MIXER_W + j * LANES
        o_ref[0, :, c0:c0 + LANES] = v.astype(BF16)

    def rope(x0, x1, c, s):
        return x0 * c - x1 * s, x0 * s + x1 * c

    def head_norm(x0, x1, g_ref_):
        gs = _dot_hilo(x0 * x0 + x1 * x1, bd)
        r = lax.rsqrt(gs * (1.0 / HEAD_DIM) + EPS)
        return x0 * r * g_ref_[0:1, :], x1 * r * g_ref_[1:2, :]

    for m in range(4):
        q0, q1, k0, k1 = get(m, 0), get(m, 1), get(m, 2), get(m, 3)
        if m == 0:
            q0, q1 = head_norm(q0, q1, gq_ref)
            k0, k1 = head_norm(k0, k1, gk_ref)
        if m in (0, 2):
            q0, q1 = rope(q0, q1, ca, sa)
            k0, k1 = rope(k0, k1, ca, sa)
        elif m == 1:
            q0, q1 = rope(q0, q1, cb, sb)
            k0, k1 = rope(k0, k1, cb, sb)
        qscale = LOG2E * (DIFF_DIM if m == 1 else HEAD_DIM) ** -0.5
        put(m, 0, q0 * qscale)
        put(m, 1, q1 * qscale)
        put(m, 2, k0)
        put(m, 3, k1)
        put(m, 4, get(m, 4))
        put(m, 5, get(m, 5))


def _proj(stream, mod, gain, w_perm, ca, sa, cb, sb, gq, gk):
    lt = ca.shape[0]
    tm = _token_tile(lt)
    x_specs, x_args = _stream_specs(stream, 0, tm)
    b = x_args[0].shape[0]
    tab = pl.BlockSpec((tm, LANES), lambda b_, i: (i, 0))
    full = lambda a: pl.BlockSpec(a.shape, lambda b_, i: (0,) * a.ndim)
    return pl.pallas_call(
        _proj_body,
        grid=(b, lt // tm),
        in_specs=x_specs + _mod_specs(0) + _mod_specs(1) + [
            full(gain), full(w_perm), tab, tab, tab, tab, full(gq), full(gk)],
        out_specs=pl.BlockSpec((1, tm, QKV_W), lambda b_, i: (b_, i, 0)),
        out_shape=jax.ShapeDtypeStruct((b, lt, QKV_W), BF16),
        compiler_params=_cparams(2),
        name="proj",
    )(*x_args, mod, mod, mod, mod, gain, w_perm, ca, sa, cb, sb, gq, gk)


def _sum_lane(h):
    return (HEAD_DIM * (h + 1)) % MIXER_W


def _fill_head_values(v_ref, vm_ref):
    lane = lax.broadcasted_iota(jnp.int32, (1, MIXER_W), 1)
    v = v_ref[0]
    for h in range(N_HEADS):
        tap = jnp.where(lane == _sum_lane(h), 1.0, 0.0).astype(BF16)
        vm_ref[h] = jnp.where(lane // HEAD_DIM == h, v, tap)


def _attend(qm, kk, vm, bias=None, valid=None, extra_logit=None):
    s = _dot_nt(qm, kk)
    if bias is not None:
        s = s + bias
    if valid is not None:
        s = jnp.where(valid, s, NEG_INF)
    m = jnp.max(s, axis=-1, keepdims=True)
    if extra_logit is not None:
        m = jnp.maximum(m, extra_logit)
    return _dot(jnp.exp2(s - m).astype(BF16), vm), m


def _attn_body(*refs, kind, with_ctx, lam_init, n_ctx, n_lat):
    refs = list(refs)
    scalar_ref = refs.pop(0) if kind in ("diff", "window") else None
    q_ref, k_ref, v_ref = refs[:3]
    refs = refs[3:]
    gain_ref = refs.pop(0) if kind == "diff" else None
    bias_refs = (refs.pop(0), refs.pop(0)) if kind == "nbr" else None
    o_ref, vm_ref = refs
    step = pl.program_id(1)
    lane = lax.broadcasted_iota(jnp.int32, (1, MIXER_W), 1)
    sub = ATTN_STEP_ROWS if kind in ("global", "diff") else TOK_TILE
    band = sub + 2 * WINDOW
    strip = NBR_STRIP_ROWS * GRID_W
    out0 = n_ctx if with_ctx else 0

    def rows_of(ref, ranges, *lead):
        parts = [ref[lead + (pl.ds(start, size), slice(None))] for start, size in ranges]
        return parts[0] if len(parts) == 1 else jnp.concatenate(parts, axis=0)

    def tile(q_row, n_rows, out_row, t=None, bias_ref=None):
        q = q_ref[0, pl.ds(q_row, n_rows), :]
        ranges, valid = [(0, n_ctx)], None
        if t is not None:
            if kind in ("global", "diff"):
                ranges = [(0, n_ctx + n_lat)]
            elif kind == "window":
                k_start = jnp.clip(t * sub - WINDOW, 0, n_lat - band)
                ranges.append((pl.multiple_of(n_ctx + k_start, LANES), band))
                col = lax.broadcasted_iota(jnp.int32, (1, n_ctx + band), 1)
                row = lax.broadcasted_iota(jnp.int32, (n_rows, 1), 0)
                dist = (k_start + col - n_ctx) - (t * sub + row)
                valid = jnp.abs(jnp.where(col < n_ctx, 0, dist)) <= WINDOW
            else:
                rs = _nbr_strip_start(t, n_lat // GRID_W)
                ranges.append((pl.multiple_of(n_ctx + rs * GRID_W, GRID_W), strip))
        kk = rows_of(k_ref, ranges, 0)
        out = jnp.zeros((n_rows, MIXER_W), F32)
        for h in range(N_HEADS):
            vm = rows_of(vm_ref, ranges, h)
            tap = _sum_lane(h)
            if kind == "diff":
                o = None
                for c in range(2):
                    qm = q * _lane_mask((lane % LANES) // (DIFF_DIM // 2) == 2 * h + c)
                    pv, _ = _attend(qm, kk, vm)
                    coef = 1.0 if c == 0 else scalar_ref[0]
                    term = pv * (coef / pv[:, tap:tap + 1])
                    o = term if c == 0 else o - term
            else:
                qm = q * _lane_mask((lane % LANES) // (HEAD_DIM // 2) == h)
                if kind == "window":
                    sink = scalar_ref[h] * LOG2E
                    pv, m = _attend(qm, kk, vm, valid=valid, extra_logit=sink)
                    o = pv * (1.0 / (pv[:, tap:tap + 1] + jnp.exp2(sink - m)))
                else:
                    bias = bias_ref[0, h, 0] if (kind == "nbr" and t is not None) else None
                    pv, _ = _attend(qm, kk, vm, bias=bias)
                    o = pv * (1.0 / pv[:, tap:tap + 1])
            out = jnp.where(lane // HEAD_DIM == h, o, out)
        if kind == "diff":
            gs = _dot_hilo(out * out, _block_diag_ones(MIXER_W, HEAD_DIM))
            out = out * lax.rsqrt(gs * (1.0 / HEAD_DIM) + EPS) * gain_ref[...] * (1.0 - lam_init)
        o_ref[0, pl.ds(out_row, n_rows), :] = out.astype(BF16)

    @pl.when(step == 0)
    def _():
        _fill_head_values(v_ref, vm_ref)
        if with_ctx:
            tile(0, n_ctx, 0)

    for j in range(ATTN_STEP_ROWS // sub):
        t = step * (ATTN_STEP_ROWS // sub) + j
        row = pl.multiple_of(t * sub, sub)
        tile(n_ctx + row, sub, out0 + row, t=t, bias_ref=bias_refs[j] if bias_refs else None)


def _attention(qkv, mixer, kind, with_ctx, *, scalars=None, gain=None, bias=None, layer=0, lam_init=0.0):
    b, lt, _ = qkv.shape
    n_lat = lt - TOK_TILE
    n_out = lt if with_ctx else n_lat
    col = lambda j: pl.BlockSpec((1, lt, MIXER_W), lambda b_, i: (b_, 0, 3 * mixer + j))
    in_specs, args = [col(0), col(1), col(2)], [qkv, qkv, qkv]
    if scalars is not None:
        in_specs, args = [pl.BlockSpec(memory_space=pltpu.SMEM)] + in_specs, [scalars] + args
    if gain is not None:
        in_specs.append(pl.BlockSpec(gain.shape, lambda b_, i: (0, 0)))
        args.append(gain)
    if bias is not None:
        n_sub = ATTN_STEP_ROWS // TOK_TILE
        n_tiles = n_lat // TOK_TILE
        for j in range(n_sub):
            in_specs.append(pl.BlockSpec(
                (1, N_HEADS, 1) + bias.shape[3:],
                lambda b_, i, j=j: (layer, 0, _nbr_pattern(i * n_sub + j, n_tiles), 0, 0)))
            args.append(bias)
    return pl.pallas_call(
        functools.partial(_attn_body, kind=kind, with_ctx=with_ctx, lam_init=lam_init,
                          n_ctx=TOK_TILE, n_lat=n_lat),
        grid=(b, n_lat // ATTN_STEP_ROWS),
        in_specs=in_specs,
        out_specs=pl.BlockSpec((1, n_out, MIXER_W), lambda b_, i: (b_, 0, 0)),
        out_shape=jax.ShapeDtypeStruct((b, n_out, MIXER_W), BF16),
        scratch_shapes=[pltpu.VMEM((N_HEADS, lt, MIXER_W), BF16)],
        compiler_params=_cparams(2),
        name="attn_" + kind,
    )(*args)


def _outproj_body(oa_ref, ob_ref, oc_ref, od_ref, w_ref, gc_ref, gl_ref, g_ref, *rest, row0):
    x_refs, out_ref = rest[:-1], rest[-1]
    acc = None
    for m, o_ref in enumerate((oa_ref, ob_ref, oc_ref, od_ref)):
        t = _dot(o_ref[0], w_ref[m * MIXER_W:(m + 1) * MIXER_W, :])
        acc = t if acc is None else acc + t
    ms = jnp.mean(acc * acc, axis=-1, keepdims=True)
    y = (acc * lax.rsqrt(ms + EPS)) * g_ref[...]
    out_ref[0] = _stream_tile(x_refs, row0) + _tile_mod(gc_ref, gl_ref, acc.shape[0], row0) * y


def _outproj(o_mix, w_out, stream, mod, gain, row0):
    b, n_tok, _ = o_mix[0].shape
    tm = _token_tile(n_tok)
    x_specs, x_args = _stream_specs(stream, row0, tm)
    d = x_args[0].shape[-1]
    o_spec = pl.BlockSpec((1, tm, MIXER_W), lambda b_, i: (b_, i, 0))
    return pl.pallas_call(
        functools.partial(_outproj_body, row0=row0),
        grid=(b, n_tok // tm),
        in_specs=[o_spec, o_spec, o_spec, o_spec,
                  pl.BlockSpec(w_out.shape, lambda b_, i: (0, 0))] + _mod_specs(2) + [
                  pl.BlockSpec(gain.shape, lambda b_, i: (0, 0))] + x_specs,
        out_specs=pl.BlockSpec((1, tm, d), lambda b_, i: (b_, i, 0)),
        out_shape=jax.ShapeDtypeStruct((b, n_tok, d), F32),
        compiler_params=_cparams(2),
        name="outproj",
    )(*o_mix, w_out, mod, mod, gain, *x_args)


def _route_body(x_ref, shc_ref, shl_ref, scc_ref, scl_ref, g_ref, wr_ref, h_ref, lg_ref, *, row0):
    tm = x_ref.shape[1]
    h = _norm_modulate(x_ref[0], g_ref[...], _tile_mod(shc_ref, shl_ref, tm, row0),
                       _tile_mod(scc_ref, scl_ref, tm, row0))
    h_ref[0] = h.astype(BF16)
    w_hi, w_lo = _split(wr_ref[...])
    h_hi, h_lo = _split(h)
    lg_ref[0] = _dot_nt(w_hi, h_hi) + (_dot_nt(w_hi, h_lo) + _dot_nt(w_lo, h_hi))


def _route(xs, mod, gain, w_router_t, row0):
    b, n_tok, d = xs.shape
    tm = _token_tile(n_tok)
    n_e = w_router_t.shape[0]
    return pl.pallas_call(
        functools.partial(_route_body, row0=row0),
        grid=(b, n_tok // tm),
        in_specs=[pl.BlockSpec((1, tm, d), lambda b_, i: (b_, i, 0))] + _mod_specs(3) + _mod_specs(4) + [
                  pl.BlockSpec(gain.shape, lambda b_, i: (0, 0)),
                  pl.BlockSpec(w_router_t.shape, lambda b_, i: (0, 0))],
        out_specs=[pl.BlockSpec((1, tm, d), lambda b_, i: (b_, i, 0)),
                   pl.BlockSpec((1, n_e, tm), lambda b_, i: (b_, 0, i))],
        out_shape=[jax.ShapeDtypeStruct((b, n_tok, d), BF16),
                   jax.ShapeDtypeStruct((b, n_e, n_tok), F32)],
        compiler_params=_cparams(2),
        name="route",
    )(xs, mod, mod, mod, mod, gain, w_router_t)


def _exclusive_cumsum(x, tri):
    off = jnp.zeros((x.shape[0], 1), F32)
    outs = []
    for c in range(x.shape[1] // LANES):
        xc = x[:, c * LANES:(c + 1) * LANES]
        inc = _dot(xc.astype(BF16), tri)
        outs.append(inc - xc + off)
        off = off + inc[:, LANES - 1:LANES]
    return jnp.concatenate(outs, axis=1)


def _select_body(lg_ref, pos_ref, aff_ref, *, segments):
    r = lax.broadcasted_iota(jnp.int32, (LANES, LANES), 0)
    c = lax.broadcasted_iota(jnp.int32, (LANES, LANES), 1)
    tri = jnp.where(r <= c, 1.0, 0.0).astype(BF16)
    for t0, t, cap, slot0 in segments:
        lg = lg_ref[0, :, t0:t0 + t]
        e = jnp.exp(lg - jnp.max(lg, axis=0, keepdims=True))
        aff = e / jnp.sum(e, axis=0, keepdims=True)

        def count_above(thr):
            return jnp.sum(jnp.where(aff > thr, 1.0, 0.0), axis=1, keepdims=True)

        def unsettled(carry):
            return jnp.logical_and(carry[2] > 0.0, carry[3] < BISECT_MAX_STEPS)

        def bisect(carry):
            lo, hi, _, step = carry
            mid = 0.5 * (lo + hi)
            cnt = count_above(mid)
            new_lo = jnp.where(cnt >= cap, mid, lo)
            new_hi = jnp.where(cnt <= cap, mid, hi)
            moving = jnp.logical_and(new_lo < new_hi, jnp.logical_and(mid > lo, mid < hi))
            return new_lo, new_hi, jnp.sum(jnp.where(moving, 1.0, 0.0)), step + 1

        lo0 = jnp.full((aff.shape[0], 1), -1.0, F32)
        hi0 = jnp.max(aff, axis=1, keepdims=True)
        lo, hi, _, _ = lax.while_loop(unsettled, bisect, (lo0, hi0, jnp.float32(1.0), jnp.int32(0)))
        gt = aff > hi
        eq = jnp.logical_and(aff > lo, aff <= hi)
        need = cap - count_above(hi)
        eq_rank = _exclusive_cumsum(jnp.where(eq, 1.0, 0.0), tri)
        sel = jnp.logical_or(gt, jnp.logical_and(eq, eq_rank < need))
        slot = _exclusive_cumsum(jnp.where(sel, 1.0, 0.0), tri) + slot0
        pos_ref[0, :, t0:t0 + t] = jnp.where(sel, slot, -1.0).astype(jnp.int32)
        aff_ref[0, :, t0:t0 + t] = aff


def _select(logits, segments):
    b, n_e, lt = logits.shape
    spec = pl.BlockSpec((1, n_e, lt), lambda b_: (b_, 0, 0))
    return pl.pallas_call(
        functools.partial(_select_body, segments=segments),
        grid=(b,),
        in_specs=[spec],
        out_specs=[spec, spec],
        out_shape=[jax.ShapeDtypeStruct((b, n_e, lt), jnp.int32),
                   jax.ShapeDtypeStruct((b, n_e, lt), F32)],
        compiler_params=_cparams(1),
        name="select",
    )(logits)


def _slot_hits(pos_row, n_slots):
    return lax.broadcasted_iota(jnp.int32, (n_slots, pos_row.shape[1]), 0) == pos_row


def _gather_body(pos_ref, h_ref, xe_ref):
    n_grp, n_slots = xe_ref.shape[1:3]
    e0 = pl.program_id(1) * n_grp
    onehot = jnp.concatenate(
        [jnp.where(_slot_hits(pos_ref[0, pl.ds(e0 + j, 1), :], n_slots), 1.0, 0.0).astype(BF16)
         for j in range(n_grp)], axis=0)
    xe = _dot(onehot, h_ref[0]).astype(BF16)
    xe_ref[0] = xe.reshape(n_grp, n_slots, xe.shape[1])


def _gather(pos, h, n_slots):
    b, n_e, n_tok = pos.shape
    d = h.shape[2]
    return pl.pallas_call(
        _gather_body,
        grid=(b, n_e // GATHER_EXPERTS),
        in_specs=[pl.BlockSpec((1, n_e, n_tok), lambda b_, g: (b_, 0, 0)),
                  pl.BlockSpec((1, n_tok, d), lambda b_, g: (b_, 0, 0))],
        out_specs=pl.BlockSpec((1, GATHER_EXPERTS, n_slots, d), lambda b_, g: (b_, g, 0, 0)),
        out_shape=jax.ShapeDtypeStruct((b, n_e, n_slots, d), BF16),
        compiler_params=_cparams(2),
        name="gather",
    )(pos, h)


def _experts_body(pos_ref, aff_ref, xe_ref, wg_ref, wu_ref, wd_ref, y_ref, wg_s, wu_s, wd_s):
    @pl.when(pl.program_id(1) == 0)
    def _():
        wg_s[...] = wg_ref[0, 0].astype(BF16)
        wu_s[...] = wu_ref[0, 0].astype(BF16)
        wd_s[...] = wd_ref[0, 0].astype(BF16)

    n_b, _, n_slots, d = xe_ref.shape
    e = pl.program_id(0)
    x = xe_ref[...].reshape(n_b * n_slots, d)
    gate = jnp.concatenate(
        [jnp.sum(jnp.where(_slot_hits(pos_ref[j, pl.ds(e, 1), :], n_slots),
                           aff_ref[j, pl.ds(e, 1), :], 0.0), axis=1, keepdims=True)
         for j in range(n_b)], axis=0)
    y = None
    for c0 in range(0, wg_s.shape[1], FFN_CHUNK):
        a = _dot(x, wg_s[:, c0:c0 + FFN_CHUNK])
        u = _dot(x, wu_s[:, c0:c0 + FFN_CHUNK])
        act = ((a * jax.nn.sigmoid(a)) * u).astype(BF16)
        t = _dot(act, wd_s[c0:c0 + FFN_CHUNK, :])
        y = t if y is None else y + t
    y_ref[...] = (y * gate).astype(BF16).reshape(n_b, 1, n_slots, d)


def _experts(pos, aff, xe, w_gate, w_up, w_down, layer):
    b, n_e, n_tok = pos.shape
    n_slots = xe.shape[2]
    d, f = w_gate.shape[2:]
    nb = FFN_BATCHES if b % FFN_BATCHES == 0 else 1
    row_spec = pl.BlockSpec((nb, n_e, n_tok), lambda e, g: (g, 0, 0))
    slot_spec = pl.BlockSpec((nb, 1, n_slots, d), lambda e, g: (g, e, 0, 0))
    return pl.pallas_call(
        _experts_body,
        grid=(n_e, b // nb),
        in_specs=[row_spec, row_spec, slot_spec,
                  pl.BlockSpec((1, 1, d, f), lambda e, g: (layer, e, 0, 0)),
                  pl.BlockSpec((1, 1, d, f), lambda e, g: (layer, e, 0, 0)),
                  pl.BlockSpec((1, 1, f, d), lambda e, g: (layer, e, 0, 0))],
        out_specs=slot_spec,
        out_shape=jax.ShapeDtypeStruct(xe.shape, BF16),
        scratch_shapes=[pltpu.VMEM((d, f), BF16), pltpu.VMEM((d, f), BF16), pltpu.VMEM((f, d), BF16)],
        compiler_params=_cparams(2),
        name="experts",
    )(pos, aff, xe, w_gate, w_up, w_down)


def _combine_body(pos_ref, y_ref, x_ref, gc_ref, gl_ref, g_ref, out_ref, *, row0):
    n_e, n_slots = y_ref.shape[1:3]
    pos = pos_ref[0]
    slot = lax.broadcasted_iota(jnp.int32, (n_slots, pos.shape[1]), 0)
    onehot = jnp.concatenate(
        [jnp.where(slot == pos[e:e + 1, :], 1.0, 0.0).astype(BF16) for e in range(n_e)], axis=0)
    acc = lax.dot_general(onehot, y_ref[0].reshape(n_e * n_slots, y_ref.shape[3]),
                          (((0,), (0,)), ((), ())), preferred_element_type=F32)
    ms = jnp.mean(acc * acc, axis=-1, keepdims=True)
    y = (acc * lax.rsqrt(ms + EPS)) * g_ref[...]
    out_ref[0] = x_ref[0] + _tile_mod(gc_ref, gl_ref, acc.shape[0], row0) * y


def _combine(pos, y, xs, mod, gain, row0):
    b, n_tok, d = xs.shape
    tm = _token_tile(n_tok)
    n_e, n_slots = y.shape[1:3]
    return pl.pallas_call(
        functools.partial(_combine_body, row0=row0),
        grid=(b, n_tok // tm),
        in_specs=[pl.BlockSpec((1, n_e, tm), lambda b_, i: (b_, 0, i)),
                  pl.BlockSpec((1, n_e, n_slots, d), lambda b_, i: (b_, 0, 0, 0)),
                  pl.BlockSpec((1, tm, d), lambda b_, i: (b_, i, 0))] + _mod_specs(5) + [
                  pl.BlockSpec(gain.shape, lambda b_, i: (0, 0))],
        out_specs=pl.BlockSpec((1, tm, d), lambda b_, i: (b_, i, 0)),
        out_shape=jax.ShapeDtypeStruct(xs.shape, F32),
        compiler_params=_cparams(2),
        name="combine",
    )(pos, y, xs, mod, mod, gain)


def kernel(x, c, ctx, c_ctx, w_ada, b_ada, g_pre_mix, g_post_mix, g_pre_ffn, g_post_ffn, w_in, w_out, q_gain_a, k_gain_a, lam_q1, lam_k1, lam_q2, lam_k2, subln_gain_b, sink_c, rpb_d, w_router, w_gate, w_up, w_down):
    b, n_lat, d = x.shape
    n_ctx = ctx.shape[1]
    depth = w_ada.shape[0]
    assert n_ctx == TOK_TILE and n_lat % ATTN_STEP_ROWS == 0 and d == D_MODEL and b + 1 <= 16

    stream = (ctx, x)
    cc = jnp.zeros((16, d), F32).at[:b].set(c).at[b].set(c_ctx)
    mod_all = _ada(cc, w_ada, b_ada)
    mod_all = jnp.stack([jnp.broadcast_to(mod_all[:, b:b + 1], (depth, b, 6 * d)), mod_all[:, :b]],
                        axis=2).reshape(depth, b, 2, 6, 1, d)
    ca, sa = _rope_lane_tables(n_ctx, n_lat, HEAD_DIM)
    cb, sb = _rope_lane_tables(n_ctx, n_lat, DIFF_DIM)
    w_perm_all = _permute_w_in(w_in)
    w_out_all = w_out.astype(BF16)
    w_router_t = jnp.swapaxes(w_router, 1, 2)
    nbr_bias = _neighbourhood_bias(rpb_d, n_ctx, n_lat // GRID_W)
    pair_gain = lambda g: jnp.stack([jnp.tile(g[:, 0::2], (1, 4)), jnp.tile(g[:, 1::2], (1, 4))], axis=1)
    gq_all, gk_all = pair_gain(q_gain_a), pair_gain(k_gain_a)
    row = lambda v: v.reshape(1, -1)

    for l in range(depth):
        with_ctx = l < depth - 1
        row0 = 0 if with_ctx else 1
        lam_init = 0.8 - 0.6 * math.exp(-0.3 * l)
        mod = mod_all[l]
        qkv = _proj(stream, mod, row(g_pre_mix[l]), w_perm_all[l], ca, sa, cb, sb, gq_all[l], gk_all[l])

        lam = (jnp.exp(jnp.sum(lam_q1[l] * lam_k1[l])) - jnp.exp(jnp.sum(lam_q2[l] * lam_k2[l]))
               + lam_init).reshape(1)
        o_mix = (
            _attention(qkv, 0, "global", with_ctx),
            _attention(qkv, 1, "diff", with_ctx, scalars=lam, lam_init=lam_init,
                       gain=row(jnp.tile(subln_gain_b[l], N_HEADS))),
            _attention(qkv, 2, "window", with_ctx, scalars=sink_c[l]),
            _attention(qkv, 3, "nbr", with_ctx, bias=nbr_bias, layer=l),
        )
        xs = _outproj(o_mix, w_out_all[l], stream, mod, row(g_post_mix[l]), row0)

        h, logits = _route(xs, mod, row(g_pre_ffn[l]), w_router_t[l], row0)
        cap_lat = CAPACITY_FACTOR * n_lat // N_EXPERTS
        cap_ctx = CAPACITY_FACTOR * n_ctx // N_EXPERTS
        if with_ctx:
            segments = ((0, n_ctx, cap_ctx, 0), (n_ctx, n_lat, cap_lat, cap_ctx))
            n_slots = cap_ctx + cap_lat
        else:
            segments = ((0, n_lat, cap_lat, 0),)
            n_slots = cap_lat
        pos, aff = _select(logits, segments)
        y = _experts(pos, aff, _gather(pos, h, n_slots), w_gate, w_up, w_down, l)
        stream = _combine(pos, y, xs, mod, row(g_post_ffn[l]), row0)
    return stream
```

```python
import functools
import math

import numpy as np
import jax
import jax.numpy as jnp
from jax import lax
from jax.experimental import pallas as pl
from jax.experimental.pallas import tpu as pltpu

F32 = jnp.float32
BF16 = jnp.bfloat16

D_MODEL = 1024
GRID_W = 64
HEAD_DIM = 64
DIFF_DIM = 32
N_HEADS = 4
MIXER_W = N_HEADS * HEAD_DIM
QKV_W = 4 * 3 * MIXER_W
WINDOW = 128
NA_ROWS = 8
NA_COLS = 16
ROPE_BASE = 10000.0
N_EXPERTS = 16
CAPACITY_FACTOR = 2
EPS = 1e-6
NEG_INF = -1e30
LOG2E = 1.4426950408889634
PROJ_WIDTHS = (256, 128, 128, 256, 256, 256, 256, 128, 128, 256, 256, 256)
LANES = 128
TOK_TILE = 256
VMEM_LIMIT = 56 * 1024 * 1024
ATTN_SCORE_ROWS = {"global": 512, "diff": 512, "window": 256, "nbr": 256}
ATTN_STEP_ROWS = {"global": 1024, "diff": 1024, "window": 1024, "nbr": 512}
NBR_TILE_ROWS = TOK_TILE // GRID_W
NBR_STRIP_ROWS = 12
GATHER_EXPERTS = 4
FFN_BATCHES = 4
FFN_CHUNK = 256
BISECT_MAX_STEPS = 192


def _cparams(n_axes):
    return pltpu.CompilerParams(dimension_semantics=("arbitrary",) * n_axes,
                                vmem_limit_bytes=VMEM_LIMIT)


def _proj_source_columns():
    offs = np.concatenate([[0], np.cumsum(PROJ_WIDTHS)])

    def pairs(base, n_slots, slot_src, n_pairs):
        x0 = [base + slot_src(j) + 2 * i for j in range(n_slots) for i in range(n_pairs)]
        return x0 + [c + 1 for c in x0]

    cols = []
    for m in range(4):
        q0, k0, v0 = offs[3 * m], offs[3 * m + 1], offs[3 * m + 2]
        if m in (0, 2):
            cols += pairs(q0, 4, lambda j: j * HEAD_DIM, 32)
            cols += pairs(k0, 4, lambda j: (j // 2) * HEAD_DIM, 32)
            cols += [v0 + (j // 2) * HEAD_DIM + d for j in range(4) for d in range(HEAD_DIM)]
        elif m == 1:
            cols += pairs(q0, 8, lambda j: j * DIFF_DIM, 16)
            cols += pairs(k0, 8, lambda j: j * DIFF_DIM, 16)
            cols += [v0 + c for c in range(MIXER_W)]
        else:
            cols += pairs(q0, 4, lambda j: j * HEAD_DIM, 32)
            cols += pairs(k0, 4, lambda j: j * HEAD_DIM, 32)
            cols += [v0 + c for c in range(MIXER_W)]
    return np.asarray(cols, dtype=np.int32)


def _permute_w_in(w_in):
    src = _proj_source_columns()
    pick = (jnp.arange(w_in.shape[-1], dtype=jnp.int32)[:, None] == src[None, :]).astype(BF16)
    return jnp.einsum('ldk,kn->ldn', w_in.astype(BF16), pick, preferred_element_type=BF16)


def _rope_tables(n_tokens, dim):
    t = jnp.arange(n_tokens, dtype=jnp.int32)
    rows = (t // GRID_W).astype(F32)
    cols = (t % GRID_W).astype(F32)
    n_axis = dim // 4
    inv_freq = ROPE_BASE ** (-jnp.arange(n_axis, dtype=F32) / n_axis)
    ang = jnp.concatenate([rows[:, None] * inv_freq, cols[:, None] * inv_freq], axis=-1)
    return jnp.cos(ang), jnp.sin(ang)


def _rope_lane_tables(n_ctx, n_lat, dim):
    cos, sin = _rope_tables(n_lat, dim)
    reps = LANES // cos.shape[1]
    cos = jnp.concatenate([jnp.ones((n_ctx, LANES), F32), jnp.tile(cos, (1, reps))], axis=0)
    sin = jnp.concatenate([jnp.zeros((n_ctx, LANES), F32), jnp.tile(sin, (1, reps))], axis=0)
    return cos, sin


def _nbr_strip_start(tile, n_rows):
    lo = tile * NBR_TILE_ROWS - NA_ROWS // 2
    return jnp.clip(lo, 0, n_rows - NBR_STRIP_ROWS) if isinstance(lo, jax.Array) else int(
        np.clip(lo, 0, n_rows - NBR_STRIP_ROWS))


def _nbr_pattern(tile, n_tiles):
    if isinstance(tile, jax.Array):
        return jnp.where(tile == 0, 0, jnp.where(tile == n_tiles - 1, 2, 1))
    return 0 if tile == 0 else (2 if tile == n_tiles - 1 else 1)


def _neighbourhood_bias(rpb, n_ctx, n_rows):
    n_tiles = n_rows // NBR_TILE_ROWS
    assert n_tiles >= 3 and n_rows >= NBR_STRIP_ROWS

    def rows_of(tile):
        ss = _nbr_strip_start(tile, n_rows)
        r = tile * NBR_TILE_ROWS + np.arange(NBR_TILE_ROWS)[:, None]
        rs = np.clip(r - NA_ROWS // 2, 0, n_rows - NA_ROWS)
        kr = ss + np.arange(NBR_STRIP_ROWS)[None, :]
        valid = (kr >= rs) & (kr < rs + NA_ROWS)
        return valid, np.where(valid, kr - r + NA_ROWS - 1, 0)

    reps = [rows_of(t) for t in (0, 1, n_tiles - 1)]
    for t in range(n_tiles):
        v, ri = rows_of(t)
        assert np.array_equal(v, reps[_nbr_pattern(t, n_tiles)][0])
        assert np.array_equal(ri, reps[_nbr_pattern(t, n_tiles)][1])
    row_valid = np.stack([v for v, _ in reps])
    ri = np.stack([r for _, r in reps])
    oh_r = (ri[..., None] == np.arange(2 * NA_ROWS - 1)) & row_valid[..., None]
    cq = np.arange(GRID_W)
    col_start = np.clip(cq - NA_COLS // 2, 0, GRID_W - NA_COLS)
    col_valid = (cq[None, :] >= col_start[:, None]) & (cq[None, :] < col_start[:, None] + NA_COLS)
    ci = np.clip(cq[None, :] - cq[:, None] + NA_COLS - 1, 0, 2 * NA_COLS - 2)
    oh_c = np.arange(2 * NA_COLS - 1)[:, None, None] == ci[None]
    bias = jnp.einsum('lhrc,pajr,cqw->lhpaqjw', rpb.astype(F32), oh_r.astype(np.float32),
                      oh_c.astype(np.float32), precision=lax.Precision.HIGHEST)
    valid = row_valid[None, None, :, :, None, :, None] & col_valid[None, None, None, None, :, None, :]
    bias = jnp.where(valid, bias * LOG2E, NEG_INF)
    bias = bias.reshape(bias.shape[:2] + (3, TOK_TILE, NBR_STRIP_ROWS * GRID_W))
    return jnp.concatenate([jnp.zeros(bias.shape[:4] + (n_ctx,), F32), bias], axis=-1)


def _block_diag_ones(n, group):
    r = lax.broadcasted_iota(jnp.int32, (n, n), 0) // group
    c = lax.broadcasted_iota(jnp.int32, (n, n), 1) // group
    return jnp.where(r == c, 1.0, 0.0).astype(BF16)


def _dot(a, b):
    return jnp.dot(a, b, preferred_element_type=F32)


def _dot_nt(a, b):
    return lax.dot_general(a, b, (((1,), (1,)), ((), ())), preferred_element_type=F32)


def _split(a):
    hi = a.astype(BF16)
    return hi, (a - hi.astype(F32)).astype(BF16)


def _dot_hilo(a, b_bf16):
    hi, lo = _split(a)
    return _dot(hi, b_bf16) + _dot(lo, b_bf16)


def _lane_mask(cond):
    return jnp.where(cond, 1.0, 0.0).astype(BF16)


def _norm_modulate(x, gain, shift, scale):
    ms = jnp.mean(x * x, axis=-1, keepdims=True)
    h = (x * lax.rsqrt(ms + EPS)) * gain
    return h * (1.0 + scale) + shift


def _token_tile(n_tok):
    return next(t for t in (768, 512, TOK_TILE) if n_tok % t == 0)


def _mod_specs(chunk):
    return [pl.BlockSpec((1, 1, 1, 1, D_MODEL), lambda b, i, r=r: (b, r, chunk, 0, 0)) for r in (0, 1)]


def _tile_mod(mc_ref, ml_ref, n_rows, row0):
    lat = ml_ref[0, 0, 0]
    if row0 > 0:
        return lat
    row = lax.broadcasted_iota(jnp.int32, (n_rows, 1), 0)
    is_ctx = jnp.logical_and(row < TOK_TILE, pl.program_id(1) == 0)
    return jnp.where(is_ctx, mc_ref[0, 0, 0], lat)


def _ada_body(c_ref, w_ref, b_ref, o_ref):
    c = c_ref[...]
    cs = (c * jax.nn.sigmoid(c)).astype(BF16)
    o_ref[0] = _dot(cs, w_ref[0].astype(BF16)) + b_ref[0]


def _ada(cc, w_ada, b_ada):
    depth, d, n = w_ada.shape
    tn = 512
    return pl.pallas_call(
        _ada_body,
        grid=(depth, n // tn),
        in_specs=[pl.BlockSpec(cc.shape, lambda l, j: (0, 0)),
                  pl.BlockSpec((1, d, tn), lambda l, j: (l, 0, j)),
                  pl.BlockSpec((1, 1, tn), lambda l, j: (l, 0, j))],
        out_specs=pl.BlockSpec((1, cc.shape[0], tn), lambda l, j: (l, 0, j)),
        out_shape=jax.ShapeDtypeStruct((depth, cc.shape[0], n), F32),
        compiler_params=_cparams(2),
        name="ada",
    )(cc, w_ada, b_ada.reshape(depth, 1, n))


def _stream_specs(stream, row0, tm):
    ctx_arr, lat_arr = stream if isinstance(stream, tuple) else (stream, stream)
    lat_tile0 = 0 if isinstance(stream, tuple) else 1
    k = tm // TOK_TILE
    blk = (1, TOK_TILE, lat_arr.shape[-1])
    specs = [pl.BlockSpec(blk, lambda b_, i, j=j: (b_, jnp.maximum(i * k + row0 + j - 1, 0) + lat_tile0, 0))
             for j in range(k)]
    args = [lat_arr] * k
    if row0 == 0:
        specs, args = [pl.BlockSpec(blk, lambda b_, i: (b_, 0, 0))] + specs, [ctx_arr] + args
    return specs, args


def _stream_tile(x_refs, row0):
    parts = [r[0] for r in x_refs]
    if row0 == 0:
        parts = [jnp.where(pl.program_id(1) == 0, parts[0], parts[1])] + parts[2:]
    return parts[0] if len(parts) == 1 else jnp.concatenate(parts, axis=0)


def _proj_body(*refs):
    (shc_ref, shl_ref, scc_ref, scl_ref, g_ref, w_ref, ca_ref, sa_ref, cb_ref, sb_ref, gq_ref, gk_ref,
     o_ref) = refs[-13:]
    x = _stream_tile(refs[:-13], 0)
    tm = x.shape[0]
    h = _norm_modulate(x, g_ref[...], _tile_mod(shc_ref, shl_ref, tm, 0), _tile_mod(scc_ref, scl_ref, tm, 0))
    acc = _dot(h.astype(BF16), w_ref[...])
    ca, sa, cb, sb = ca_ref[...], sa_ref[...], cb_ref[...], sb_ref[...]
    bd = _block_diag_ones(LANES, 32)

    def get(m, j):
        c0 = (3 * m) * MIXER_W + j * LANES
        return acc[:, c0:c0 + LANES]

    def put(m, j, v):
        c0 = (3 * m) * MIXER_W + j * LANES
        o_ref[0, :, c0:c0 + LANES] = v.astype(BF16)

    def rope(x0, x1, c, s):
        return x0 * c - x1 * s, x0 * s + x1 * c

    def head_norm(x0, x1, g_ref_):
        gs = _dot_hilo(x0 * x0 + x1 * x1, bd)
        r = lax.rsqrt(gs * (1.0 / HEAD_DIM) + EPS)
        return x0 * r * g_ref_[0:1, :], x1 * r * g_ref_[1:2, :]

    for m in range(4):
        q0, q1, k0, k1 = get(m, 0), get(m, 1), get(m, 2), get(m, 3)
        if m == 0:
            q0, q1 = head_norm(q0, q1, gq_ref)
            k0, k1 = head_norm(k0, k1, gk_ref)
        if m in (0, 2):
            q0, q1 = rope(q0, q1, ca, sa)
            k0, k1 = rope(k0, k1, ca, sa)
        elif m == 1:
            q0, q1 = rope(q0, q1, cb, sb)
            k0, k1 = rope(k0, k1, cb, sb)
        qscale = LOG2E * (DIFF_DIM if m == 1 else HEAD_DIM) ** -0.5
        put(m, 0, q0 * qscale)
        put(m, 1, q1 * qscale)
        put(m, 2, k0)
        put(m, 3, k1)
        put(m, 4, get(m, 4))
        put(m, 5, get(m, 5))


def _proj(stream, mod, gain, w_perm, ca, sa, cb, sb, gq, gk):
    lt = ca.shape[0]
    tm = _token_tile(lt)
    x_specs, x_args = _stream_specs(stream, 0, tm)
    b = x_args[0].shape[0]
    tab = pl.BlockSpec((tm, LANES), lambda b_, i: (i, 0))
    full = lambda a: pl.BlockSpec(a.shape, lambda b_, i: (0,) * a.ndim)
    return pl.pallas_call(
        _proj_body,
        grid=(b, lt // tm),
        in_specs=x_specs + _mod_specs(0) + _mod_specs(1) + [
            full(gain), full(w_perm), tab, tab, tab, tab, full(gq), full(gk)],
        out_specs=pl.BlockSpec((1, tm, QKV_W), lambda b_, i: (b_, i, 0)),
        out_shape=jax.ShapeDtypeStruct((b, lt, QKV_W), BF16),
        compiler_params=_cparams(2),
        name="proj",
    )(*x_args, mod, mod, mod, mod, gain, w_perm, ca, sa, cb, sb, gq, gk)


def _sum_lane(h):
    return (HEAD_DIM * (h + 1)) % MIXER_W


def _fill_head_values(v_ref, vm_ref):
    lane = lax.broadcasted_iota(jnp.int32, (1, MIXER_W), 1)
    v = v_ref[0]
    for h in range(N_HEADS):
        tap = jnp.where(lane == _sum_lane(h), 1.0, 0.0).astype(BF16)
        vm_ref[h] = jnp.where(lane // HEAD_DIM == h, v, tap)


def _attend(qm, kk, vm, bias=None, valid=None, extra_logit=None):
    s = _dot_nt(qm, kk)
    if bias is not None:
        s = s + bias
    if valid is not None:
        s = jnp.where(valid, s, NEG_INF)
    m = jnp.max(s, axis=-1, keepdims=True)
    if extra_logit is not None:
        m = jnp.maximum(m, extra_logit)
    return _dot(jnp.exp2(s - m).astype(BF16), vm), m


def _attn_body(*refs, kind, with_ctx, lam_init, n_ctx, n_lat):
    sub = ATTN_SCORE_ROWS[kind]
    n_sub = ATTN_STEP_ROWS[kind] // sub
    refs = list(refs)
    scalar_ref = refs.pop(0) if kind in ("diff", "window") else None
    q_ref, k_ref, v_ref = refs[:3]
    refs = refs[3:]
    gain_ref = refs.pop(0) if kind == "diff" else None
    bias_refs = [refs.pop(0) for _ in range(n_sub)] if kind == "nbr" else None
    o_ref, vm_ref = refs
    step = pl.program_id(1)
    lane = lax.broadcasted_iota(jnp.int32, (1, MIXER_W), 1)
    band = sub + 2 * WINDOW
    strip = NBR_STRIP_ROWS * GRID_W
    out0 = n_ctx if with_ctx else 0

    def rows_of(ref, ranges, *lead):
        parts = [ref[lead + (pl.ds(start, size), slice(None))] for start, size in ranges]
        return parts[0] if len(parts) == 1 else jnp.concatenate(parts, axis=0)

    def tile(q_row, n_rows, out_row, t=None, bias_ref=None):
        q = q_ref[0, pl.ds(q_row, n_rows), :]
        ranges, valid = [(0, n_ctx)], None
        if t is not None:
            if kind in ("global", "diff"):
                ranges = [(0, n_ctx + n_lat)]
            elif kind == "window":
                k_start = jnp.clip(t * sub - WINDOW, 0, n_lat - band)
                ranges.append((pl.multiple_of(n_ctx + k_start, LANES), band))
                col = lax.broadcasted_iota(jnp.int32, (1, n_ctx + band), 1)
                row = lax.broadcasted_iota(jnp.int32, (n_rows, 1), 0)
                dist = (k_start + col - n_ctx) - (t * sub + row)
                valid = jnp.abs(jnp.where(col < n_ctx, 0, dist)) <= WINDOW
            else:
                rs = _nbr_strip_start(t, n_lat // GRID_W)
                ranges.append((pl.multiple_of(n_ctx + rs * GRID_W, GRID_W), strip))
        kk = rows_of(k_ref, ranges, 0)
        out = jnp.zeros((n_rows, MIXER_W), F32)
        for h in range(N_HEADS):
            vm = rows_of(vm_ref, ranges, h)
            tap = _sum_lane(h)
            if kind == "diff":
                o = None
                for c in range(2):
                    qm = q * _lane_mask((lane % LANES) // (DIFF_DIM // 2) == 2 * h + c)
                    pv, _ = _attend(qm, kk, vm)
                    coef = 1.0 if c == 0 else scalar_ref[0]
                    term = pv * (coef / pv[:, tap:tap + 1])
                    o = term if c == 0 else o - term
            else:
                qm = q * _lane_mask((lane % LANES) // (HEAD_DIM // 2) == h)
                if kind == "window":
                    sink = scalar_ref[h] * LOG2E
                    pv, m = _attend(qm, kk, vm, valid=valid, extra_logit=sink)
                    o = pv * (1.0 / (pv[:, tap:tap + 1] + jnp.exp2(sink - m)))
                else:
                    bias = bias_ref[0, h, 0] if (kind == "nbr" and t is not None) else None
                    pv, _ = _attend(qm, kk, vm, bias=bias)
                    o = pv * (1.0 / pv[:, tap:tap + 1])
            out = jnp.where(lane // HEAD_DIM == h, o, out)
        if kind == "diff":
            gs = _dot_hilo(out * out, _block_diag_ones(MIXER_W, HEAD_DIM))
            out = out * lax.rsqrt(gs * (1.0 / HEAD_DIM) + EPS) * gain_ref[...] * (1.0 - lam_init)
        o_ref[0, pl.ds(out_row, n_rows), :] = out.astype(BF16)

    @pl.when(step == 0)
    def _():
        _fill_head_values(v_ref, vm_ref)
        if with_ctx:
            tile(0, n_ctx, 0)

    for j in range(n_sub):
        t = step * n_sub + j
        row = pl.multiple_of(t * sub, sub)
        tile(n_ctx + row, sub, out0 + row, t=t, bias_ref=bias_refs[j] if bias_refs else None)


def _attention(qkv, mixer, kind, with_ctx, *, scalars=None, gain=None, bias=None, layer=0, lam_init=0.0):
    b, lt, _ = qkv.shape
    n_lat = lt - TOK_TILE
    n_out = lt if with_ctx else n_lat
    col = lambda j: pl.BlockSpec((1, lt, MIXER_W), lambda b_, i: (b_, 0, 3 * mixer + j))
    in_specs, args = [col(0), col(1), col(2)], [qkv, qkv, qkv]
    if scalars is not None:
        in_specs, args = [pl.BlockSpec(memory_space=pltpu.SMEM)] + in_specs, [scalars] + args
    if gain is not None:
        in_specs.append(pl.BlockSpec(gain.shape, lambda b_, i: (0, 0)))
        args.append(gain)
    if bias is not None:
        n_sub = ATTN_STEP_ROWS[kind] // ATTN_SCORE_ROWS[kind]
        n_tiles = n_lat // TOK_TILE
        for j in range(n_sub):
            in_specs.append(pl.BlockSpec(
                (1, N_HEADS, 1) + bias.shape[3:],
                lambda b_, i, j=j: (layer, 0, _nbr_pattern(i * n_sub + j, n_tiles), 0, 0)))
            args.append(bias)
    return pl.pallas_call(
        functools.partial(_attn_body, kind=kind, with_ctx=with_ctx, lam_init=lam_init,
                          n_ctx=TOK_TILE, n_lat=n_lat),
        grid=(b, n_lat // ATTN_STEP_ROWS[kind]),
        in_specs=in_specs,
        out_specs=pl.BlockSpec((1, n_out, MIXER_W), lambda b_, i: (b_, 0, 0)),
        out_shape=jax.ShapeDtypeStruct((b, n_out, MIXER_W), BF16),
        scratch_shapes=[pltpu.VMEM((N_HEADS, lt, MIXER_W), BF16)],
        compiler_params=_cparams(2),
        name="attn_" + kind,
    )(*args)


def _outproj_body(oa_ref, ob_ref, oc_ref, od_ref, w_ref, gc_ref, gl_ref, g_ref, *rest, row0):
    x_refs, out_ref = rest[:-1], rest[-1]
    acc = None
    for m, o_ref in enumerate((oa_ref, ob_ref, oc_ref, od_ref)):
        t = _dot(o_ref[0], w_ref[m * MIXER_W:(m + 1) * MIXER_W, :])
        acc = t if acc is None else acc + t
    ms = jnp.mean(acc * acc, axis=-1, keepdims=True)
    y = (acc * lax.rsqrt(ms + EPS)) * g_ref[...]
    out_ref[0] = _stream_tile(x_refs, row0) + _tile_mod(gc_ref, gl_ref, acc.shape[0], row0) * y


def _outproj(o_mix, w_out, stream, mod, gain, row0):
    b, n_tok, _ = o_mix[0].shape
    tm = _token_tile(n_tok)
    x_specs, x_args = _stream_specs(stream, row0, tm)
    d = x_args[0].shape[-1]
    o_spec = pl.BlockSpec((1, tm, MIXER_W), lambda b_, i: (b_, i, 0))
    return pl.pallas_call(
        functools.partial(_outproj_body, row0=row0),
        grid=(b, n_tok // tm),
        in_specs=[o_spec, o_spec, o_spec, o_spec,
                  pl.BlockSpec(w_out.shape, lambda b_, i: (0, 0))] + _mod_specs(2) + [
                  pl.BlockSpec(gain.shape, lambda b_, i: (0, 0))] + x_specs,
        out_specs=pl.BlockSpec((1, tm, d), lambda b_, i: (b_, i, 0)),
        out_shape=jax.ShapeDtypeStruct((b, n_tok, d), F32),
        compiler_params=_cparams(2),
        name="outproj",
    )(*o_mix, w_out, mod, mod, gain, *x_args)


def _route_body(x_ref, shc_ref, shl_ref, scc_ref, scl_ref, g_ref, wr_ref, h_ref, lg_ref, *, row0):
    tm = x_ref.shape[1]
    h = _norm_modulate(x_ref[0], g_ref[...], _tile_mod(shc_ref, shl_ref, tm, row0),
                       _tile_mod(scc_ref, scl_ref, tm, row0))
    h_ref[0] = h.astype(BF16)
    w_hi, w_lo = _split(wr_ref[...])
    h_hi, h_lo = _split(h)
    lg_ref[0] = _dot_nt(w_hi, h_hi) + (_dot_nt(w_hi, h_lo) + _dot_nt(w_lo, h_hi))


def _route(xs, mod, gain, w_router_t, row0):
    b, n_tok, d = xs.shape
    tm = _token_tile(n_tok)
    n_e = w_router_t.shape[0]
    return pl.pallas_call(
        functools.partial(_route_body, row0=row0),
        grid=(b, n_tok // tm),
        in_specs=[pl.BlockSpec((1, tm, d), lambda b_, i: (b_, i, 0))] + _mod_specs(3) + _mod_specs(4) + [
                  pl.BlockSpec(gain.shape, lambda b_, i: (0, 0)),
                  pl.BlockSpec(w_router_t.shape, lambda b_, i: (0, 0))],
        out_specs=[pl.BlockSpec((1, tm, d), lambda b_, i: (b_, i, 0)),
                   pl.BlockSpec((1, n_e, tm), lambda b_, i: (b_, 0, i))],
        out_shape=[jax.ShapeDtypeStruct((b, n_tok, d), BF16),
                   jax.ShapeDtypeStruct((b, n_e, n_tok), F32)],
        compiler_params=_cparams(2),
        name="route",
    )(xs, mod, mod, mod, mod, gain, w_router_t)


def _exclusive_cumsum(x, tri):
    off = jnp.zeros((x.shape[0], 1), F32)
    outs = []
    for c in range(x.shape[1] // LANES):
        xc = x[:, c * LANES:(c + 1) * LANES]
        inc = _dot(xc.astype(BF16), tri)
        outs.append(inc - xc + off)
        off = off + inc[:, LANES - 1:LANES]
    return jnp.concatenate(outs, axis=1)


def _select_body(lg_ref, pos_ref, aff_ref, *, segments):
    r = lax.broadcasted_iota(jnp.int32, (LANES, LANES), 0)
    c = lax.broadcasted_iota(jnp.int32, (LANES, LANES), 1)
    tri = jnp.where(r <= c, 1.0, 0.0).astype(BF16)
    n_b, n_e = lg_ref.shape[:2]
    for t0, t, cap, slot0 in segments:
        affs = []
        for j in range(n_b):
            lg = lg_ref[j, :, t0:t0 + t]
            e = jnp.exp(lg - jnp.max(lg, axis=0, keepdims=True))
            affs.append(e / jnp.sum(e, axis=0, keepdims=True))
        aff = jnp.concatenate(affs, axis=0)

        def count_above(thr):
            return jnp.sum(jnp.where(aff > thr, 1.0, 0.0), axis=1, keepdims=True)

        def unsettled(carry):
            return jnp.logical_and(carry[2] > 0.0, carry[3] < BISECT_MAX_STEPS)

        def bisect(carry):
            lo, hi, _, step = carry
            mid = 0.5 * (lo + hi)
            cnt = count_above(mid)
            new_lo = jnp.where(cnt >= cap, mid, lo)
            new_hi = jnp.where(cnt <= cap, mid, hi)
            moving = jnp.logical_and(new_lo < new_hi, jnp.logical_and(mid > lo, mid < hi))
            return new_lo, new_hi, jnp.sum(jnp.where(moving, 1.0, 0.0)), step + 1

        lo0 = jnp.full((aff.shape[0], 1), -1.0, F32)
        hi0 = jnp.max(aff, axis=1, keepdims=True)
        lo, hi, _, _ = lax.while_loop(unsettled, bisect, (lo0, hi0, jnp.float32(1.0), jnp.int32(0)))
        gt = aff > hi
        eq = jnp.logical_and(aff > lo, aff <= hi)
        need = cap - count_above(hi)
        eq_rank = _exclusive_cumsum(jnp.where(eq, 1.0, 0.0), tri)
        sel = jnp.logical_or(gt, jnp.logical_and(eq, eq_rank < need))
        slot = _exclusive_cumsum(jnp.where(sel, 1.0, 0.0), tri) + slot0
        pos = jnp.where(sel, slot, -1.0).astype(jnp.int32)
        for j in range(n_b):
            pos_ref[j, :, t0:t0 + t] = pos[j * n_e:(j + 1) * n_e]
            aff_ref[j, :, t0:t0 + t] = affs[j]


def _select(logits, segments):
    b, n_e, lt = logits.shape
    spec = pl.BlockSpec((b, n_e, lt), lambda i: (0, 0, 0))
    return pl.pallas_call(
        functools.partial(_select_body, segments=segments),
        grid=(1,),
        in_specs=[spec],
        out_specs=[spec, spec],
        out_shape=[jax.ShapeDtypeStruct((b, n_e, lt), jnp.int32),
                   jax.ShapeDtypeStruct((b, n_e, lt), F32)],
        compiler_params=_cparams(1),
        name="select",
    )(logits)


def _slot_hits(pos_row, n_slots):
    return lax.broadcasted_iota(jnp.int32, (n_slots, pos_row.shape[1]), 0) == pos_row


def _gather_body(pos_ref, h_ref, xe_ref):
    n_grp, n_slots = xe_ref.shape[1:3]
    e0 = pl.program_id(1) * n_grp
    onehot = jnp.concatenate(
        [jnp.where(_slot_hits(pos_ref[0, pl.ds(e0 + j, 1), :], n_slots), 1.0, 0.0).astype(BF16)
         for j in range(n_grp)], axis=0)
    xe = _dot(onehot, h_ref[0]).astype(BF16)
    xe_ref[0] = xe.reshape(n_grp, n_slots, xe.shape[1])


def _gather(pos, h, n_slots):
    b, n_e, n_tok = pos.shape
    d = h.shape[2]
    return pl.pallas_call(
        _gather_body,
        grid=(b, n_e // GATHER_EXPERTS),
        in_specs=[pl.BlockSpec((1, n_e, n_tok), lambda b_, g: (b_, 0, 0)),
                  pl.BlockSpec((1, n_tok, d), lambda b_, g: (b_, 0, 0))],
        out_specs=pl.BlockSpec((1, GATHER_EXPERTS, n_slots, d), lambda b_, g: (b_, g, 0, 0)),
        out_shape=jax.ShapeDtypeStruct((b, n_e, n_slots, d), BF16),
        compiler_params=_cparams(2),
        name="gather",
    )(pos, h)


def _experts_body(pos_ref, aff_ref, xe_ref, wg_ref, wu_ref, wd_ref, y_ref, wg_s, wu_s, wd_s):
    @pl.when(pl.program_id(1) == 0)
    def _():
        wg_s[...] = wg_ref[0, 0].astype(BF16)
        wu_s[...] = wu_ref[0, 0].astype(BF16)
        wd_s[...] = wd_ref[0, 0].astype(BF16)

    n_b, _, n_slots, d = xe_ref.shape
    e = pl.program_id(0)
    x = xe_ref[...].reshape(n_b * n_slots, d)
    gate = jnp.concatenate(
        [jnp.sum(jnp.where(_slot_hits(pos_ref[j, pl.ds(e, 1), :], n_slots),
                           aff_ref[j, pl.ds(e, 1), :], 0.0), axis=1, keepdims=True)
         for j in range(n_b)], axis=0)
    y = None
    for c0 in range(0, wg_s.shape[1], FFN_CHUNK):
        a = _dot(x, wg_s[:, c0:c0 + FFN_CHUNK])
        u = _dot(x, wu_s[:, c0:c0 + FFN_CHUNK])
        act = ((a * jax.nn.sigmoid(a)) * u).astype(BF16)
        t = _dot(act, wd_s[c0:c0 + FFN_CHUNK, :])
        y = t if y is None else y + t
    y_ref[...] = (y * gate).astype(BF16).reshape(n_b, 1, n_slots, d)


def _experts(pos, aff, xe, w_gate, w_up, w_down, layer):
    b, n_e, n_tok = pos.shape
    n_slots = xe.shape[2]
    d, f = w_gate.shape[2:]
    nb = FFN_BATCHES if b % FFN_BATCHES == 0 else 1
    row_spec = pl.BlockSpec((nb, n_e, n_tok), lambda e, g: (g, 0, 0))
    slot_spec = pl.BlockSpec((nb, 1, n_slots, d), lambda e, g: (g, e, 0, 0))
    return pl.pallas_call(
        _experts_body,
        grid=(n_e, b // nb),
        in_specs=[row_spec, row_spec, slot_spec,
                  pl.BlockSpec((1, 1, d, f), lambda e, g: (layer, e, 0, 0)),
                  pl.BlockSpec((1, 1, d, f), lambda e, g: (layer, e, 0, 0)),
                  pl.BlockSpec((1, 1, f, d), lambda e, g: (layer, e, 0, 0))],
        out_specs=slot_spec,
        out_shape=jax.ShapeDtypeStruct(xe.shape, BF16),
        scratch_shapes=[pltpu.VMEM((d, f), BF16), pltpu.VMEM((d, f), BF16), pltpu.VMEM((f, d), BF16)],
        compiler_params=_cparams(2),
        name="experts",
    )(pos, aff, xe, w_gate, w_up, w_down)


def _combine_body(pos_ref, y_ref, x_ref, gc_ref, gl_ref, g_ref, out_ref, *, row0):
    n_e, n_slots = y_ref.shape[1:3]
    pos = pos_ref[0]
    slot = lax.broadcasted_iota(jnp.int32, (n_slots, pos.shape[1]), 0)
    onehot = jnp.concatenate(
        [jnp.where(slot == pos[e:e + 1, :], 1.0, 0.0).astype(BF16) for e in range(n_e)], axis=0)
    acc = lax.dot_general(onehot, y_ref[0].reshape(n_e * n_slots, y_ref.shape[3]),
                          (((0,), (0,)), ((), ())), preferred_element_type=F32)
    ms = jnp.mean(acc * acc, axis=-1, keepdims=True)
    y = (acc * lax.rsqrt(ms + EPS)) * g_ref[...]
    out_ref[0] = x_ref[0] + _tile_mod(gc_ref, gl_ref, acc.shape[0], row0) * y


def _combine(pos, y, xs, mod, gain, row0):
    b, n_tok, d = xs.shape
    tm = _token_tile(n_tok)
    n_e, n_slots = y.shape[1:3]
    return pl.pallas_call(
        functools.partial(_combine_body, row0=row0),
        grid=(b, n_tok // tm),
        in_specs=[pl.BlockSpec((1, n_e, tm), lambda b_, i: (b_, 0, i)),
                  pl.BlockSpec((1, n_e, n_slots, d), lambda b_, i: (b_, 0, 0, 0)),
                  pl.BlockSpec((1, tm, d), lambda b_, i: (b_, i, 0))] + _mod_specs(5) + [
                  pl.BlockSpec(gain.shape, lambda b_, i: (0, 0))],
        out_specs=pl.BlockSpec((1, tm, d), lambda b_, i: (b_, i, 0)),
        out_shape=jax.ShapeDtypeStruct(xs.shape, F32),
        compiler_params=_cparams(2),
        name="combine",
    )(pos, y, xs, mod, mod, gain)


def kernel(x, c, ctx, c_ctx, w_ada, b_ada, g_pre_mix, g_post_mix, g_pre_ffn, g_post_ffn, w_in, w_out, q_gain_a, k_gain_a, lam_q1, lam_k1, lam_q2, lam_k2, subln_gain_b, sink_c, rpb_d, w_router, w_gate, w_up, w_down):
    b, n_lat, d = x.shape
    n_ctx = ctx.shape[1]
    depth = w_ada.shape[0]
    assert n_ctx == TOK_TILE and n_lat % max(ATTN_STEP_ROWS.values()) == 0 and d == D_MODEL and b + 1 <= 16

    stream = (ctx, x)
    cc = jnp.zeros((16, d), F32).at[:b].set(c).at[b].set(c_ctx)
    mod_all = _ada(cc, w_ada, b_ada)
    mod_all = jnp.stack([jnp.broadcast_to(mod_all[:, b:b + 1], (depth, b, 6 * d)), mod_all[:, :b]],
                        axis=2).reshape(depth, b, 2, 6, 1, d)
    ca, sa = _rope_lane_tables(n_ctx, n_lat, HEAD_DIM)
    cb, sb = _rope_lane_tables(n_ctx, n_lat, DIFF_DIM)
    w_perm_all = _permute_w_in(w_in)
    w_out_all = w_out.astype(BF16)
    w_router_t = jnp.swapaxes(w_router, 1, 2)
    nbr_bias = _neighbourhood_bias(rpb_d, n_ctx, n_lat // GRID_W)
    pair_gain = lambda g: jnp.stack([jnp.tile(g[:, 0::2], (1, 4)), jnp.tile(g[:, 1::2], (1, 4))], axis=1)
    gq_all, gk_all = pair_gain(q_gain_a), pair_gain(k_gain_a)
    row = lambda v: v.reshape(1, -1)

    for l in range(depth):
        with_ctx = l < depth - 1
        row0 = 0 if with_ctx else 1
        lam_init = 0.8 - 0.6 * math.exp(-0.3 * l)
        mod = mod_all[l]
        qkv = _proj(stream, mod, row(g_pre_mix[l]), w_perm_all[l], ca, sa, cb, sb, gq_all[l], gk_all[l])

        lam = (jnp.exp(jnp.sum(lam_q1[l] * lam_k1[l])) - jnp.exp(jnp.sum(lam_q2[l] * lam_k2[l]))
               + lam_init).reshape(1)
        o_mix = (
            _attention(qkv, 0, "global", with_ctx),
            _attention(qkv, 1, "diff", with_ctx, scalars=lam, lam_init=lam_init,
                       gain=row(jnp.tile(subln_gain_b[l], N_HEADS))),
            _attention(qkv, 2, "window", with_ctx, scalars=sink_c[l]),
            _attention(qkv, 3, "nbr", with_ctx, bias=nbr_bias, layer=l),
        )
        xs = _outproj(o_mix, w_out_all[l], stream, mod, row(g_post_mix[l]), row0)

        h, logits = _route(xs, mod, row(g_pre_ffn[l]), w_router_t[l], row0)
        cap_lat = CAPACITY_FACTOR * n_lat // N_EXPERTS
        cap_ctx = CAPACITY_FACTOR * n_ctx // N_EXPERTS
        if with_ctx:
            segments = ((0, n_ctx, cap_ctx, 0), (n_ctx, n_lat, cap_lat, cap_ctx))
            n_slots = cap_ctx + cap_lat
        else:
            segments = ((0, n_lat, cap_lat, 0),)
            n_slots = cap_lat
        pos, aff = _select(logits, segments)
        y = _experts(pos, aff, _gather(pos, h, n_slots), w_gate, w_up, w_down, l)
        stream = _combine(pos, y, xs, mod, row(g_post_ffn[l]), row0)
    return stream
```

```python
import functools
import math

import numpy as np
import jax
import jax.numpy as jnp
from jax import lax
from jax.experimental import pallas as pl
from jax.experimental.pallas import tpu as pltpu

F32 = jnp.float32
BF16 = jnp.bfloat16

D_MODEL = 1024
GRID_W = 64
HEAD_DIM = 64
DIFF_DIM = 32
N_HEADS = 4
MIXER_W = N_HEADS * HEAD_DIM
QKV_W = 4 * 3 * MIXER_W
WINDOW = 128
NA_ROWS = 8
NA_COLS = 16
ROPE_BASE = 10000.0
N_EXPERTS = 16
CAPACITY_FACTOR = 2
EPS = 1e-6
NEG_INF = -1e30
LOG2E = 1.4426950408889634
PROJ_WIDTHS = (256, 128, 128, 256, 256, 256, 256, 128, 128, 256, 256, 256)
LANES = 128
TOK_TILE = 256
VMEM_LIMIT = 56 * 1024 * 1024
ATTN_SCORE_ROWS = {"global": 512, "diff": 512, "window": 256, "nbr": 256}
ATTN_STEP_ROWS = {"global": 1024, "diff": 1024, "window": 1024, "nbr": 1024}
NBR_TILE_ROWS = TOK_TILE // GRID_W
NBR_STRIP_ROWS = 12
GATHER_EXPERTS = 4
FFN_BATCHES = 4
FFN_CHUNK = 256
BISECT_MAX_STEPS = 192


def _cparams(n_axes):
    return pltpu.CompilerParams(dimension_semantics=("arbitrary",) * n_axes,
                                vmem_limit_bytes=VMEM_LIMIT)


def _proj_source_columns():
    offs = np.concatenate([[0], np.cumsum(PROJ_WIDTHS)])

    def pairs(base, n_slots, slot_src, n_pairs):
        x0 = [base + slot_src(j) + 2 * i for j in range(n_slots) for i in range(n_pairs)]
        return x0 + [c + 1 for c in x0]

    cols = []
    for m in range(4):
        q0, k0, v0 = offs[3 * m], offs[3 * m + 1], offs[3 * m + 2]
        if m in (0, 2):
            cols += pairs(q0, 4, lambda j: j * HEAD_DIM, 32)
            cols += pairs(k0, 4, lambda j: (j // 2) * HEAD_DIM, 32)
            cols += [v0 + (j // 2) * HEAD_DIM + d for j in range(4) for d in range(HEAD_DIM)]
        elif m == 1:
            cols += pairs(q0, 8, lambda j: j * DIFF_DIM, 16)
            cols += pairs(k0, 8, lambda j: j * DIFF_DIM, 16)
            cols += [v0 + c for c in range(MIXER_W)]
        else:
            cols += pairs(q0, 4, lambda j: j * HEAD_DIM, 32)
            cols += pairs(k0, 4, lambda j: j * HEAD_DIM, 32)
            cols += [v0 + c for c in range(MIXER_W)]
    return np.asarray(cols, dtype=np.int32)


def _permute_w_in(w_in):
    src = _proj_source_columns()
    pick = (jnp.arange(w_in.shape[-1], dtype=jnp.int32)[:, None] == src[None, :]).astype(BF16)
    return jnp.einsum('ldk,kn->ldn', w_in.astype(BF16), pick, preferred_element_type=BF16)


def _rope_tables(n_tokens, dim):
    t = jnp.arange(n_tokens, dtype=jnp.int32)
    rows = (t // GRID_W).astype(F32)
    cols = (t % GRID_W).astype(F32)
    n_axis = dim // 4
    inv_freq = ROPE_BASE ** (-jnp.arange(n_axis, dtype=F32) / n_axis)
    ang = jnp.concatenate([rows[:, None] * inv_freq, cols[:, None] * inv_freq], axis=-1)
    return jnp.cos(ang), jnp.sin(ang)


def _rope_lane_tables(n_ctx, n_lat, dim):
    cos, sin = _rope_tables(n_lat, dim)
    reps = LANES // cos.shape[1]
    cos = jnp.concatenate([jnp.ones((n_ctx, LANES), F32), jnp.tile(cos, (1, reps))], axis=0)
    sin = jnp.concatenate([jnp.zeros((n_ctx, LANES), F32), jnp.tile(sin, (1, reps))], axis=0)
    return cos, sin


def _nbr_strip_start(tile, n_rows):
    lo = tile * NBR_TILE_ROWS - NA_ROWS // 2
    return jnp.clip(lo, 0, n_rows - NBR_STRIP_ROWS) if isinstance(lo, jax.Array) else int(
        np.clip(lo, 0, n_rows - NBR_STRIP_ROWS))


def _nbr_pattern(tile, n_tiles):
    if isinstance(tile, jax.Array):
        return jnp.where(tile == 0, 0, jnp.where(tile == n_tiles - 1, 2, 1))
    return 0 if tile == 0 else (2 if tile == n_tiles - 1 else 1)


def _neighbourhood_bias(rpb, n_rows):
    n_tiles = n_rows // NBR_TILE_ROWS
    assert n_tiles >= 3 and n_rows >= NBR_STRIP_ROWS

    def rows_of(tile):
        ss = _nbr_strip_start(tile, n_rows)
        r = tile * NBR_TILE_ROWS + np.arange(NBR_TILE_ROWS)[:, None]
        rs = np.clip(r - NA_ROWS // 2, 0, n_rows - NA_ROWS)
        kr = ss + np.arange(NBR_STRIP_ROWS)[None, :]
        valid = (kr >= rs) & (kr < rs + NA_ROWS)
        return valid, np.where(valid, kr - r + NA_ROWS - 1, 0)

    reps = [rows_of(t) for t in (0, 1, n_tiles - 1)]
    for t in range(n_tiles):
        v, ri = rows_of(t)
        assert np.array_equal(v, reps[_nbr_pattern(t, n_tiles)][0])
        assert np.array_equal(ri, reps[_nbr_pattern(t, n_tiles)][1])
    row_valid = np.stack([v for v, _ in reps])
    ri = np.stack([r for _, r in reps])
    oh_r = (ri[..., None] == np.arange(2 * NA_ROWS - 1)) & row_valid[..., None]
    cq = np.arange(GRID_W)
    col_start = np.clip(cq - NA_COLS // 2, 0, GRID_W - NA_COLS)
    col_valid = (cq[None, :] >= col_start[:, None]) & (cq[None, :] < col_start[:, None] + NA_COLS)
    ci = np.clip(cq[None, :] - cq[:, None] + NA_COLS - 1, 0, 2 * NA_COLS - 2)
    oh_c = np.arange(2 * NA_COLS - 1)[:, None, None] == ci[None]
    bias = jnp.einsum('lhrc,pajr,cqw->lhpaqjw', rpb.astype(F32), oh_r.astype(np.float32),
                      oh_c.astype(np.float32), precision=lax.Precision.HIGHEST)
    valid = row_valid[None, None, :, :, None, :, None] & col_valid[None, None, None, None, :, None, :]
    bias = jnp.where(valid, bias * LOG2E, NEG_INF)
    return bias.reshape(bias.shape[:2] + (3, TOK_TILE, NBR_STRIP_ROWS * GRID_W))


def _block_diag_ones(n, group):
    r = lax.broadcasted_iota(jnp.int32, (n, n), 0) // group
    c = lax.broadcasted_iota(jnp.int32, (n, n), 1) // group
    return jnp.where(r == c, 1.0, 0.0).astype(BF16)


def _dot(a, b):
    return jnp.dot(a, b, preferred_element_type=F32)


def _dot_nt(a, b):
    return lax.dot_general(a, b, (((1,), (1,)), ((), ())), preferred_element_type=F32)


def _split(a):
    hi = a.astype(BF16)
    return hi, (a - hi.astype(F32)).astype(BF16)


def _dot_hilo(a, b_bf16):
    hi, lo = _split(a)
    return _dot(hi, b_bf16) + _dot(lo, b_bf16)


def _lane_mask(cond):
    return jnp.where(cond, 1.0, 0.0).astype(BF16)


def _norm_modulate(x, gain, shift, scale):
    ms = jnp.mean(x * x, axis=-1, keepdims=True)
    h = (x * lax.rsqrt(ms + EPS)) * gain
    return h * (1.0 + scale) + shift


def _token_tile(n_tok):
    return next(t for t in (768, 512, TOK_TILE) if n_tok % t == 0)


def _mod_specs(chunk):
    return [pl.BlockSpec((1, 1, 1, 1, D_MODEL), lambda b, i, r=r: (b, r, chunk, 0, 0)) for r in (0, 1)]


def _tile_mod(mc_ref, ml_ref, n_rows, row0):
    lat = ml_ref[0, 0, 0]
    if row0 > 0:
        return lat
    row = lax.broadcasted_iota(jnp.int32, (n_rows, 1), 0)
    is_ctx = jnp.logical_and(row < TOK_TILE, pl.program_id(1) == 0)
    return jnp.where(is_ctx, mc_ref[0, 0, 0], lat)


def _ada_body(c_ref, w_ref, b_ref, o_ref):
    c = c_ref[...]
    cs = (c * jax.nn.sigmoid(c)).astype(BF16)
    o_ref[0] = _dot(cs, w_ref[0].astype(BF16)) + b_ref[0]


def _ada(cc, w_ada, b_ada):
    depth, d, n = w_ada.shape
    tn = 512
    return pl.pallas_call(
        _ada_body,
        grid=(depth, n // tn),
        in_specs=[pl.BlockSpec(cc.shape, lambda l, j: (0, 0)),
                  pl.BlockSpec((1, d, tn), lambda l, j: (l, 0, j)),
                  pl.BlockSpec((1, 1, tn), lambda l, j: (l, 0, j))],
        out_specs=pl.BlockSpec((1, cc.shape[0], tn), lambda l, j: (l, 0, j)),
        out_shape=jax.ShapeDtypeStruct((depth, cc.shape[0], n), F32),
        compiler_params=_cparams(2),
        name="ada",
    )(cc, w_ada, b_ada.reshape(depth, 1, n))


def _stream_specs(stream, row0, tm):
    ctx_arr, lat_arr = stream if isinstance(stream, tuple) else (stream, stream)
    lat_tile0 = 0 if isinstance(stream, tuple) else 1
    k = tm // TOK_TILE
    blk = (1, TOK_TILE, lat_arr.shape[-1])
    specs = [pl.BlockSpec(blk, lambda b_, i, j=j: (b_, jnp.maximum(i * k + row0 + j - 1, 0) + lat_tile0, 0))
             for j in range(k)]
    args = [lat_arr] * k
    if row0 == 0:
        specs, args = [pl.BlockSpec(blk, lambda b_, i: (b_, 0, 0))] + specs, [ctx_arr] + args
    return specs, args


def _stream_tile(x_refs, row0):
    parts = [r[0] for r in x_refs]
    if row0 == 0:
        parts = [jnp.where(pl.program_id(1) == 0, parts[0], parts[1])] + parts[2:]
    return parts[0] if len(parts) == 1 else jnp.concatenate(parts, axis=0)


def _proj_body(*refs):
    (shc_ref, shl_ref, scc_ref, scl_ref, g_ref, w_ref, ca_ref, sa_ref, cb_ref, sb_ref, gq_ref, gk_ref,
     o_ref) = refs[-13:]
    x = _stream_tile(refs[:-13], 0)
    tm = x.shape[0]
    h = _norm_modulate(x, g_ref[...], _tile_mod(shc_ref, shl_ref, tm, 0), _tile_mod(scc_ref, scl_ref, tm, 0))
    hb = h.astype(BF16)
    accs = [_dot(hb, w_ref[:, 3 * m * MIXER_W:3 * (m + 1) * MIXER_W]) for m in range(4)]
    ca, sa, cb, sb = ca_ref[...], sa_ref[...], cb_ref[...], sb_ref[...]
    bd = _block_diag_ones(LANES, 32)

    def get(m, j):
        return accs[m][:, j * LANES:(j + 1) * LANES]

    def put(m, j, v):
        c0 = (3 * m) * MIXER_W + j * LANES
        o_ref[0, :, c0:c0 + LANES] = v.astype(BF16)

    def rope(x0, x1, c, s):
        return x0 * c - x1 * s, x0 * s + x1 * c

    def head_norm(x0, x1, g_ref_):
        gs = _dot_hilo(x0 * x0 + x1 * x1, bd)
        r = lax.rsqrt(gs * (1.0 / HEAD_DIM) + EPS)
        return x0 * r * g_ref_[0:1, :], x1 * r * g_ref_[1:2, :]

    for m in range(4):
        q0, q1, k0, k1 = get(m, 0), get(m, 1), get(m, 2), get(m, 3)
        if m == 0:
            q0, q1 = head_norm(q0, q1, gq_ref)
            k0, k1 = head_norm(k0, k1, gk_ref)
        if m in (0, 2):
            q0, q1 = rope(q0, q1, ca, sa)
            k0, k1 = rope(k0, k1, ca, sa)
        elif m == 1:
            q0, q1 = rope(q0, q1, cb, sb)
            k0, k1 = rope(k0, k1, cb, sb)
        qscale = LOG2E * (DIFF_DIM if m == 1 else HEAD_DIM) ** -0.5
        put(m, 0, q0 * qscale)
        put(m, 1, q1 * qscale)
        put(m, 2, k0)
        put(m, 3, k1)
        put(m, 4, get(m, 4))
        put(m, 5, get(m, 5))


def _proj(stream, mod, gain, w_perm, ca, sa, cb, sb, gq, gk):
    lt = ca.shape[0]
    tm = _token_tile(lt)
    x_specs, x_args = _stream_specs(stream, 0, tm)
    b = x_args[0].shape[0]
    tab = pl.BlockSpec((tm, LANES), lambda b_, i: (i, 0))
    full = lambda a: pl.BlockSpec(a.shape, lambda b_, i: (0,) * a.ndim)
    return pl.pallas_call(
        _proj_body,
        grid=(b, lt // tm),
        in_specs=x_specs + _mod_specs(0) + _mod_specs(1) + [
            full(gain), full(w_perm), tab, tab, tab, tab, full(gq), full(gk)],
        out_specs=pl.BlockSpec((1, tm, QKV_W), lambda b_, i: (b_, i, 0)),
        out_shape=jax.ShapeDtypeStruct((b, lt, QKV_W), BF16),
        compiler_params=_cparams(2),
        name="proj",
    )(*x_args, mod, mod, mod, mod, gain, w_perm, ca, sa, cb, sb, gq, gk)


def _sum_lane(h):
    return (HEAD_DIM * (h + 1)) % MIXER_W


def _fill_head_values(v_ref, vm_ref):
    lane = lax.broadcasted_iota(jnp.int32, (1, MIXER_W), 1)
    v = v_ref[0]
    for h in range(N_HEADS):
        tap = jnp.where(lane == _sum_lane(h), 1.0, 0.0).astype(BF16)
        vm_ref[h] = jnp.where(lane // HEAD_DIM == h, v, tap)


def _attend(qm, kk, vm, bias=None, valid=None, extra_logit=None):
    s = _dot_nt(qm, kk)
    if bias is not None:
        n0 = s.shape[1] - bias.shape[1]
        s = jnp.concatenate([s[:, :n0], s[:, n0:] + bias], axis=1)
    if valid is not None:
        s = jnp.where(valid, s, NEG_INF)
    m = jnp.max(s, axis=-1, keepdims=True)
    if extra_logit is not None:
        m = jnp.maximum(m, extra_logit)
    return _dot(jnp.exp2(s - m).astype(BF16), vm), m


def _attn_body(*refs, kind, with_ctx, lam_init, n_ctx, n_lat):
    sub = ATTN_SCORE_ROWS[kind]
    n_sub = ATTN_STEP_ROWS[kind] // sub
    refs = list(refs)
    scalar_ref = refs.pop(0) if kind in ("diff", "window") else None
    q_ref, k_ref, v_ref = refs[:3]
    refs = refs[3:]
    gain_ref = refs.pop(0) if kind == "diff" else None
    bias_refs = [refs.pop(0) for _ in range(n_sub)] if kind == "nbr" else None
    o_ref, vm_ref = refs
    step = pl.program_id(1)
    lane = lax.broadcasted_iota(jnp.int32, (1, MIXER_W), 1)
    band = sub + 2 * WINDOW
    strip = NBR_STRIP_ROWS * GRID_W
    out0 = n_ctx if with_ctx else 0

    def rows_of(ref, ranges, *lead):
        parts = [ref[lead + (pl.ds(start, size), slice(None))] for start, size in ranges]
        return parts[0] if len(parts) == 1 else jnp.concatenate(parts, axis=0)

    def tile(q_row, n_rows, out_row, t=None, bias_ref=None):
        q = q_ref[0, pl.ds(q_row, n_rows), :]
        ranges, valid = [(0, n_ctx)], None
        if t is not None:
            if kind in ("global", "diff"):
                ranges = [(0, n_ctx + n_lat)]
            elif kind == "window":
                k_start = jnp.clip(t * sub - WINDOW, 0, n_lat - band)
                ranges.append((pl.multiple_of(n_ctx + k_start, LANES), band))
                col = lax.broadcasted_iota(jnp.int32, (1, n_ctx + band), 1)
                row = lax.broadcasted_iota(jnp.int32, (n_rows, 1), 0)
                dist = (k_start + col - n_ctx) - (t * sub + row)
                valid = jnp.abs(jnp.where(col < n_ctx, 0, dist)) <= WINDOW
            else:
                rs = _nbr_strip_start(t, n_lat // GRID_W)
                ranges.append((pl.multiple_of(n_ctx + rs * GRID_W, GRID_W), strip))
        kk = rows_of(k_ref, ranges, 0)
        out = jnp.zeros((n_rows, MIXER_W), F32)
        for h in range(N_HEADS):
            vm = rows_of(vm_ref, ranges, h)
            tap = _sum_lane(h)
            if kind == "diff":
                o = None
                for c in range(2):
                    qm = q * _lane_mask((lane % LANES) // (DIFF_DIM // 2) == 2 * h + c)
                    pv, _ = _attend(qm, kk, vm)
                    coef = 1.0 if c == 0 else scalar_ref[0]
                    term = pv * (coef / pv[:, tap:tap + 1])
                    o = term if c == 0 else o - term
            else:
                qm = q * _lane_mask((lane % LANES) // (HEAD_DIM // 2) == h)
                if kind == "window":
                    sink = scalar_ref[h] * LOG2E
                    pv, m = _attend(qm, kk, vm, valid=valid, extra_logit=sink)
                    o = pv * (1.0 / (pv[:, tap:tap + 1] + jnp.exp2(sink - m)))
                else:
                    bias = bias_ref[0, h, 0] if (kind == "nbr" and t is not None) else None
                    pv, _ = _attend(qm, kk, vm, bias=bias)
                    o = pv * (1.0 / pv[:, tap:tap + 1])
            out = jnp.where(lane // HEAD_DIM == h, o, out)
        if kind == "diff":
            gs = _dot_hilo(out * out, _block_diag_ones(MIXER_W, HEAD_DIM))
            out = out * lax.rsqrt(gs * (1.0 / HEAD_DIM) + EPS) * gain_ref[...] * (1.0 - lam_init)
        o_ref[0, pl.ds(out_row, n_rows), :] = out.astype(BF16)

    @pl.when(step == 0)
    def _():
        _fill_head_values(v_ref, vm_ref)
        if with_ctx:
            tile(0, n_ctx, 0)

    for j in range(n_sub):
        t = step * n_sub + j
        row = pl.multiple_of(t * sub, sub)
        tile(n_ctx + row, sub, out0 + row, t=t, bias_ref=bias_refs[j] if bias_refs else None)


def _attention(qkv, mixer, kind, with_ctx, *, scalars=None, gain=None, bias=None, layer=0, lam_init=0.0):
    b, lt, _ = qkv.shape
    n_lat = lt - TOK_TILE
    n_out = lt if with_ctx else n_lat
    col = lambda j: pl.BlockSpec((1, lt, MIXER_W), lambda b_, i: (b_, 0, 3 * mixer + j))
    in_specs, args = [col(0), col(1), col(2)], [qkv, qkv, qkv]
    if scalars is not None:
        in_specs, args = [pl.BlockSpec(memory_space=pltpu.SMEM)] + in_specs, [scalars] + args
    if gain is not None:
        in_specs.append(pl.BlockSpec(gain.shape, lambda b_, i: (0, 0)))
        args.append(gain)
    if bias is not None:
        n_sub = ATTN_STEP_ROWS[kind] // ATTN_SCORE_ROWS[kind]
        n_tiles = n_lat // TOK_TILE
        for j in range(n_sub):
            in_specs.append(pl.BlockSpec(
                (1, N_HEADS, 1) + bias.shape[3:],
                lambda b_, i, j=j: (layer, 0, _nbr_pattern(i * n_sub + j, n_tiles), 0, 0)))
            args.append(bias)
    return pl.pallas_call(
        functools.partial(_attn_body, kind=kind, with_ctx=with_ctx, lam_init=lam_init,
                          n_ctx=TOK_TILE, n_lat=n_lat),
        grid=(b, n_lat // ATTN_STEP_ROWS[kind]),
        in_specs=in_specs,
        out_specs=pl.BlockSpec((1, n_out, MIXER_W), lambda b_, i: (b_, 0, 0)),
        out_shape=jax.ShapeDtypeStruct((b, n_out, MIXER_W), BF16),
        scratch_shapes=[pltpu.VMEM((N_HEADS, lt, MIXER_W), BF16)],
        compiler_params=_cparams(2),
        name="attn_" + kind,
    )(*args)


def _outproj_body(oa_ref, ob_ref, oc_ref, od_ref, w_ref, gc_ref, gl_ref, g_ref,
                  shc_ref, shl_ref, scc_ref, scl_ref, g2_ref, wr_ref, *rest, row0):
    x_refs, (out_ref, h_ref, lg_ref) = rest[:-3], rest[-3:]
    acc = None
    for m, o_ref in enumerate((oa_ref, ob_ref, oc_ref, od_ref)):
        t = _dot(o_ref[0], w_ref[m * MIXER_W:(m + 1) * MIXER_W, :])
        acc = t if acc is None else acc + t
    tm = acc.shape[0]
    ms = jnp.mean(acc * acc, axis=-1, keepdims=True)
    y = (acc * lax.rsqrt(ms + EPS)) * g_ref[...]
    x_new = _stream_tile(x_refs, row0) + _tile_mod(gc_ref, gl_ref, tm, row0) * y
    out_ref[0] = x_new
    h = _norm_modulate(x_new, g2_ref[...], _tile_mod(shc_ref, shl_ref, tm, row0),
                       _tile_mod(scc_ref, scl_ref, tm, row0))
    h_ref[0] = h.astype(BF16)
    w_hi, w_lo = _split(wr_ref[...])
    h_hi, h_lo = _split(h)
    lg_ref[0] = _dot_nt(w_hi, h_hi) + (_dot_nt(w_hi, h_lo) + _dot_nt(w_lo, h_hi))


def _outproj(o_mix, w_out, stream, mod, gain, gain_ffn, w_router_t, row0):
    b, n_tok, _ = o_mix[0].shape
    tm = _token_tile(n_tok)
    x_specs, x_args = _stream_specs(stream, row0, tm)
    d = x_args[0].shape[-1]
    n_e = w_router_t.shape[0]
    o_spec = pl.BlockSpec((1, tm, MIXER_W), lambda b_, i: (b_, i, 0))
    full = lambda a: pl.BlockSpec(a.shape, lambda b_, i: (0, 0))
    tok_spec = pl.BlockSpec((1, tm, d), lambda b_, i: (b_, i, 0))
    return pl.pallas_call(
        functools.partial(_outproj_body, row0=row0),
        grid=(b, n_tok // tm),
        in_specs=[o_spec, o_spec, o_spec, o_spec, full(w_out)] + _mod_specs(2) + [full(gain)]
        + _mod_specs(3) + _mod_specs(4) + [full(gain_ffn), full(w_router_t)] + x_specs,
        out_specs=[tok_spec, tok_spec, pl.BlockSpec((1, n_e, tm), lambda b_, i: (b_, 0, i))],
        out_shape=[jax.ShapeDtypeStruct((b, n_tok, d), F32),
                   jax.ShapeDtypeStruct((b, n_tok, d), BF16),
                   jax.ShapeDtypeStruct((b, n_e, n_tok), F32)],
        compiler_params=_cparams(2),
        name="outproj",
    )(*o_mix, w_out, mod, mod, gain, mod, mod, mod, mod, gain_ffn, w_router_t, *x_args)


def _exclusive_cumsum(x, tri):
    off = jnp.zeros((x.shape[0], 1), F32)
    outs = []
    for c in range(x.shape[1] // LANES):
        xc = x[:, c * LANES:(c + 1) * LANES]
        inc = _dot(xc.astype(BF16), tri)
        outs.append(inc - xc + off)
        off = off + inc[:, LANES - 1:LANES]
    return jnp.concatenate(outs, axis=1)


def _select_body(lg_ref, pos_ref, aff_ref, *, segments):
    r = lax.broadcasted_iota(jnp.int32, (LANES, LANES), 0)
    c = lax.broadcasted_iota(jnp.int32, (LANES, LANES), 1)
    tri = jnp.where(r <= c, 1.0, 0.0).astype(BF16)
    n_b, n_e = lg_ref.shape[:2]
    for t0, t, cap, slot0 in segments:
        affs = []
        for j in range(n_b):
            lg = lg_ref[j, :, t0:t0 + t]
            e = jnp.exp(lg - jnp.max(lg, axis=0, keepdims=True))
            affs.append(e / jnp.sum(e, axis=0, keepdims=True))
        aff = jnp.concatenate(affs, axis=0)

        def count_above(thr):
            return jnp.sum(jnp.where(aff > thr, 1.0, 0.0), axis=1, keepdims=True)

        def unsettled(carry):
            return jnp.logical_and(carry[2] > 0.0, carry[3] < BISECT_MAX_STEPS)

        def bisect(carry):
            lo, hi, _, step = carry
            mid = 0.5 * (lo + hi)
            cnt = count_above(mid)
            new_lo = jnp.where(cnt >= cap, mid, lo)
            new_hi = jnp.where(cnt <= cap, mid, hi)
            moving = jnp.logical_and(new_lo < new_hi, jnp.logical_and(mid > lo, mid < hi))
            return new_lo, new_hi, jnp.sum(jnp.where(moving, 1.0, 0.0)), step + 1

        lo0 = jnp.full((aff.shape[0], 1), -1.0, F32)
        hi0 = jnp.max(aff, axis=1, keepdims=True)
        lo, hi, _, _ = lax.while_loop(unsettled, bisect, (lo0, hi0, jnp.float32(1.0), jnp.int32(0)))
        gt = aff > hi
        eq = jnp.logical_and(aff > lo, aff <= hi)
        need = cap - count_above(hi)
        eq_rank = _exclusive_cumsum(jnp.where(eq, 1.0, 0.0), tri)
        sel = jnp.logical_or(gt, jnp.logical_and(eq, eq_rank < need))
        slot = _exclusive_cumsum(jnp.where(sel, 1.0, 0.0), tri) + slot0
        pos = jnp.where(sel, slot, -1.0).astype(jnp.int32)
        for j in range(n_b):
            pos_ref[j, :, t0:t0 + t] = pos[j * n_e:(j + 1) * n_e]
            aff_ref[j, :, t0:t0 + t] = affs[j]


def _select(logits, segments):
    b, n_e, lt = logits.shape
    spec = pl.BlockSpec((b, n_e, lt), lambda i: (0, 0, 0))
    return pl.pallas_call(
        functools.partial(_select_body, segments=segments),
        grid=(1,),
        in_specs=[spec],
        out_specs=[spec, spec],
        out_shape=[jax.ShapeDtypeStruct((b, n_e, lt), jnp.int32),
                   jax.ShapeDtypeStruct((b, n_e, lt), F32)],
        compiler_params=_cparams(1),
        name="select",
    )(logits)


def _slot_hits(pos_row, n_slots):
    return lax.broadcasted_iota(jnp.int32, (n_slots, pos_row.shape[1]), 0) == pos_row


def _gather_body(pos_ref, h_ref, xe_ref):
    n_grp, n_slots = xe_ref.shape[1:3]
    e0 = pl.program_id(1) * n_grp
    onehot = jnp.concatenate(
        [jnp.where(_slot_hits(pos_ref[0, pl.ds(e0 + j, 1), :], n_slots), 1.0, 0.0).astype(BF16)
         for j in range(n_grp)], axis=0)
    xe = _dot(onehot, h_ref[0]).astype(BF16)
    xe_ref[0] = xe.reshape(n_grp, n_slots, xe.shape[1])


def _gather(pos, h, n_slots):
    b, n_e, n_tok = pos.shape
    d = h.shape[2]
    return pl.pallas_call(
        _gather_body,
        grid=(b, n_e // GATHER_EXPERTS),
        in_specs=[pl.BlockSpec((1, n_e, n_tok), lambda b_, g: (b_, 0, 0)),
                  pl.BlockSpec((1, n_tok, d), lambda b_, g: (b_, 0, 0))],
        out_specs=pl.BlockSpec((1, GATHER_EXPERTS, n_slots, d), lambda b_, g: (b_, g, 0, 0)),
        out_shape=jax.ShapeDtypeStruct((b, n_e, n_slots, d), BF16),
        compiler_params=_cparams(2),
        name="gather",
    )(pos, h)


def _experts_body(pos_ref, aff_ref, xe_ref, wg_ref, wu_ref, wd_ref, y_ref, wg_s, wu_s, wd_s):
    @pl.when(pl.program_id(1) == 0)
    def _():
        wg_s[...] = wg_ref[0, 0].astype(BF16)
        wu_s[...] = wu_ref[0, 0].astype(BF16)
        wd_s[...] = wd_ref[0, 0].astype(BF16)

    n_b, _, n_slots, d = xe_ref.shape
    e = pl.program_id(0)
    x = xe_ref[...].reshape(n_b * n_slots, d)
    gate = jnp.concatenate(
        [jnp.sum(jnp.where(_slot_hits(pos_ref[j, pl.ds(e, 1), :], n_slots),
                           aff_ref[j, pl.ds(e, 1), :], 0.0), axis=1, keepdims=True)
         for j in range(n_b)], axis=0)
    y = None
    for c0 in range(0, wg_s.shape[1], FFN_CHUNK):
        a = _dot(x, wg_s[:, c0:c0 + FFN_CHUNK])
        u = _dot(x, wu_s[:, c0:c0 + FFN_CHUNK])
        act = ((a * jax.nn.sigmoid(a)) * u).astype(BF16)
        t = _dot(act, wd_s[c0:c0 + FFN_CHUNK, :])
        y = t if y is None else y + t
    y_ref[...] = (y * gate).astype(BF16).reshape(n_b, 1, n_slots, d)


def _experts(pos, aff, xe, w_gate, w_up, w_down, layer):
    b, n_e, n_tok = pos.shape
    n_slots = xe.shape[2]
    d, f = w_gate.shape[2:]
    nb = FFN_BATCHES if b % FFN_BATCHES == 0 else 1
    row_spec = pl.BlockSpec((nb, n_e, n_tok), lambda e, g: (g, 0, 0))
    slot_spec = pl.BlockSpec((nb, 1, n_slots, d), lambda e, g: (g, e, 0, 0))
    return pl.pallas_call(
        _experts_body,
        grid=(n_e, b // nb),
        in_specs=[row_spec, row_spec, slot_spec,
                  pl.BlockSpec((1, 1, d, f), lambda e, g: (layer, e, 0, 0)),
                  pl.BlockSpec((1, 1, d, f), lambda e, g: (layer, e, 0, 0)),
                  pl.BlockSpec((1, 1, f, d), lambda e, g: (layer, e, 0, 0))],
        out_specs=slot_spec,
        out_shape=jax.ShapeDtypeStruct(xe.shape, BF16),
        scratch_shapes=[pltpu.VMEM((d, f), BF16), pltpu.VMEM((d, f), BF16), pltpu.VMEM((f, d), BF16)],
        compiler_params=_cparams(2),
        name="experts",
    )(pos, aff, xe, w_gate, w_up, w_down)


def _combine_body(pos_ref, y_ref, x_ref, gc_ref, gl_ref, g_ref, out_ref, *, row0):
    n_e, n_slots = y_ref.shape[1:3]
    pos = pos_ref[0]
    slot = lax.broadcasted_iota(jnp.int32, (n_slots, pos.shape[1]), 0)
    onehot = jnp.concatenate(
        [jnp.where(slot == pos[e:e + 1, :], 1.0, 0.0).astype(BF16) for e in range(n_e)], axis=0)
    acc = lax.dot_general(onehot, y_ref[0].reshape(n_e * n_slots, y_ref.shape[3]),
                          (((0,), (0,)), ((), ())), preferred_element_type=F32)
    ms = jnp.mean(acc * acc, axis=-1, keepdims=True)
    y = (acc * lax.rsqrt(ms + EPS)) * g_ref[...]
    out_ref[0] = x_ref[0] + _tile_mod(gc_ref, gl_ref, acc.shape[0], row0) * y


def _combine(pos, y, xs, mod, gain, row0):
    b, n_tok, d = xs.shape
    tm = _token_tile(n_tok)
    n_e, n_slots = y.shape[1:3]
    return pl.pallas_call(
        functools.partial(_combine_body, row0=row0),
        grid=(b, n_tok // tm),
        in_specs=[pl.BlockSpec((1, n_e, tm), lambda b_, i: (b_, 0, i)),
                  pl.BlockSpec((1, n_e, n_slots, d), lambda b_, i: (b_, 0, 0, 0)),
                  pl.BlockSpec((1, tm, d), lambda b_, i: (b_, i, 0))] + _mod_specs(5) + [
                  pl.BlockSpec(gain.shape, lambda b_, i: (0, 0))],
        out_specs=pl.BlockSpec((1, tm, d), lambda b_, i: (b_, i, 0)),
        out_shape=jax.ShapeDtypeStruct(xs.shape, F32),
        compiler_params=_cparams(2),
        name="combine",
    )(pos, y, xs, mod, mod, gain)


def kernel(x, c, ctx, c_ctx, w_ada, b_ada, g_pre_mix, g_post_mix, g_pre_ffn, g_post_ffn, w_in, w_out, q_gain_a, k_gain_a, lam_q1, lam_k1, lam_q2, lam_k2, subln_gain_b, sink_c, rpb_d, w_router, w_gate, w_up, w_down):
    b, n_lat, d = x.shape
    n_ctx = ctx.shape[1]
    depth = w_ada.shape[0]
    assert n_ctx == TOK_TILE and n_lat % max(ATTN_STEP_ROWS.values()) == 0 and d == D_MODEL and b + 1 <= 16

    stream = (ctx, x)
    cc = jnp.zeros((16, d), F32).at[:b].set(c).at[b].set(c_ctx)
    mod_all = _ada(cc, w_ada, b_ada)
    mod_all = jnp.stack([jnp.broadcast_to(mod_all[:, b:b + 1], (depth, b, 6 * d)), mod_all[:, :b]],
                        axis=2).reshape(depth, b, 2, 6, 1, d)
    ca, sa = _rope_lane_tables(n_ctx, n_lat, HEAD_DIM)
    cb, sb = _rope_lane_tables(n_ctx, n_lat, DIFF_DIM)
    w_perm_all = _permute_w_in(w_in)
    w_out_all = w_out.astype(BF16)
    w_router_t = jnp.swapaxes(w_router, 1, 2)
    nbr_bias = _neighbourhood_bias(rpb_d, n_lat // GRID_W)
    pair_gain = lambda g: jnp.stack([jnp.tile(g[:, 0::2], (1, 4)), jnp.tile(g[:, 1::2], (1, 4))], axis=1)
    gq_all, gk_all = pair_gain(q_gain_a), pair_gain(k_gain_a)
    row = lambda v: v.reshape(1, -1)

    for l in range(depth):
        with_ctx = l < depth - 1
        row0 = 0 if with_ctx else 1
        lam_init = 0.8 - 0.6 * math.exp(-0.3 * l)
        mod = mod_all[l]
        qkv = _proj(stream, mod, row(g_pre_mix[l]), w_perm_all[l], ca, sa, cb, sb, gq_all[l], gk_all[l])

        lam = (jnp.exp(jnp.sum(lam_q1[l] * lam_k1[l])) - jnp.exp(jnp.sum(lam_q2[l] * lam_k2[l]))
               + lam_init).reshape(1)
        o_mix = (
            _attention(qkv, 0, "global", with_ctx),
            _attention(qkv, 1, "diff", with_ctx, scalars=lam, lam_init=lam_init,
                       gain=row(jnp.tile(subln_gain_b[l], N_HEADS))),
            _attention(qkv, 2, "window", with_ctx, scalars=sink_c[l]),
            _attention(qkv, 3, "nbr", with_ctx, bias=nbr_bias, layer=l),
        )
        xs, h, logits = _outproj(o_mix, w_out_all[l], stream, mod, row(g_post_mix[l]),
                                 row(g_pre_ffn[l]), w_router_t[l], row0)
        cap_lat = CAPACITY_FACTOR * n_lat // N_EXPERTS
        cap_ctx = CAPACITY_FACTOR * n_ctx // N_EXPERTS
        if with_ctx:
            segments = ((0, n_ctx, cap_ctx, 0), (n_ctx, n_lat, cap_lat, cap_ctx))
            n_slots = cap_ctx + cap_lat
        else:
            segments = ((0, n_lat, cap_lat, 0),)
            n_slots = cap_lat
        pos, aff = _select(logits, segments)
        y = _experts(pos, aff, _gather(pos, h, n_slots), w_gate, w_up, w_down, l)
        stream = _combine(pos, y, xs, mod, row(g_post_ffn[l]), row0)
    return stream
```

```python
import functools
import math

import numpy as np
import jax
import jax.numpy as jnp
from jax import lax
from jax.experimental import pallas as pl
from jax.experimental.pallas import tpu as pltpu

F32 = jnp.float32
BF16 = jnp.bfloat16

D_MODEL = 1024
GRID_W = 64
HEAD_DIM = 64
DIFF_DIM = 32
N_HEADS = 4
MIXER_W = N_HEADS * HEAD_DIM
QKV_W = 4 * 3 * MIXER_W
WINDOW = 128
NA_ROWS = 8
NA_COLS = 16
ROPE_BASE = 10000.0
N_EXPERTS = 16
CAPACITY_FACTOR = 2
EPS = 1e-6
NEG_INF = -1e30
LOG2E = 1.4426950408889634
PROJ_WIDTHS = (256, 128, 128, 256, 256, 256, 256, 128, 128, 256, 256, 256)
LANES = 128
TOK_TILE = 256
VMEM_LIMIT = 56 * 1024 * 1024
ATTN_SCORE_ROWS = {"global": 512, "diff": 512, "window": 256, "nbr": 256}
ATTN_STEP_ROWS = {"global": 1024, "diff": 1024, "window": 1024, "nbr": 1024}
NBR_TILE_ROWS = TOK_TILE // GRID_W
NBR_STRIP_ROWS = 12
GATHER_EXPERTS = 4
FFN_BATCHES = 4
FFN_CHUNK = 512
BISECT_MAX_STEPS = 192


def _cparams(n_axes):
    return pltpu.CompilerParams(dimension_semantics=("arbitrary",) * n_axes,
                                vmem_limit_bytes=VMEM_LIMIT)


def _proj_source_columns():
    offs = np.concatenate([[0], np.cumsum(PROJ_WIDTHS)])

    def pairs(base, n_slots, slot_src, n_pairs):
        x0 = [base + slot_src(j) + 2 * i for j in range(n_slots) for i in range(n_pairs)]
        return x0 + [c + 1 for c in x0]

    cols = []
    for m in range(4):
        q0, k0, v0 = offs[3 * m], offs[3 * m + 1], offs[3 * m + 2]
        if m in (0, 2):
            cols += pairs(q0, 4, lambda j: j * HEAD_DIM, 32)
            cols += pairs(k0, 4, lambda j: (j // 2) * HEAD_DIM, 32)
            cols += [v0 + (j // 2) * HEAD_DIM + d for j in range(4) for d in range(HEAD_DIM)]
        elif m == 1:
            cols += pairs(q0, 8, lambda j: j * DIFF_DIM, 16)
            cols += pairs(k0, 8, lambda j: j * DIFF_DIM, 16)
            cols += [v0 + c for c in range(MIXER_W)]
        else:
            cols += pairs(q0, 4, lambda j: j * HEAD_DIM, 32)
            cols += pairs(k0, 4, lambda j: j * HEAD_DIM, 32)
            cols += [v0 + c for c in range(MIXER_W)]
    return np.asarray(cols, dtype=np.int32)


def _permute_w_in(w_in):
    src = _proj_source_columns()
    pick = (jnp.arange(w_in.shape[-1], dtype=jnp.int32)[:, None] == src[None, :]).astype(BF16)
    return jnp.einsum('ldk,kn->ldn', w_in.astype(BF16), pick, preferred_element_type=BF16)


def _rope_tables(n_tokens, dim):
    t = jnp.arange(n_tokens, dtype=jnp.int32)
    rows = (t // GRID_W).astype(F32)
    cols = (t % GRID_W).astype(F32)
    n_axis = dim // 4
    inv_freq = ROPE_BASE ** (-jnp.arange(n_axis, dtype=F32) / n_axis)
    ang = jnp.concatenate([rows[:, None] * inv_freq, cols[:, None] * inv_freq], axis=-1)
    return jnp.cos(ang), jnp.sin(ang)


def _rope_lane_tables(n_ctx, n_lat, dim):
    cos, sin = _rope_tables(n_lat, dim)
    reps = LANES // cos.shape[1]
    cos = jnp.concatenate([jnp.ones((n_ctx, LANES), F32), jnp.tile(cos, (1, reps))], axis=0)
    sin = jnp.concatenate([jnp.zeros((n_ctx, LANES), F32), jnp.tile(sin, (1, reps))], axis=0)
    return cos, sin


def _nbr_strip_start(tile, n_rows):
    lo = tile * NBR_TILE_ROWS - NA_ROWS // 2
    return jnp.clip(lo, 0, n_rows - NBR_STRIP_ROWS) if isinstance(lo, jax.Array) else int(
        np.clip(lo, 0, n_rows - NBR_STRIP_ROWS))


def _nbr_pattern(tile, n_tiles):
    if isinstance(tile, jax.Array):
        return jnp.where(tile == 0, 0, jnp.where(tile == n_tiles - 1, 2, 1))
    return 0 if tile == 0 else (2 if tile == n_tiles - 1 else 1)


def _neighbourhood_bias(rpb, n_rows):
    n_tiles = n_rows // NBR_TILE_ROWS
    assert n_tiles >= 3 and n_rows >= NBR_STRIP_ROWS

    def rows_of(tile):
        ss = _nbr_strip_start(tile, n_rows)
        r = tile * NBR_TILE_ROWS + np.arange(NBR_TILE_ROWS)[:, None]
        rs = np.clip(r - NA_ROWS // 2, 0, n_rows - NA_ROWS)
        kr = ss + np.arange(NBR_STRIP_ROWS)[None, :]
        valid = (kr >= rs) & (kr < rs + NA_ROWS)
        return valid, np.where(valid, kr - r + NA_ROWS - 1, 0)

    reps = [rows_of(t) for t in (0, 1, n_tiles - 1)]
    for t in range(n_tiles):
        v, ri = rows_of(t)
        assert np.array_equal(v, reps[_nbr_pattern(t, n_tiles)][0])
        assert np.array_equal(ri, reps[_nbr_pattern(t, n_tiles)][1])
    n_r, n_c = 2 * NA_ROWS - 1, 2 * NA_COLS - 1
    row_valid = np.stack([v for v, _ in reps])
    ri = np.where(row_valid, np.stack([r for _, r in reps]), n_r)
    oh_r = ri[..., None] == np.arange(n_r + 1)
    cq = np.arange(GRID_W)
    col_start = np.clip(cq - NA_COLS // 2, 0, GRID_W - NA_COLS)
    col_valid = (cq[None, :] >= col_start[:, None]) & (cq[None, :] < col_start[:, None] + NA_COLS)
    ci = np.where(col_valid, np.clip(cq[None, :] - cq[:, None] + NA_COLS - 1, 0, n_c - 1), n_c)
    oh_c = np.arange(n_c + 1)[:, None, None] == ci[None]
    table = jnp.pad(rpb.astype(F32) * LOG2E, ((0, 0), (0, 0), (0, 1), (0, 1)), constant_values=NEG_INF)
    bias = jnp.einsum('lhrc,pajr,cqw->lhpaqjw', table, oh_r.astype(np.float32),
                      oh_c.astype(np.float32), precision=lax.Precision.HIGHEST)
    return bias.reshape(bias.shape[:2] + (3, TOK_TILE, NBR_STRIP_ROWS * GRID_W))


def _block_diag_ones(n, group):
    r = lax.broadcasted_iota(jnp.int32, (n, n), 0) // group
    c = lax.broadcasted_iota(jnp.int32, (n, n), 1) // group
    return jnp.where(r == c, 1.0, 0.0).astype(BF16)


def _dot(a, b):
    return jnp.dot(a, b, preferred_element_type=F32)


def _dot_nt(a, b):
    return lax.dot_general(a, b, (((1,), (1,)), ((), ())), preferred_element_type=F32)


def _split(a):
    hi = a.astype(BF16)
    return hi, (a - hi.astype(F32)).astype(BF16)


def _dot_hilo(a, b_bf16):
    hi, lo = _split(a)
    return _dot(hi, b_bf16) + _dot(lo, b_bf16)


def _lane_mask(cond):
    return jnp.where(cond, 1.0, 0.0).astype(BF16)


def _norm_modulate(x, gain, shift, scale):
    ms = jnp.mean(x * x, axis=-1, keepdims=True)
    h = (x * lax.rsqrt(ms + EPS)) * gain
    return h * (1.0 + scale) + shift


def _token_tile(n_tok):
    return next(t for t in (768, 512, TOK_TILE) if n_tok % t == 0)


def _mod_specs(chunk):
    return [pl.BlockSpec((1, 1, 1, 1, D_MODEL), lambda b, i, r=r: (b, r, chunk, 0, 0)) for r in (0, 1)]


def _tile_mod(mc_ref, ml_ref, n_rows, row0):
    lat = ml_ref[0, 0, 0]
    if row0 > 0:
        return lat
    row = lax.broadcasted_iota(jnp.int32, (n_rows, 1), 0)
    is_ctx = jnp.logical_and(row < TOK_TILE, pl.program_id(1) == 0)
    return jnp.where(is_ctx, mc_ref[0, 0, 0], lat)


def _ada_body(c_ref, w_ref, b_ref, o_ref):
    c = c_ref[...]
    cs = (c * jax.nn.sigmoid(c)).astype(BF16)
    o_ref[0] = _dot(cs, w_ref[0].astype(BF16)) + b_ref[0]


def _ada(cc, w_ada, b_ada):
    depth, d, n = w_ada.shape
    tn = 512
    return pl.pallas_call(
        _ada_body,
        grid=(depth, n // tn),
        in_specs=[pl.BlockSpec(cc.shape, lambda l, j: (0, 0)),
                  pl.BlockSpec((1, d, tn), lambda l, j: (l, 0, j)),
                  pl.BlockSpec((1, 1, tn), lambda l, j: (l, 0, j))],
        out_specs=pl.BlockSpec((1, cc.shape[0], tn), lambda l, j: (l, 0, j)),
        out_shape=jax.ShapeDtypeStruct((depth, cc.shape[0], n), F32),
        compiler_params=_cparams(2),
        name="ada",
    )(cc, w_ada, b_ada.reshape(depth, 1, n))


def _stream_specs(stream, row0, tm):
    ctx_arr, lat_arr = stream if isinstance(stream, tuple) else (stream, stream)
    lat_tile0 = 0 if isinstance(stream, tuple) else 1
    k = tm // TOK_TILE
    blk = (1, TOK_TILE, lat_arr.shape[-1])
    specs = [pl.BlockSpec(blk, lambda b_, i, j=j: (b_, jnp.maximum(i * k + row0 + j - 1, 0) + lat_tile0, 0))
             for j in range(k)]
    args = [lat_arr] * k
    if row0 == 0:
        specs, args = [pl.BlockSpec(blk, lambda b_, i: (b_, 0, 0))] + specs, [ctx_arr] + args
    return specs, args


def _stream_tile(x_refs, row0):
    parts = [r[0] for r in x_refs]
    if row0 == 0:
        parts = [jnp.where(pl.program_id(1) == 0, parts[0], parts[1])] + parts[2:]
    return parts[0] if len(parts) == 1 else jnp.concatenate(parts, axis=0)


def _proj_body(*refs):
    (shc_ref, shl_ref, scc_ref, scl_ref, g_ref, w_ref, ca_ref, sa_ref, cb_ref, sb_ref, gq_ref, gk_ref,
     o_ref) = refs[-13:]
    x = _stream_tile(refs[:-13], 0)
    tm = x.shape[0]
    h = _norm_modulate(x, g_ref[...], _tile_mod(shc_ref, shl_ref, tm, 0), _tile_mod(scc_ref, scl_ref, tm, 0))
    hb = h.astype(BF16)
    accs = [_dot(hb, w_ref[:, 3 * m * MIXER_W:3 * (m + 1) * MIXER_W]) for m in range(4)]
    ca, sa, cb, sb = ca_ref[...], sa_ref[...], cb_ref[...], sb_ref[...]
    bd = _block_diag_ones(LANES, 32)

    def get(m, j):
        return accs[m][:, j * LANES:(j + 1) * LANES]

    def put(m, j, v):
        c0 = (3 * m) * MIXER_W + j * LANES
        o_ref[0, :, c0:c0 + LANES] = v.astype(BF16)

    def rope(x0, x1, c, s):
        return x0 * c - x1 * s, x0 * s + x1 * c

    def head_norm(x0, x1, g_ref_):
        gs = _dot_hilo(x0 * x0 + x1 * x1, bd)
        r = lax.rsqrt(gs * (1.0 / HEAD_DIM) + EPS)
        return x0 * r * g_ref_[0:1, :], x1 * r * g_ref_[1:2, :]

    for m in range(4):
        q0, q1, k0, k1 = get(m, 0), get(m, 1), get(m, 2), get(m, 3)
        if m == 0:
            q0, q1 = head_norm(q0, q1, gq_ref)
            k0, k1 = head_norm(k0, k1, gk_ref)
        if m in (0, 2):
            q0, q1 = rope(q0, q1, ca, sa)
            k0, k1 = rope(k0, k1, ca, sa)
        elif m == 1:
            q0, q1 = rope(q0, q1, cb, sb)
            k0, k1 = rope(k0, k1, cb, sb)
        qscale = LOG2E * (DIFF_DIM if m == 1 else HEAD_DIM) ** -0.5
        put(m, 0, q0 * qscale)
        put(m, 1, q1 * qscale)
        put(m, 2, k0)
        put(m, 3, k1)
        put(m, 4, get(m, 4))
        put(m, 5, get(m, 5))


def _proj(stream, mod, gain, w_perm, ca, sa, cb, sb, gq, gk):
    lt = ca.shape[0]
    tm = _token_tile(lt)
    x_specs, x_args = _stream_specs(stream, 0, tm)
    b = x_args[0].shape[0]
    tab = pl.BlockSpec((tm, LANES), lambda b_, i: (i, 0))
    full = lambda a: pl.BlockSpec(a.shape, lambda b_, i: (0,) * a.ndim)
    return pl.pallas_call(
        _proj_body,
        grid=(b, lt // tm),
        in_specs=x_specs + _mod_specs(0) + _mod_specs(1) + [
            full(gain), full(w_perm), tab, tab, tab, tab, full(gq), full(gk)],
        out_specs=pl.BlockSpec((1, tm, QKV_W), lambda b_, i: (b_, i, 0)),
        out_shape=jax.ShapeDtypeStruct((b, lt, QKV_W), BF16),
        compiler_params=_cparams(2),
        name="proj",
    )(*x_args, mod, mod, mod, mod, gain, w_perm, ca, sa, cb, sb, gq, gk)


def _sum_lane(h):
    return (HEAD_DIM * (h + 1)) % MIXER_W


def _fill_head_values(v_ref, vm_ref):
    lane = lax.broadcasted_iota(jnp.int32, (1, MIXER_W), 1)
    v = v_ref[0]
    for h in range(N_HEADS):
        tap = jnp.where(lane == _sum_lane(h), 1.0, 0.0).astype(BF16)
        vm_ref[h] = jnp.where(lane // HEAD_DIM == h, v, tap)


def _attend(qm, kk, vm, bias=None, valid=None, extra_logit=None):
    s = _dot_nt(qm, kk)
    if bias is not None:
        n0 = s.shape[1] - bias.shape[1]
        s = jnp.concatenate([s[:, :n0], s[:, n0:] + bias], axis=1)
    if valid is not None:
        s = jnp.where(valid, s, NEG_INF)
    m = jnp.max(s, axis=-1, keepdims=True)
    if extra_logit is not None:
        m = jnp.maximum(m, extra_logit)
    return _dot(jnp.exp2(s - m).astype(BF16), vm), m


def _attn_body(*refs, kind, with_ctx, lam_init, n_ctx, n_lat):
    sub = ATTN_SCORE_ROWS[kind]
    n_sub = ATTN_STEP_ROWS[kind] // sub
    refs = list(refs)
    scalar_ref = refs.pop(0) if kind in ("diff", "window") else None
    q_ref, k_ref, v_ref = refs[:3]
    refs = refs[3:]
    gain_ref = refs.pop(0) if kind == "diff" else None
    bias_refs = [refs.pop(0) for _ in range(n_sub)] if kind == "nbr" else None
    o_ref, vm_ref = refs
    step = pl.program_id(1)
    lane = lax.broadcasted_iota(jnp.int32, (1, MIXER_W), 1)
    band = sub + 2 * WINDOW
    strip = NBR_STRIP_ROWS * GRID_W
    out0 = n_ctx if with_ctx else 0

    def rows_of(ref, ranges, *lead):
        parts = [ref[lead + (pl.ds(start, size), slice(None))] for start, size in ranges]
        return parts[0] if len(parts) == 1 else jnp.concatenate(parts, axis=0)

    def tile(q_row, n_rows, out_row, t=None, bias_ref=None):
        q = q_ref[0, pl.ds(q_row, n_rows), :]
        ranges, valid = [(0, n_ctx)], None
        if t is not None:
            if kind in ("global", "diff"):
                ranges = [(0, n_ctx + n_lat)]
            elif kind == "window":
                k_start = jnp.clip(t * sub - WINDOW, 0, n_lat - band)
                ranges.append((pl.multiple_of(n_ctx + k_start, LANES), band))
                col = lax.broadcasted_iota(jnp.int32, (1, n_ctx + band), 1)
                row = lax.broadcasted_iota(jnp.int32, (n_rows, 1), 0)
                dist = (k_start + col - n_ctx) - (t * sub + row)
                valid = jnp.abs(jnp.where(col < n_ctx, 0, dist)) <= WINDOW
            else:
                rs = _nbr_strip_start(t, n_lat // GRID_W)
                ranges.append((pl.multiple_of(n_ctx + rs * GRID_W, GRID_W), strip))
        kk = rows_of(k_ref, ranges, 0)
        out = jnp.zeros((n_rows, MIXER_W), F32)
        for h in range(N_HEADS):
            vm = rows_of(vm_ref, ranges, h)
            tap = _sum_lane(h)
            if kind == "diff":
                o = None
                for c in range(2):
                    qm = q * _lane_mask((lane % LANES) // (DIFF_DIM // 2) == 2 * h + c)
                    pv, _ = _attend(qm, kk, vm)
                    coef = 1.0 if c == 0 else scalar_ref[0]
                    term = pv * (coef / pv[:, tap:tap + 1])
                    o = term if c == 0 else o - term
            else:
                qm = q * _lane_mask((lane % LANES) // (HEAD_DIM // 2) == h)
                if kind == "window":
                    sink = scalar_ref[h] * LOG2E
                    pv, m = _attend(qm, kk, vm, valid=valid, extra_logit=sink)
                    o = pv * (1.0 / (pv[:, tap:tap + 1] + jnp.exp2(sink - m)))
                else:
                    bias = bias_ref[0, h, 0] if (kind == "nbr" and t is not None) else None
                    pv, _ = _attend(qm, kk, vm, bias=bias)
                    o = pv * (1.0 / pv[:, tap:tap + 1])
            out = jnp.where(lane // HEAD_DIM == h, o, out)
        if kind == "diff":
            gs = _dot_hilo(out * out, _block_diag_ones(MIXER_W, HEAD_DIM))
            out = out * lax.rsqrt(gs * (1.0 / HEAD_DIM) + EPS) * gain_ref[...] * (1.0 - lam_init)
        o_ref[0, pl.ds(out_row, n_rows), :] = out.astype(BF16)

    @pl.when(step == 0)
    def _():
        _fill_head_values(v_ref, vm_ref)
        if with_ctx:
            tile(0, n_ctx, 0)

    for j in range(n_sub):
        t = step * n_sub + j
        row = pl.multiple_of(t * sub, sub)
        tile(n_ctx + row, sub, out0 + row, t=t, bias_ref=bias_refs[j] if bias_refs else None)


def _attention(qkv, mixer, kind, with_ctx, *, scalars=None, gain=None, bias=None, layer=0, lam_init=0.0):
    b, lt, _ = qkv.shape
    n_lat = lt - TOK_TILE
    n_out = lt if with_ctx else n_lat
    col = lambda j: pl.BlockSpec((1, lt, MIXER_W), lambda b_, i: (b_, 0, 3 * mixer + j))
    in_specs, args = [col(0), col(1), col(2)], [qkv, qkv, qkv]
    if scalars is not None:
        in_specs, args = [pl.BlockSpec(memory_space=pltpu.SMEM)] + in_specs, [scalars] + args
    if gain is not None:
        in_specs.append(pl.BlockSpec(gain.shape, lambda b_, i: (0, 0)))
        args.append(gain)
    if bias is not None:
        n_sub = ATTN_STEP_ROWS[kind] // ATTN_SCORE_ROWS[kind]
        n_tiles = n_lat // TOK_TILE
        for j in range(n_sub):
            in_specs.append(pl.BlockSpec(
                (1, N_HEADS, 1) + bias.shape[3:],
                lambda b_, i, j=j: (layer, 0, _nbr_pattern(i * n_sub + j, n_tiles), 0, 0)))
            args.append(bias)
    return pl.pallas_call(
        functools.partial(_attn_body, kind=kind, with_ctx=with_ctx, lam_init=lam_init,
                          n_ctx=TOK_TILE, n_lat=n_lat),
        grid=(b, n_lat // ATTN_STEP_ROWS[kind]),
        in_specs=in_specs,
        out_specs=pl.BlockSpec((1, n_out, MIXER_W), lambda b_, i: (b_, 0, 0)),
        out_shape=jax.ShapeDtypeStruct((b, n_out, MIXER_W), BF16),
        scratch_shapes=[pltpu.VMEM((N_HEADS, lt, MIXER_W), BF16)],
        compiler_params=_cparams(2),
        name="attn_" + kind,
    )(*args)


def _outproj_body(oa_ref, ob_ref, oc_ref, od_ref, w_ref, gc_ref, gl_ref, g_ref,
                  shc_ref, shl_ref, scc_ref, scl_ref, g2_ref, wr_ref, *rest, row0):
    x_refs, (out_ref, h_ref, lg_ref) = rest[:-3], rest[-3:]
    acc = None
    for m, o_ref in enumerate((oa_ref, ob_ref, oc_ref, od_ref)):
        t = _dot(o_ref[0], w_ref[m * MIXER_W:(m + 1) * MIXER_W, :])
        acc = t if acc is None else acc + t
    tm = acc.shape[0]
    ms = jnp.mean(acc * acc, axis=-1, keepdims=True)
    y = (acc * lax.rsqrt(ms + EPS)) * g_ref[...]
    x_new = _stream_tile(x_refs, row0) + _tile_mod(gc_ref, gl_ref, tm, row0) * y
    out_ref[0] = x_new
    h = _norm_modulate(x_new, g2_ref[...], _tile_mod(shc_ref, shl_ref, tm, row0),
                       _tile_mod(scc_ref, scl_ref, tm, row0))
    h_ref[0] = h.astype(BF16)
    w_hi, w_lo = _split(wr_ref[...])
    h_hi, h_lo = _split(h)
    lg_ref[0] = _dot_nt(w_hi, h_hi) + (_dot_nt(w_hi, h_lo) + _dot_nt(w_lo, h_hi))


def _outproj(o_mix, w_out, stream, mod, gain, gain_ffn, w_router_t, row0):
    b, n_tok, _ = o_mix[0].shape
    tm = _token_tile(n_tok)
    x_specs, x_args = _stream_specs(stream, row0, tm)
    d = x_args[0].shape[-1]
    n_e = w_router_t.shape[0]
    o_spec = pl.BlockSpec((1, tm, MIXER_W), lambda b_, i: (b_, i, 0))
    full = lambda a: pl.BlockSpec(a.shape, lambda b_, i: (0, 0))
    tok_spec = pl.BlockSpec((1, tm, d), lambda b_, i: (b_, i, 0))
    return pl.pallas_call(
        functools.partial(_outproj_body, row0=row0),
        grid=(b, n_tok // tm),
        in_specs=[o_spec, o_spec, o_spec, o_spec, full(w_out)] + _mod_specs(2) + [full(gain)]
        + _mod_specs(3) + _mod_specs(4) + [full(gain_ffn), full(w_router_t)] + x_specs,
        out_specs=[tok_spec, tok_spec, pl.BlockSpec((1, n_e, tm), lambda b_, i: (b_, 0, i))],
        out_shape=[jax.ShapeDtypeStruct((b, n_tok, d), F32),
                   jax.ShapeDtypeStruct((b, n_tok, d), BF16),
                   jax.ShapeDtypeStruct((b, n_e, n_tok), F32)],
        compiler_params=_cparams(2),
        name="outproj",
    )(*o_mix, w_out, mod, mod, gain, mod, mod, mod, mod, gain_ffn, w_router_t, *x_args)


def _exclusive_cumsum(x, tri):
    off = jnp.zeros((x.shape[0], 1), F32)
    outs = []
    for c in range(x.shape[1] // LANES):
        xc = x[:, c * LANES:(c + 1) * LANES]
        inc = _dot(xc.astype(BF16), tri)
        outs.append(inc - xc + off)
        off = off + inc[:, LANES - 1:LANES]
    return jnp.concatenate(outs, axis=1)


def _select_body(lg_ref, pos_ref, aff_ref, *, segments):
    r = lax.broadcasted_iota(jnp.int32, (LANES, LANES), 0)
    c = lax.broadcasted_iota(jnp.int32, (LANES, LANES), 1)
    tri = jnp.where(r <= c, 1.0, 0.0).astype(BF16)
    n_b, n_e = lg_ref.shape[:2]
    for t0, t, cap, slot0 in segments:
        affs = []
        for j in range(n_b):
            lg = lg_ref[j, :, t0:t0 + t]
            e = jnp.exp(lg - jnp.max(lg, axis=0, keepdims=True))
            affs.append(e / jnp.sum(e, axis=0, keepdims=True))
        aff = jnp.concatenate(affs, axis=0)

        def count_above(thr):
            return jnp.sum(jnp.where(aff > thr, 1.0, 0.0), axis=1, keepdims=True)

        def unsettled(carry):
            return jnp.logical_and(carry[2] > 0.0, carry[3] < BISECT_MAX_STEPS)

        def bisect(carry):
            lo, hi, _, step = carry
            mid = 0.5 * (lo + hi)
            cnt = count_above(mid)
            new_lo = jnp.where(cnt >= cap, mid, lo)
            new_hi = jnp.where(cnt <= cap, mid, hi)
            moving = jnp.logical_and(new_lo < new_hi, jnp.logical_and(mid > lo, mid < hi))
            return new_lo, new_hi, jnp.sum(jnp.where(moving, 1.0, 0.0)), step + 1

        lo0 = jnp.full((aff.shape[0], 1), -1.0, F32)
        hi0 = jnp.max(aff, axis=1, keepdims=True)
        lo, hi, _, _ = lax.while_loop(unsettled, bisect, (lo0, hi0, jnp.float32(1.0), jnp.int32(0)))
        gt = aff > hi
        eq = jnp.logical_and(aff > lo, aff <= hi)
        need = cap - count_above(hi)
        eq_rank = _exclusive_cumsum(jnp.where(eq, 1.0, 0.0), tri)
        sel = jnp.logical_or(gt, jnp.logical_and(eq, eq_rank < need))
        slot = _exclusive_cumsum(jnp.where(sel, 1.0, 0.0), tri) + slot0
        pos = jnp.where(sel, slot, -1.0).astype(jnp.int32)
        for j in range(n_b):
            pos_ref[j, :, t0:t0 + t] = pos[j * n_e:(j + 1) * n_e]
            aff_ref[j, :, t0:t0 + t] = affs[j]


def _select(logits, segments):
    b, n_e, lt = logits.shape
    spec = pl.BlockSpec((b, n_e, lt), lambda i: (0, 0, 0))
    return pl.pallas_call(
        functools.partial(_select_body, segments=segments),
        grid=(1,),
        in_specs=[spec],
        out_specs=[spec, spec],
        out_shape=[jax.ShapeDtypeStruct((b, n_e, lt), jnp.int32),
                   jax.ShapeDtypeStruct((b, n_e, lt), F32)],
        compiler_params=_cparams(1),
        name="select",
    )(logits)


def _slot_hits(pos_row, n_slots):
    return lax.broadcasted_iota(jnp.int32, (n_slots, pos_row.shape[1]), 0) == pos_row


def _gather_body(pos_ref, h_ref, xe_ref):
    n_grp, n_slots = xe_ref.shape[1:3]
    e0 = pl.program_id(1) * n_grp
    onehot = jnp.concatenate(
        [jnp.where(_slot_hits(pos_ref[0, pl.ds(e0 + j, 1), :], n_slots), 1.0, 0.0).astype(BF16)
         for j in range(n_grp)], axis=0)
    xe = _dot(onehot, h_ref[0]).astype(BF16)
    xe_ref[0] = xe.reshape(n_grp, n_slots, xe.shape[1])


def _gather(pos, h, n_slots):
    b, n_e, n_tok = pos.shape
    d = h.shape[2]
    return pl.pallas_call(
        _gather_body,
        grid=(b, n_e // GATHER_EXPERTS),
        in_specs=[pl.BlockSpec((1, n_e, n_tok), lambda b_, g: (b_, 0, 0)),
                  pl.BlockSpec((1, n_tok, d), lambda b_, g: (b_, 0, 0))],
        out_specs=pl.BlockSpec((1, GATHER_EXPERTS, n_slots, d), lambda b_, g: (b_, g, 0, 0)),
        out_shape=jax.ShapeDtypeStruct((b, n_e, n_slots, d), BF16),
        compiler_params=_cparams(2),
        name="gather",
    )(pos, h)


def _experts_body(pos_ref, aff_ref, xe_ref, wg_ref, wu_ref, wd_ref, y_ref, wg_s, wu_s, wd_s):
    @pl.when(pl.program_id(1) == 0)
    def _():
        wg_s[...] = wg_ref[0, 0].astype(BF16)
        wu_s[...] = wu_ref[0, 0].astype(BF16)
        wd_s[...] = wd_ref[0, 0].astype(BF16)

    n_b, _, n_slots, d = xe_ref.shape
    e = pl.program_id(0)
    x = xe_ref[...].reshape(n_b * n_slots, d)
    gate = jnp.concatenate(
        [jnp.sum(jnp.where(_slot_hits(pos_ref[j, pl.ds(e, 1), :], n_slots),
                           aff_ref[j, pl.ds(e, 1), :], 0.0), axis=1, keepdims=True)
         for j in range(n_b)], axis=0)
    y = None
    for c0 in range(0, wg_s.shape[1], FFN_CHUNK):
        a = _dot(x, wg_s[:, c0:c0 + FFN_CHUNK])
        u = _dot(x, wu_s[:, c0:c0 + FFN_CHUNK])
        act = ((a * jax.nn.sigmoid(a)) * u).astype(BF16)
        t = _dot(act, wd_s[c0:c0 + FFN_CHUNK, :])
        y = t if y is None else y + t
    y_ref[...] = (y * gate).astype(BF16).reshape(n_b, 1, n_slots, d)


def _experts(pos, aff, xe, w_gate, w_up, w_down, layer):
    b, n_e, n_tok = pos.shape
    n_slots = xe.shape[2]
    d, f = w_gate.shape[2:]
    nb = FFN_BATCHES if b % FFN_BATCHES == 0 else 1
    row_spec = pl.BlockSpec((nb, n_e, n_tok), lambda e, g: (g, 0, 0))
    slot_spec = pl.BlockSpec((nb, 1, n_slots, d), lambda e, g: (g, e, 0, 0))
    return pl.pallas_call(
        _experts_body,
        grid=(n_e, b // nb),
        in_specs=[row_spec, row_spec, slot_spec,
                  pl.BlockSpec((1, 1, d, f), lambda e, g: (layer, e, 0, 0)),
                  pl.BlockSpec((1, 1, d, f), lambda e, g: (layer, e, 0, 0)),
                  pl.BlockSpec((1, 1, f, d), lambda e, g: (layer, e, 0, 0))],
        out_specs=slot_spec,
        out_shape=jax.ShapeDtypeStruct(xe.shape, BF16),
        scratch_shapes=[pltpu.VMEM((d, f), BF16), pltpu.VMEM((d, f), BF16), pltpu.VMEM((f, d), BF16)],
        compiler_params=_cparams(2),
        name="experts",
    )(pos, aff, xe, w_gate, w_up, w_down)


def _combine_body(pos_ref, y_ref, x_ref, gc_ref, gl_ref, g_ref, out_ref, *, row0):
    n_e, n_slots = y_ref.shape[1:3]
    pos = pos_ref[0]
    slot = lax.broadcasted_iota(jnp.int32, (n_slots, pos.shape[1]), 0)
    onehot = jnp.concatenate(
        [jnp.where(slot == pos[e:e + 1, :], 1.0, 0.0).astype(BF16) for e in range(n_e)], axis=0)
    acc = lax.dot_general(onehot, y_ref[0].reshape(n_e * n_slots, y_ref.shape[3]),
                          (((0,), (0,)), ((), ())), preferred_element_type=F32)
    ms = jnp.mean(acc * acc, axis=-1, keepdims=True)
    y = (acc * lax.rsqrt(ms + EPS)) * g_ref[...]
    out_ref[0] = x_ref[0] + _tile_mod(gc_ref, gl_ref, acc.shape[0], row0) * y


def _combine(pos, y, xs, mod, gain, row0):
    b, n_tok, d = xs.shape
    tm = _token_tile(n_tok)
    n_e, n_slots = y.shape[1:3]
    return pl.pallas_call(
        functools.partial(_combine_body, row0=row0),
        grid=(b, n_tok // tm),
        in_specs=[pl.BlockSpec((1, n_e, tm), lambda b_, i: (b_, 0, i)),
                  pl.BlockSpec((1, n_e, n_slots, d), lambda b_, i: (b_, 0, 0, 0)),
                  pl.BlockSpec((1, tm, d), lambda b_, i: (b_, i, 0))] + _mod_specs(5) + [
                  pl.BlockSpec(gain.shape, lambda b_, i: (0, 0))],
        out_specs=pl.BlockSpec((1, tm, d), lambda b_, i: (b_, i, 0)),
        out_shape=jax.ShapeDtypeStruct(xs.shape, F32),
        compiler_params=_cparams(2),
        name="combine",
    )(pos, y, xs, mod, mod, gain)


def kernel(x, c, ctx, c_ctx, w_ada, b_ada, g_pre_mix, g_post_mix, g_pre_ffn, g_post_ffn, w_in, w_out, q_gain_a, k_gain_a, lam_q1, lam_k1, lam_q2, lam_k2, subln_gain_b, sink_c, rpb_d, w_router, w_gate, w_up, w_down):
    b, n_lat, d = x.shape
    n_ctx = ctx.shape[1]
    depth = w_ada.shape[0]
    assert n_ctx == TOK_TILE and n_lat % max(ATTN_STEP_ROWS.values()) == 0 and d == D_MODEL and b + 1 <= 16

    stream = (ctx, x)
    cc = jnp.zeros((16, d), F32).at[:b].set(c).at[b].set(c_ctx)
    mod_all = _ada(cc, w_ada, b_ada)
    mod_all = jnp.stack([jnp.broadcast_to(mod_all[:, b:b + 1], (depth, b, 6 * d)), mod_all[:, :b]],
                        axis=2).reshape(depth, b, 2, 6, 1, d)
    ca, sa = _rope_lane_tables(n_ctx, n_lat, HEAD_DIM)
    cb, sb = _rope_lane_tables(n_ctx, n_lat, DIFF_DIM)
    w_perm_all = _permute_w_in(w_in)
    w_out_all = w_out.astype(BF16)
    w_router_t = jnp.swapaxes(w_router, 1, 2)
    nbr_bias = _neighbourhood_bias(rpb_d, n_lat // GRID_W)
    pair_gain = lambda g: jnp.stack([jnp.tile(g[:, 0::2], (1, 4)), jnp.tile(g[:, 1::2], (1, 4))], axis=1)
    gq_all, gk_all = pair_gain(q_gain_a), pair_gain(k_gain_a)
    row = lambda v: v.reshape(1, -1)

    for l in range(depth):
        with_ctx = l < depth - 1
        row0 = 0 if with_ctx else 1
        lam_init = 0.8 - 0.6 * math.exp(-0.3 * l)
        mod = mod_all[l]
        qkv = _proj(stream, mod, row(g_pre_mix[l]), w_perm_all[l], ca, sa, cb, sb, gq_all[l], gk_all[l])

        lam = (jnp.exp(jnp.sum(lam_q1[l] * lam_k1[l])) - jnp.exp(jnp.sum(lam_q2[l] * lam_k2[l]))
               + lam_init).reshape(1)
        o_mix = (
            _attention(qkv, 0, "global", with_ctx),
            _attention(qkv, 1, "diff", with_ctx, scalars=lam, lam_init=lam_init,
                       gain=row(jnp.tile(subln_gain_b[l], N_HEADS))),
            _attention(qkv, 2, "window", with_ctx, scalars=sink_c[l]),
            _attention(qkv, 3, "nbr", with_ctx, bias=nbr_bias, layer=l),
        )
        xs, h, logits = _outproj(o_mix, w_out_all[l], stream, mod, row(g_post_mix[l]),
                                 row(g_pre_ffn[l]), w_router_t[l], row0)
        cap_lat = CAPACITY_FACTOR * n_lat // N_EXPERTS
        cap_ctx = CAPACITY_FACTOR * n_ctx // N_EXPERTS
        if with_ctx:
            segments = ((0, n_ctx, cap_ctx, 0), (n_ctx, n_lat, cap_lat, cap_ctx))
            n_slots = cap_ctx + cap_lat
        else:
            segments = ((0, n_lat, cap_lat, 0),)
            n_slots = cap_lat
        pos, aff = _select(logits, segments)
        y = _experts(pos, aff, _gather(pos, h, n_slots), w_gate, w_up, w_down, l)
        stream = _combine(pos, y, xs, mod, row(g_post_ffn[l]), row0)
    return stream
```

```python
import functools
import math

import numpy as np
import jax
import jax.numpy as jnp
from jax import lax
from jax.experimental import pallas as pl
from jax.experimental.pallas import tpu as pltpu

F32 = jnp.float32
BF16 = jnp.bfloat16

D_MODEL = 1024
GRID_W = 64
HEAD_DIM = 64
DIFF_DIM = 32
N_HEADS = 4
MIXER_W = N_HEADS * HEAD_DIM
QKV_W = 4 * 3 * MIXER_W
WINDOW = 128
NA_ROWS = 8
NA_COLS = 16
ROPE_BASE = 10000.0
N_EXPERTS = 16
CAPACITY_FACTOR = 2
EPS = 1e-6
NEG_INF = -1e30
LOG2E = 1.4426950408889634
PROJ_WIDTHS = (256, 128, 128, 256, 256, 256, 256, 128, 128, 256, 256, 256)
LANES = 128
TOK_TILE = 256
VMEM_LIMIT = 56 * 1024 * 1024
ATTN_SCORE_ROWS = {"global": 512, "diff": 512, "window": 256, "nbr": 256}
ATTN_STEP_ROWS = {"global": 2048, "diff": 2048, "window": 2048, "nbr": 1024}
NBR_TILE_ROWS = TOK_TILE // GRID_W
NBR_STRIP_ROWS = 12
GATHER_EXPERTS = 4
FFN_BATCHES = 4
FFN_CHUNK = 512
BISECT_MAX_STEPS = 192


def _cparams(n_axes):
    return pltpu.CompilerParams(dimension_semantics=("arbitrary",) * n_axes,
                                vmem_limit_bytes=VMEM_LIMIT)


def _proj_source_columns():
    offs = np.concatenate([[0], np.cumsum(PROJ_WIDTHS)])

    def pairs(base, n_slots, slot_src, n_pairs):
        x0 = [base + slot_src(j) + 2 * i for j in range(n_slots) for i in range(n_pairs)]
        return x0 + [c + 1 for c in x0]

    cols = []
    for m in range(4):
        q0, k0, v0 = offs[3 * m], offs[3 * m + 1], offs[3 * m + 2]
        if m in (0, 2):
            cols += pairs(q0, 4, lambda j: j * HEAD_DIM, 32)
            cols += pairs(k0, 4, lambda j: (j // 2) * HEAD_DIM, 32)
            cols += [v0 + (j // 2) * HEAD_DIM + d for j in range(4) for d in range(HEAD_DIM)]
        elif m == 1:
            cols += pairs(q0, 8, lambda j: j * DIFF_DIM, 16)
            cols += pairs(k0, 8, lambda j: j * DIFF_DIM, 16)
            cols += [v0 + c for c in range(MIXER_W)]
        else:
            cols += pairs(q0, 4, lambda j: j * HEAD_DIM, 32)
            cols += pairs(k0, 4, lambda j: j * HEAD_DIM, 32)
            cols += [v0 + c for c in range(MIXER_W)]
    return np.asarray(cols, dtype=np.int32)


def _permute_w_in(w_in):
    src = _proj_source_columns()
    pick = (jnp.arange(w_in.shape[-1], dtype=jnp.int32)[:, None] == src[None, :]).astype(BF16)
    return jnp.einsum('ldk,kn->ldn', w_in.astype(BF16), pick, preferred_element_type=BF16)


def _rope_tables(n_tokens, dim):
    t = jnp.arange(n_tokens, dtype=jnp.int32)
    rows = (t // GRID_W).astype(F32)
    cols = (t % GRID_W).astype(F32)
    n_axis = dim // 4
    inv_freq = ROPE_BASE ** (-jnp.arange(n_axis, dtype=F32) / n_axis)
    ang = jnp.concatenate([rows[:, None] * inv_freq, cols[:, None] * inv_freq], axis=-1)
    return jnp.cos(ang), jnp.sin(ang)


def _rope_lane_tables(n_ctx, n_lat, dim):
    cos, sin = _rope_tables(n_lat, dim)
    reps = LANES // cos.shape[1]
    cos = jnp.concatenate([jnp.ones((n_ctx, LANES), F32), jnp.tile(cos, (1, reps))], axis=0)
    sin = jnp.concatenate([jnp.zeros((n_ctx, LANES), F32), jnp.tile(sin, (1, reps))], axis=0)
    return cos, sin


def _nbr_strip_start(tile, n_rows):
    lo = tile * NBR_TILE_ROWS - NA_ROWS // 2
    return jnp.clip(lo, 0, n_rows - NBR_STRIP_ROWS) if isinstance(lo, jax.Array) else int(
        np.clip(lo, 0, n_rows - NBR_STRIP_ROWS))


def _nbr_pattern(tile, n_tiles):
    if isinstance(tile, jax.Array):
        return jnp.where(tile == 0, 0, jnp.where(tile == n_tiles - 1, 2, 1))
    return 0 if tile == 0 else (2 if tile == n_tiles - 1 else 1)


def _neighbourhood_bias(rpb, n_rows):
    n_tiles = n_rows // NBR_TILE_ROWS
    assert n_tiles >= 3 and n_rows >= NBR_STRIP_ROWS

    def rows_of(tile):
        ss = _nbr_strip_start(tile, n_rows)
        r = tile * NBR_TILE_ROWS + np.arange(NBR_TILE_ROWS)[:, None]
        rs = np.clip(r - NA_ROWS // 2, 0, n_rows - NA_ROWS)
        kr = ss + np.arange(NBR_STRIP_ROWS)[None, :]
        valid = (kr >= rs) & (kr < rs + NA_ROWS)
        return valid, np.where(valid, kr - r + NA_ROWS - 1, 0)

    reps = [rows_of(t) for t in (0, 1, n_tiles - 1)]
    for t in range(n_tiles):
        v, ri = rows_of(t)
        assert np.array_equal(v, reps[_nbr_pattern(t, n_tiles)][0])
        assert np.array_equal(ri, reps[_nbr_pattern(t, n_tiles)][1])
    n_r, n_c = 2 * NA_ROWS - 1, 2 * NA_COLS - 1
    row_valid = np.stack([v for v, _ in reps])
    ri = np.where(row_valid, np.stack([r for _, r in reps]), n_r)
    oh_r = ri[..., None] == np.arange(n_r + 1)
    cq = np.arange(GRID_W)
    col_start = np.clip(cq - NA_COLS // 2, 0, GRID_W - NA_COLS)
    col_valid = (cq[None, :] >= col_start[:, None]) & (cq[None, :] < col_start[:, None] + NA_COLS)
    ci = np.where(col_valid, np.clip(cq[None, :] - cq[:, None] + NA_COLS - 1, 0, n_c - 1), n_c)
    oh_c = np.arange(n_c + 1)[:, None, None] == ci[None]
    table = jnp.pad(rpb.astype(F32) * LOG2E, ((0, 0), (0, 0), (0, 1), (0, 1)), constant_values=NEG_INF)
    bias = jnp.einsum('lhrc,pajr,cqw->lhpaqjw', table, oh_r.astype(np.float32),
                      oh_c.astype(np.float32), precision=lax.Precision.HIGHEST)
    return bias.reshape(bias.shape[:2] + (3, TOK_TILE, NBR_STRIP_ROWS * GRID_W))


def _block_diag_ones(n, group):
    r = lax.broadcasted_iota(jnp.int32, (n, n), 0) // group
    c = lax.broadcasted_iota(jnp.int32, (n, n), 1) // group
    return jnp.where(r == c, 1.0, 0.0).astype(BF16)


def _dot(a, b):
    return jnp.dot(a, b, preferred_element_type=F32)


def _dot_nt(a, b):
    return lax.dot_general(a, b, (((1,), (1,)), ((), ())), preferred_element_type=F32)


def _split(a):
    hi = a.astype(BF16)
    return hi, (a - hi.astype(F32)).astype(BF16)


def _dot_hilo(a, b_bf16):
    hi, lo = _split(a)
    return _dot(hi, b_bf16) + _dot(lo, b_bf16)


def _lane_mask(cond):
    return jnp.where(cond, 1.0, 0.0).astype(BF16)


def _norm_modulate(x, gain, shift, scale):
    ms = jnp.mean(x * x, axis=-1, keepdims=True)
    h = (x * lax.rsqrt(ms + EPS)) * gain
    return h * (1.0 + scale) + shift


def _token_tile(n_tok):
    return next(t for t in (768, 512, TOK_TILE) if n_tok % t == 0)


def _mod_specs(chunk):
    return [pl.BlockSpec((1, 1, 1, 1, D_MODEL), lambda b, i, r=r: (b, r, chunk, 0, 0)) for r in (0, 1)]


def _tile_mod(mc_ref, ml_ref, n_rows, row0):
    lat = ml_ref[0, 0, 0]
    if row0 > 0:
        return lat
    row = lax.broadcasted_iota(jnp.int32, (n_rows, 1), 0)
    is_ctx = jnp.logical_and(row < TOK_TILE, pl.program_id(1) == 0)
    return jnp.where(is_ctx, mc_ref[0, 0, 0], lat)


def _ada_body(c_ref, w_ref, b_ref, o_ref):
    c = c_ref[...]
    cs = (c * jax.nn.sigmoid(c)).astype(BF16)
    o_ref[0] = _dot(cs, w_ref[0].astype(BF16)) + b_ref[0]


def _ada(cc, w_ada, b_ada):
    depth, d, n = w_ada.shape
    tn = 512
    return pl.pallas_call(
        _ada_body,
        grid=(depth, n // tn),
        in_specs=[pl.BlockSpec(cc.shape, lambda l, j: (0, 0)),
                  pl.BlockSpec((1, d, tn), lambda l, j: (l, 0, j)),
                  pl.BlockSpec((1, 1, tn), lambda l, j: (l, 0, j))],
        out_specs=pl.BlockSpec((1, cc.shape[0], tn), lambda l, j: (l, 0, j)),
        out_shape=jax.ShapeDtypeStruct((depth, cc.shape[0], n), F32),
        compiler_params=_cparams(2),
        name="ada",
    )(cc, w_ada, b_ada.reshape(depth, 1, n))


def _stream_specs(stream, row0, tm):
    ctx_arr, lat_arr = stream if isinstance(stream, tuple) else (stream, stream)
    lat_tile0 = 0 if isinstance(stream, tuple) else 1
    k = tm // TOK_TILE
    blk = (1, TOK_TILE, lat_arr.shape[-1])
    specs = [pl.BlockSpec(blk, lambda b_, i, j=j: (b_, jnp.maximum(i * k + row0 + j - 1, 0) + lat_tile0, 0))
             for j in range(k)]
    args = [lat_arr] * k
    if row0 == 0:
        specs, args = [pl.BlockSpec(blk, lambda b_, i: (b_, 0, 0))] + specs, [ctx_arr] + args
    return specs, args


def _stream_tile(x_refs, row0):
    parts = [r[0] for r in x_refs]
    if row0 == 0:
        parts = [jnp.where(pl.program_id(1) == 0, parts[0], parts[1])] + parts[2:]
    return parts[0] if len(parts) == 1 else jnp.concatenate(parts, axis=0)


def _proj_body(*refs):
    (shc_ref, shl_ref, scc_ref, scl_ref, g_ref, w_ref, ca_ref, sa_ref, cb_ref, sb_ref, gq_ref, gk_ref,
     o_ref) = refs[-13:]
    x = _stream_tile(refs[:-13], 0)
    tm = x.shape[0]
    h = _norm_modulate(x, g_ref[...], _tile_mod(shc_ref, shl_ref, tm, 0), _tile_mod(scc_ref, scl_ref, tm, 0))
    hb = h.astype(BF16)
    accs = [_dot(hb, w_ref[:, 3 * m * MIXER_W:3 * (m + 1) * MIXER_W]) for m in range(4)]
    ca, sa, cb, sb = ca_ref[...], sa_ref[...], cb_ref[...], sb_ref[...]
    bd = _block_diag_ones(LANES, 32)

    def get(m, j):
        return accs[m][:, j * LANES:(j + 1) * LANES]

    def put(m, j, v):
        c0 = (3 * m) * MIXER_W + j * LANES
        o_ref[0, :, c0:c0 + LANES] = v.astype(BF16)

    def rope(x0, x1, c, s):
        return x0 * c - x1 * s, x0 * s + x1 * c

    def head_norm(x0, x1, g_ref_):
        gs = _dot_hilo(x0 * x0 + x1 * x1, bd)
        r = lax.rsqrt(gs * (1.0 / HEAD_DIM) + EPS)
        return x0 * r * g_ref_[0:1, :], x1 * r * g_ref_[1:2, :]

    for m in range(4):
        q0, q1, k0, k1 = get(m, 0), get(m, 1), get(m, 2), get(m, 3)
        if m == 0:
            q0, q1 = head_norm(q0, q1, gq_ref)
            k0, k1 = head_norm(k0, k1, gk_ref)
        if m in (0, 2):
            q0, q1 = rope(q0, q1, ca, sa)
            k0, k1 = rope(k0, k1, ca, sa)
        elif m == 1:
            q0, q1 = rope(q0, q1, cb, sb)
            k0, k1 = rope(k0, k1, cb, sb)
        qscale = LOG2E * (DIFF_DIM if m == 1 else HEAD_DIM) ** -0.5
        put(m, 0, q0 * qscale)
        put(m, 1, q1 * qscale)
        put(m, 2, k0)
        put(m, 3, k1)
        put(m, 4, get(m, 4))
        put(m, 5, get(m, 5))


def _proj(stream, mod, gain, w_perm, ca, sa, cb, sb, gq, gk):
    lt = ca.shape[0]
    tm = _token_tile(lt)
    x_specs, x_args = _stream_specs(stream, 0, tm)
    b = x_args[0].shape[0]
    tab = pl.BlockSpec((tm, LANES), lambda b_, i: (i, 0))
    full = lambda a: pl.BlockSpec(a.shape, lambda b_, i: (0,) * a.ndim)
    return pl.pallas_call(
        _proj_body,
        grid=(b, lt // tm),
        in_specs=x_specs + _mod_specs(0) + _mod_specs(1) + [
            full(gain), full(w_perm), tab, tab, tab, tab, full(gq), full(gk)],
        out_specs=pl.BlockSpec((1, tm, QKV_W), lambda b_, i: (b_, i, 0)),
        out_shape=jax.ShapeDtypeStruct((b, lt, QKV_W), BF16),
        compiler_params=_cparams(2),
        name="proj",
    )(*x_args, mod, mod, mod, mod, gain, w_perm, ca, sa, cb, sb, gq, gk)


def _sum_lane(h):
    return (HEAD_DIM * (h + 1)) % MIXER_W


def _fill_head_values(v_ref, vm_ref):
    lane = lax.broadcasted_iota(jnp.int32, (1, MIXER_W), 1)
    v = v_ref[0]
    for h in range(N_HEADS):
        tap = jnp.where(lane == _sum_lane(h), 1.0, 0.0).astype(BF16)
        vm_ref[h] = jnp.where(lane // HEAD_DIM == h, v, tap)


def _attend(qm, kk, vm, bias=None, valid=None, extra_logit=None):
    s = _dot_nt(qm, kk)
    if bias is not None:
        n0 = s.shape[1] - bias.shape[1]
        s = jnp.concatenate([s[:, :n0], s[:, n0:] + bias], axis=1)
    if valid is not None:
        s = jnp.where(valid, s, NEG_INF)
    m = jnp.max(s, axis=-1, keepdims=True)
    if extra_logit is not None:
        m = jnp.maximum(m, extra_logit)
    return _dot(jnp.exp2(s - m).astype(BF16), vm), m


def _attn_body(*refs, kind, with_ctx, lam_init, n_ctx, n_lat):
    sub = ATTN_SCORE_ROWS[kind]
    n_sub = ATTN_STEP_ROWS[kind] // sub
    refs = list(refs)
    scalar_ref = refs.pop(0) if kind in ("diff", "window") else None
    q_ref, k_ref, v_ref = refs[:3]
    refs = refs[3:]
    gain_ref = refs.pop(0) if kind == "diff" else None
    bias_refs = [refs.pop(0) for _ in range(n_sub)] if kind == "nbr" else None
    o_ref, vm_ref = refs
    step = pl.program_id(1)
    lane = lax.broadcasted_iota(jnp.int32, (1, MIXER_W), 1)
    band = sub + 2 * WINDOW
    strip = NBR_STRIP_ROWS * GRID_W
    out0 = n_ctx if with_ctx else 0

    def rows_of(ref, ranges, *lead):
        parts = [ref[lead + (pl.ds(start, size), slice(None))] for start, size in ranges]
        return parts[0] if len(parts) == 1 else jnp.concatenate(parts, axis=0)

    def tile(q_row, n_rows, out_row, t=None, bias_ref=None):
        q = q_ref[0, pl.ds(q_row, n_rows), :]
        ranges, valid = [(0, n_ctx)], None
        if t is not None:
            if kind in ("global", "diff"):
                ranges = [(0, n_ctx + n_lat)]
            elif kind == "window":
                k_start = jnp.clip(t * sub - WINDOW, 0, n_lat - band)
                ranges.append((pl.multiple_of(n_ctx + k_start, LANES), band))
                col = lax.broadcasted_iota(jnp.int32, (1, n_ctx + band), 1)
                row = lax.broadcasted_iota(jnp.int32, (n_rows, 1), 0)
                dist = (k_start + col - n_ctx) - (t * sub + row)
                valid = jnp.abs(jnp.where(col < n_ctx, 0, dist)) <= WINDOW
            else:
                rs = _nbr_strip_start(t, n_lat // GRID_W)
                ranges.append((pl.multiple_of(n_ctx + rs * GRID_W, GRID_W), strip))
        kk = rows_of(k_ref, ranges, 0)
        out = jnp.zeros((n_rows, MIXER_W), F32)
        for h in range(N_HEADS):
            vm = rows_of(vm_ref, ranges, h)
            tap = _sum_lane(h)
            if kind == "diff":
                o = None
                for c in range(2):
                    qm = q * _lane_mask((lane % LANES) // (DIFF_DIM // 2) == 2 * h + c)
                    pv, _ = _attend(qm, kk, vm)
                    coef = 1.0 if c == 0 else scalar_ref[0]
                    term = pv * (coef / pv[:, tap:tap + 1])
                    o = term if c == 0 else o - term
            else:
                qm = q * _lane_mask((lane % LANES) // (HEAD_DIM // 2) == h)
                if kind == "window":
                    sink = scalar_ref[h] * LOG2E
                    pv, m = _attend(qm, kk, vm, valid=valid, extra_logit=sink)
                    o = pv * (1.0 / (pv[:, tap:tap + 1] + jnp.exp2(sink - m)))
                else:
                    bias = bias_ref[0, h, 0] if (kind == "nbr" and t is not None) else None
                    pv, _ = _attend(qm, kk, vm, bias=bias)
                    o = pv * (1.0 / pv[:, tap:tap + 1])
            out = jnp.where(lane // HEAD_DIM == h, o, out)
        if kind == "diff":
            gs = _dot_hilo(out * out, _block_diag_ones(MIXER_W, HEAD_DIM))
            out = out * lax.rsqrt(gs * (1.0 / HEAD_DIM) + EPS) * gain_ref[...] * (1.0 - lam_init)
        o_ref[0, pl.ds(out_row, n_rows), :] = out.astype(BF16)

    @pl.when(step == 0)
    def _():
        _fill_head_values(v_ref, vm_ref)
        if with_ctx:
            tile(0, n_ctx, 0)

    for j in range(n_sub):
        t = step * n_sub + j
        row = pl.multiple_of(t * sub, sub)
        tile(n_ctx + row, sub, out0 + row, t=t, bias_ref=bias_refs[j] if bias_refs else None)


def _attention(qkv, mixer, kind, with_ctx, *, scalars=None, gain=None, bias=None, layer=0, lam_init=0.0):
    b, lt, _ = qkv.shape
    n_lat = lt - TOK_TILE
    n_out = lt if with_ctx else n_lat
    col = lambda j: pl.BlockSpec((1, lt, MIXER_W), lambda b_, i: (b_, 0, 3 * mixer + j))
    in_specs, args = [col(0), col(1), col(2)], [qkv, qkv, qkv]
    if scalars is not None:
        in_specs, args = [pl.BlockSpec(memory_space=pltpu.SMEM)] + in_specs, [scalars] + args
    if gain is not None:
        in_specs.append(pl.BlockSpec(gain.shape, lambda b_, i: (0, 0)))
        args.append(gain)
    if bias is not None:
        n_sub = ATTN_STEP_ROWS[kind] // ATTN_SCORE_ROWS[kind]
        n_tiles = n_lat // TOK_TILE
        for j in range(n_sub):
            in_specs.append(pl.BlockSpec(
                (1, N_HEADS, 1) + bias.shape[3:],
                lambda b_, i, j=j: (layer, 0, _nbr_pattern(i * n_sub + j, n_tiles), 0, 0)))
            args.append(bias)
    return pl.pallas_call(
        functools.partial(_attn_body, kind=kind, with_ctx=with_ctx, lam_init=lam_init,
                          n_ctx=TOK_TILE, n_lat=n_lat),
        grid=(b, n_lat // ATTN_STEP_ROWS[kind]),
        in_specs=in_specs,
        out_specs=pl.BlockSpec((1, n_out, MIXER_W), lambda b_, i: (b_, 0, 0)),
        out_shape=jax.ShapeDtypeStruct((b, n_out, MIXER_W), BF16),
        scratch_shapes=[pltpu.VMEM((N_HEADS, lt, MIXER_W), BF16)],
        compiler_params=_cparams(2),
        name="attn_" + kind,
    )(*args)


def _outproj_body(oa_ref, ob_ref, oc_ref, od_ref, w_ref, gc_ref, gl_ref, g_ref,
                  shc_ref, shl_ref, scc_ref, scl_ref, g2_ref, wr_ref, *rest, row0):
    x_refs, (out_ref, h_ref, lg_ref) = rest[:-3], rest[-3:]
    acc = None
    for m, o_ref in enumerate((oa_ref, ob_ref, oc_ref, od_ref)):
        t = _dot(o_ref[0], w_ref[m * MIXER_W:(m + 1) * MIXER_W, :])
        acc = t if acc is None else acc + t
    tm = acc.shape[0]
    ms = jnp.mean(acc * acc, axis=-1, keepdims=True)
    y = (acc * lax.rsqrt(ms + EPS)) * g_ref[...]
    x_new = _stream_tile(x_refs, row0) + _tile_mod(gc_ref, gl_ref, tm, row0) * y
    out_ref[0] = x_new
    h = _norm_modulate(x_new, g2_ref[...], _tile_mod(shc_ref, shl_ref, tm, row0),
                       _tile_mod(scc_ref, scl_ref, tm, row0))
    h_ref[0] = h.astype(BF16)
    w_hi, w_lo = _split(wr_ref[...])
    h_hi, h_lo = _split(h)
    lg_ref[0] = _dot_nt(w_hi, h_hi) + (_dot_nt(w_hi, h_lo) + _dot_nt(w_lo, h_hi))


def _outproj(o_mix, w_out, stream, mod, gain, gain_ffn, w_router_t, row0):
    b, n_tok, _ = o_mix[0].shape
    tm = _token_tile(n_tok)
    x_specs, x_args = _stream_specs(stream, row0, tm)
    d = x_args[0].shape[-1]
    n_e = w_router_t.shape[0]
    o_spec = pl.BlockSpec((1, tm, MIXER_W), lambda b_, i: (b_, i, 0))
    full = lambda a: pl.BlockSpec(a.shape, lambda b_, i: (0, 0))
    tok_spec = pl.BlockSpec((1, tm, d), lambda b_, i: (b_, i, 0))
    return pl.pallas_call(
        functools.partial(_outproj_body, row0=row0),
        grid=(b, n_tok // tm),
        in_specs=[o_spec, o_spec, o_spec, o_spec, full(w_out)] + _mod_specs(2) + [full(gain)]
        + _mod_specs(3) + _mod_specs(4) + [full(gain_ffn), full(w_router_t)] + x_specs,
        out_specs=[tok_spec, tok_spec, pl.BlockSpec((1, n_e, tm), lambda b_, i: (b_, 0, i))],
        out_shape=[jax.ShapeDtypeStruct((b, n_tok, d), F32),
                   jax.ShapeDtypeStruct((b, n_tok, d), BF16),
                   jax.ShapeDtypeStruct((b, n_e, n_tok), F32)],
        compiler_params=_cparams(2),
        name="outproj",
    )(*o_mix, w_out, mod, mod, gain, mod, mod, mod, mod, gain_ffn, w_router_t, *x_args)


def _exclusive_cumsum(x, tri):
    off = jnp.zeros((x.shape[0], 1), F32)
    outs = []
    for c in range(x.shape[1] // LANES):
        xc = x[:, c * LANES:(c + 1) * LANES]
        inc = _dot(xc.astype(BF16), tri)
        outs.append(inc - xc + off)
        off = off + inc[:, LANES - 1:LANES]
    return jnp.concatenate(outs, axis=1)


def _select_body(lg_ref, pos_ref, aff_ref, *, segments):
    r = lax.broadcasted_iota(jnp.int32, (LANES, LANES), 0)
    c = lax.broadcasted_iota(jnp.int32, (LANES, LANES), 1)
    tri = jnp.where(r <= c, 1.0, 0.0).astype(BF16)
    n_b, n_e = lg_ref.shape[:2]
    for t0, t, cap, slot0 in segments:
        affs = []
        for j in range(n_b):
            lg = lg_ref[j, :, t0:t0 + t]
            e = jnp.exp(lg - jnp.max(lg, axis=0, keepdims=True))
            affs.append(e / jnp.sum(e, axis=0, keepdims=True))
        aff = jnp.concatenate(affs, axis=0)

        def count_above(thr):
            return jnp.sum(jnp.where(aff > thr, 1.0, 0.0), axis=1, keepdims=True)

        def unsettled(carry):
            return jnp.logical_and(carry[2] > 0.0, carry[3] < BISECT_MAX_STEPS)

        def bisect(carry):
            lo, hi, _, step = carry
            mid = 0.5 * (lo + hi)
            cnt = count_above(mid)
            new_lo = jnp.where(cnt >= cap, mid, lo)
            new_hi = jnp.where(cnt <= cap, mid, hi)
            moving = jnp.logical_and(new_lo < new_hi, jnp.logical_and(mid > lo, mid < hi))
            return new_lo, new_hi, jnp.sum(jnp.where(moving, 1.0, 0.0)), step + 1

        lo0 = jnp.full((aff.shape[0], 1), -1.0, F32)
        hi0 = jnp.max(aff, axis=1, keepdims=True)
        lo, hi, _, _ = lax.while_loop(unsettled, bisect, (lo0, hi0, jnp.float32(1.0), jnp.int32(0)))
        gt = aff > hi
        eq = jnp.logical_and(aff > lo, aff <= hi)
        need = cap - count_above(hi)
        eq_rank = _exclusive_cumsum(jnp.where(eq, 1.0, 0.0), tri)
        sel = jnp.logical_or(gt, jnp.logical_and(eq, eq_rank < need))
        slot = _exclusive_cumsum(jnp.where(sel, 1.0, 0.0), tri) + slot0
        pos = jnp.where(sel, slot, -1.0).astype(jnp.int32)
        for j in range(n_b):
            pos_ref[j, :, t0:t0 + t] = pos[j * n_e:(j + 1) * n_e]
            aff_ref[j, :, t0:t0 + t] = affs[j]


def _select(logits, segments):
    b, n_e, lt = logits.shape
    spec = pl.BlockSpec((b, n_e, lt), lambda i: (0, 0, 0))
    return pl.pallas_call(
        functools.partial(_select_body, segments=segments),
        grid=(1,),
        in_specs=[spec],
        out_specs=[spec, spec],
        out_shape=[jax.ShapeDtypeStruct((b, n_e, lt), jnp.int32),
                   jax.ShapeDtypeStruct((b, n_e, lt), F32)],
        compiler_params=_cparams(1),
        name="select",
    )(logits)


def _slot_hits(pos_row, n_slots):
    return lax.broadcasted_iota(jnp.int32, (n_slots, pos_row.shape[1]), 0) == pos_row


def _gather_body(pos_ref, h_ref, xe_ref):
    n_grp, n_slots = xe_ref.shape[1:3]
    e0 = pl.program_id(1) * n_grp
    onehot = jnp.concatenate(
        [jnp.where(_slot_hits(pos_ref[0, pl.ds(e0 + j, 1), :], n_slots), 1.0, 0.0).astype(BF16)
         for j in range(n_grp)], axis=0)
    xe = _dot(onehot, h_ref[0]).astype(BF16)
    xe_ref[0] = xe.reshape(n_grp, n_slots, xe.shape[1])


def _gather(pos, h, n_slots):
    b, n_e, n_tok = pos.shape
    d = h.shape[2]
    return pl.pallas_call(
        _gather_body,
        grid=(b, n_e // GATHER_EXPERTS),
        in_specs=[pl.BlockSpec((1, n_e, n_tok), lambda b_, g: (b_, 0, 0)),
                  pl.BlockSpec((1, n_tok, d), lambda b_, g: (b_, 0, 0))],
        out_specs=pl.BlockSpec((1, GATHER_EXPERTS, n_slots, d), lambda b_, g: (b_, g, 0, 0)),
        out_shape=jax.ShapeDtypeStruct((b, n_e, n_slots, d), BF16),
        compiler_params=_cparams(2),
        name="gather",
    )(pos, h)


def _experts_body(pos_ref, aff_ref, xe_ref, wg_ref, wu_ref, wd_ref, y_ref, wg_s, wu_s, wd_s):
    @pl.when(pl.program_id(1) == 0)
    def _():
        wg_s[...] = wg_ref[0, 0].astype(BF16)
        wu_s[...] = wu_ref[0, 0].astype(BF16)
        wd_s[...] = wd_ref[0, 0].astype(BF16)

    n_b, _, n_slots, d = xe_ref.shape
    e = pl.program_id(0)
    x = xe_ref[...].reshape(n_b * n_slots, d)
    gate = jnp.concatenate(
        [jnp.sum(jnp.where(_slot_hits(pos_ref[j, pl.ds(e, 1), :], n_slots),
                           aff_ref[j, pl.ds(e, 1), :], 0.0), axis=1, keepdims=True)
         for j in range(n_b)], axis=0)
    y = None
    for c0 in range(0, wg_s.shape[1], FFN_CHUNK):
        a = _dot(x, wg_s[:, c0:c0 + FFN_CHUNK])
        u = _dot(x, wu_s[:, c0:c0 + FFN_CHUNK])
        act = ((a * jax.nn.sigmoid(a)) * u).astype(BF16)
        t = _dot(act, wd_s[c0:c0 + FFN_CHUNK, :])
        y = t if y is None else y + t
    y_ref[...] = (y * gate).astype(BF16).reshape(n_b, 1, n_slots, d)


def _experts(pos, aff, xe, w_gate, w_up, w_down, layer):
    b, n_e, n_tok = pos.shape
    n_slots = xe.shape[2]
    d, f = w_gate.shape[2:]
    nb = FFN_BATCHES if b % FFN_BATCHES == 0 else 1
    row_spec = pl.BlockSpec((nb, n_e, n_tok), lambda e, g: (g, 0, 0))
    slot_spec = pl.BlockSpec((nb, 1, n_slots, d), lambda e, g: (g, e, 0, 0))
    return pl.pallas_call(
        _experts_body,
        grid=(n_e, b // nb),
        in_specs=[row_spec, row_spec, slot_spec,
                  pl.BlockSpec((1, 1, d, f), lambda e, g: (layer, e, 0, 0)),
                  pl.BlockSpec((1, 1, d, f), lambda e, g: (layer, e, 0, 0)),
                  pl.BlockSpec((1, 1, f, d), lambda e, g: (layer, e, 0, 0))],
        out_specs=slot_spec,
        out_shape=jax.ShapeDtypeStruct(xe.shape, BF16),
        scratch_shapes=[pltpu.VMEM((d, f), BF16), pltpu.VMEM((d, f), BF16), pltpu.VMEM((f, d), BF16)],
        compiler_params=_cparams(2),
        name="experts",
    )(pos, aff, xe, w_gate, w_up, w_down)


def _combine_body(pos_ref, y_ref, x_ref, gc_ref, gl_ref, g_ref, out_ref, *, row0):
    n_e, n_slots = y_ref.shape[1:3]
    pos = pos_ref[0]
    slot = lax.broadcasted_iota(jnp.int32, (n_slots, pos.shape[1]), 0)
    onehot = jnp.concatenate(
        [jnp.where(slot == pos[e:e + 1, :], 1.0, 0.0).astype(BF16) for e in range(n_e)], axis=0)
    acc = lax.dot_general(onehot, y_ref[0].reshape(n_e * n_slots, y_ref.shape[3]),
                          (((0,), (0,)), ((), ())), preferred_element_type=F32)
    ms = jnp.mean(acc * acc, axis=-1, keepdims=True)
    y = (acc * lax.rsqrt(ms + EPS)) * g_ref[...]
    out_ref[0] = x_ref[0] + _tile_mod(gc_ref, gl_ref, acc.shape[0], row0) * y


def _combine(pos, y, xs, mod, gain, row0):
    b, n_tok, d = xs.shape
    tm = _token_tile(n_tok)
    n_e, n_slots = y.shape[1:3]
    return pl.pallas_call(
        functools.partial(_combine_body, row0=row0),
        grid=(b, n_tok // tm),
        in_specs=[pl.BlockSpec((1, n_e, tm), lambda b_, i: (b_, 0, i)),
                  pl.BlockSpec((1, n_e, n_slots, d), lambda b_, i: (b_, 0, 0, 0)),
                  pl.BlockSpec((1, tm, d), lambda b_, i: (b_, i, 0))] + _mod_specs(5) + [
                  pl.BlockSpec(gain.shape, lambda b_, i: (0, 0))],
        out_specs=pl.BlockSpec((1, tm, d), lambda b_, i: (b_, i, 0)),
        out_shape=jax.ShapeDtypeStruct(xs.shape, F32),
        compiler_params=_cparams(2),
        name="combine",
    )(pos, y, xs, mod, mod, gain)


def kernel(x, c, ctx, c_ctx, w_ada, b_ada, g_pre_mix, g_post_mix, g_pre_ffn, g_post_ffn, w_in, w_out, q_gain_a, k_gain_a, lam_q1, lam_k1, lam_q2, lam_k2, subln_gain_b, sink_c, rpb_d, w_router, w_gate, w_up, w_down):
    b, n_lat, d = x.shape
    n_ctx = ctx.shape[1]
    depth = w_ada.shape[0]
    assert n_ctx == TOK_TILE and n_lat % max(ATTN_STEP_ROWS.values()) == 0 and d == D_MODEL and b + 1 <= 16

    stream = (ctx, x)
    cc = jnp.zeros((16, d), F32).at[:b].set(c).at[b].set(c_ctx)
    mod_all = _ada(cc, w_ada, b_ada)
    mod_all = jnp.stack([jnp.broadcast_to(mod_all[:, b:b + 1], (depth, b, 6 * d)), mod_all[:, :b]],
                        axis=2).reshape(depth, b, 2, 6, 1, d)
    ca, sa = _rope_lane_tables(n_ctx, n_lat, HEAD_DIM)
    cb, sb = _rope_lane_tables(n_ctx, n_lat, DIFF_DIM)
    w_perm_all = _permute_w_in(w_in)
    w_out_all = w_out.astype(BF16)
    w_router_t = jnp.swapaxes(w_router, 1, 2)
    nbr_bias = _neighbourhood_bias(rpb_d, n_lat // GRID_W)
    pair_gain = lambda g: jnp.stack([jnp.tile(g[:, 0::2], (1, 4)), jnp.tile(g[:, 1::2], (1, 4))], axis=1)
    gq_all, gk_all = pair_gain(q_gain_a), pair_gain(k_gain_a)
    row = lambda v: v.reshape(1, -1)

    for l in range(depth):
        with_ctx = l < depth - 1
        row0 = 0 if with_ctx else 1
        lam_init = 0.8 - 0.6 * math.exp(-0.3 * l)
        mod = mod_all[l]
        qkv = _proj(stream, mod, row(g_pre_mix[l]), w_perm_all[l], ca, sa, cb, sb, gq_all[l], gk_all[l])

        lam = (jnp.exp(jnp.sum(lam_q1[l] * lam_k1[l])) - jnp.exp(jnp.sum(lam_q2[l] * lam_k2[l]))
               + lam_init).reshape(1)
        o_mix = (
            _attention(qkv, 0, "global", with_ctx),
            _attention(qkv, 1, "diff", with_ctx, scalars=lam, lam_init=lam_init,
                       gain=row(jnp.tile(subln_gain_b[l], N_HEADS))),
            _attention(qkv, 2, "window", with_ctx, scalars=sink_c[l]),
            _attention(qkv, 3, "nbr", with_ctx, bias=nbr_bias, layer=l),
        )
        xs, h, logits = _outproj(o_mix, w_out_all[l], stream, mod, row(g_post_mix[l]),
                                 row(g_pre_ffn[l]), w_router_t[l], row0)
        cap_lat = CAPACITY_FACTOR * n_lat // N_EXPERTS
        cap_ctx = CAPACITY_FACTOR * n_ctx // N_EXPERTS
        if with_ctx:
            segments = ((0, n_ctx, cap_ctx, 0), (n_ctx, n_lat, cap_lat, cap_ctx))
            n_slots = cap_ctx + cap_lat
        else:
            segments = ((0, n_lat, cap_lat, 0),)
            n_slots = cap_lat
        pos, aff = _select(logits, segments)
        y = _experts(pos, aff, _gather(pos, h, n_slots), w_gate, w_up, w_down, l)
        stream = _combine(pos, y, xs, mod, row(g_post_ffn[l]), row0)
    return stream
```

```python
import functools
import math

import numpy as np
import jax
import jax.numpy as jnp
from jax import lax
from jax.experimental import pallas as pl
from jax.experimental.pallas import tpu as pltpu

F32 = jnp.float32
BF16 = jnp.bfloat16

D_MODEL = 1024
GRID_W = 64
HEAD_DIM = 64
DIFF_DIM = 32
N_HEADS = 4
MIXER_W = N_HEADS * HEAD_DIM
QKV_W = 4 * 3 * MIXER_W
WINDOW = 128
NA_ROWS = 8
NA_COLS = 16
ROPE_BASE = 10000.0
N_EXPERTS = 16
CAPACITY_FACTOR = 2
EPS = 1e-6
NEG_INF = -1e30
LOG2E = 1.4426950408889634
PROJ_WIDTHS = (256, 128, 128, 256, 256, 256, 256, 128, 128, 256, 256, 256)
LANES = 128
TOK_TILE = 256
VMEM_LIMIT = 56 * 1024 * 1024
ATTN_SCORE_ROWS = {"global": 512, "diff": 512, "window": 256, "nbr": 256}
ATTN_STEP_ROWS = {"global": 2048, "diff": 1024, "window": 2048, "nbr": 1024}
NBR_TILE_ROWS = TOK_TILE // GRID_W
NBR_STRIP_ROWS = 12
GATHER_EXPERTS = 4
FFN_BATCHES = 4
FFN_CHUNK = 512
BISECT_MAX_STEPS = 192


def _cparams(n_axes):
    return pltpu.CompilerParams(dimension_semantics=("arbitrary",) * n_axes,
                                vmem_limit_bytes=VMEM_LIMIT)


def _proj_source_columns():
    offs = np.concatenate([[0], np.cumsum(PROJ_WIDTHS)])

    def pairs(base, n_slots, slot_src, n_pairs):
        x0 = [base + slot_src(j) + 2 * i for j in range(n_slots) for i in range(n_pairs)]
        return x0 + [c + 1 for c in x0]

    cols = []
    for m in range(4):
        q0, k0, v0 = offs[3 * m], offs[3 * m + 1], offs[3 * m + 2]
        if m in (0, 2):
            cols += pairs(q0, 4, lambda j: j * HEAD_DIM, 32)
            cols += pairs(k0, 4, lambda j: (j // 2) * HEAD_DIM, 32)
            cols += [v0 + (j // 2) * HEAD_DIM + d for j in range(4) for d in range(HEAD_DIM)]
        elif m == 1:
            cols += pairs(q0, 8, lambda j: j * DIFF_DIM, 16)
            cols += pairs(k0, 8, lambda j: j * DIFF_DIM, 16)
            cols += [v0 + c for c in range(MIXER_W)]
        else:
            cols += pairs(q0, 4, lambda j: j * HEAD_DIM, 32)
            cols += pairs(k0, 4, lambda j: j * HEAD_DIM, 32)
            cols += [v0 + c for c in range(MIXER_W)]
    return np.asarray(cols, dtype=np.int32)


def _permute_w_in(w_in):
    src = _proj_source_columns()
    pick = (jnp.arange(w_in.shape[-1], dtype=jnp.int32)[:, None] == src[None, :]).astype(BF16)
    return jnp.einsum('ldk,kn->ldn', w_in.astype(BF16), pick, preferred_element_type=BF16)


def _rope_tables(n_tokens, dim):
    t = jnp.arange(n_tokens, dtype=jnp.int32)
    rows = (t // GRID_W).astype(F32)
    cols = (t % GRID_W).astype(F32)
    n_axis = dim // 4
    inv_freq = ROPE_BASE ** (-jnp.arange(n_axis, dtype=F32) / n_axis)
    ang = jnp.concatenate([rows[:, None] * inv_freq, cols[:, None] * inv_freq], axis=-1)
    return jnp.cos(ang), jnp.sin(ang)


def _rope_lane_tables(n_ctx, n_lat, dim):
    cos, sin = _rope_tables(n_lat, dim)
    reps = LANES // cos.shape[1]
    cos = jnp.concatenate([jnp.ones((n_ctx, LANES), F32), jnp.tile(cos, (1, reps))], axis=0)
    sin = jnp.concatenate([jnp.zeros((n_ctx, LANES), F32), jnp.tile(sin, (1, reps))], axis=0)
    return cos, sin


def _nbr_strip_start(tile, n_rows):
    lo = tile * NBR_TILE_ROWS - NA_ROWS // 2
    return jnp.clip(lo, 0, n_rows - NBR_STRIP_ROWS) if isinstance(lo, jax.Array) else int(
        np.clip(lo, 0, n_rows - NBR_STRIP_ROWS))


def _nbr_pattern(tile, n_tiles):
    if isinstance(tile, jax.Array):
        return jnp.where(tile == 0, 0, jnp.where(tile == n_tiles - 1, 2, 1))
    return 0 if tile == 0 else (2 if tile == n_tiles - 1 else 1)


def _neighbourhood_bias(rpb, n_rows):
    n_tiles = n_rows // NBR_TILE_ROWS
    assert n_tiles >= 3 and n_rows >= NBR_STRIP_ROWS

    def rows_of(tile):
        ss = _nbr_strip_start(tile, n_rows)
        r = tile * NBR_TILE_ROWS + np.arange(NBR_TILE_ROWS)[:, None]
        rs = np.clip(r - NA_ROWS // 2, 0, n_rows - NA_ROWS)
        kr = ss + np.arange(NBR_STRIP_ROWS)[None, :]
        valid = (kr >= rs) & (kr < rs + NA_ROWS)
        return valid, np.where(valid, kr - r + NA_ROWS - 1, 0)

    reps = [rows_of(t) for t in (0, 1, n_tiles - 1)]
    for t in range(n_tiles):
        v, ri = rows_of(t)
        assert np.array_equal(v, reps[_nbr_pattern(t, n_tiles)][0])
        assert np.array_equal(ri, reps[_nbr_pattern(t, n_tiles)][1])
    n_r, n_c = 2 * NA_ROWS - 1, 2 * NA_COLS - 1
    row_valid = np.stack([v for v, _ in reps])
    ri = np.where(row_valid, np.stack([r for _, r in reps]), n_r)
    oh_r = ri[..., None] == np.arange(n_r + 1)
    cq = np.arange(GRID_W)
    col_start = np.clip(cq - NA_COLS // 2, 0, GRID_W - NA_COLS)
    col_valid = (cq[None, :] >= col_start[:, None]) & (cq[None, :] < col_start[:, None] + NA_COLS)
    ci = np.where(col_valid, np.clip(cq[None, :] - cq[:, None] + NA_COLS - 1, 0, n_c - 1), n_c)
    oh_c = np.arange(n_c + 1)[:, None, None] == ci[None]
    table = jnp.pad(rpb.astype(F32) * LOG2E, ((0, 0), (0, 0), (0, 1), (0, 1)), constant_values=NEG_INF)
    bias = jnp.einsum('lhrc,pajr,cqw->lhpaqjw', table, oh_r.astype(np.float32),
                      oh_c.astype(np.float32), precision=lax.Precision.HIGHEST)
    return bias.reshape(bias.shape[:2] + (3, TOK_TILE, NBR_STRIP_ROWS * GRID_W))


def _block_diag_ones(n, group):
    r = lax.broadcasted_iota(jnp.int32, (n, n), 0) // group
    c = lax.broadcasted_iota(jnp.int32, (n, n), 1) // group
    return jnp.where(r == c, 1.0, 0.0).astype(BF16)


def _dot(a, b):
    return jnp.dot(a, b, preferred_element_type=F32)


def _dot_nt(a, b):
    return lax.dot_general(a, b, (((1,), (1,)), ((), ())), preferred_element_type=F32)


def _split(a):
    hi = a.astype(BF16)
    return hi, (a - hi.astype(F32)).astype(BF16)


def _dot_hilo(a, b_bf16):
    hi, lo = _split(a)
    return _dot(hi, b_bf16) + _dot(lo, b_bf16)


def _lane_mask(cond):
    return jnp.where(cond, 1.0, 0.0).astype(BF16)


def _norm_modulate(x, gain, shift, scale):
    ms = jnp.mean(x * x, axis=-1, keepdims=True)
    h = (x * lax.rsqrt(ms + EPS)) * gain
    return h * (1.0 + scale) + shift


def _token_tile(n_tok):
    return next(t for t in (768, 512, TOK_TILE) if n_tok % t == 0)


def _mod_specs(chunk):
    return [pl.BlockSpec((1, 1, 1, 1, D_MODEL), lambda b, i, r=r: (b, r, chunk, 0, 0)) for r in (0, 1)]


def _tile_mod(mc_ref, ml_ref, n_rows, row0):
    lat = ml_ref[0, 0, 0]
    if row0 > 0:
        return lat
    row = lax.broadcasted_iota(jnp.int32, (n_rows, 1), 0)
    is_ctx = jnp.logical_and(row < TOK_TILE, pl.program_id(1) == 0)
    return jnp.where(is_ctx, mc_ref[0, 0, 0], lat)


def _ada_body(c_ref, w_ref, b_ref, o_ref):
    c = c_ref[...]
    cs = (c * jax.nn.sigmoid(c)).astype(BF16)
    o_ref[0] = _dot(cs, w_ref[0].astype(BF16)) + b_ref[0]


def _ada(cc, w_ada, b_ada):
    depth, d, n = w_ada.shape
    tn = 512
    return pl.pallas_call(
        _ada_body,
        grid=(depth, n // tn),
        in_specs=[pl.BlockSpec(cc.shape, lambda l, j: (0, 0)),
                  pl.BlockSpec((1, d, tn), lambda l, j: (l, 0, j)),
                  pl.BlockSpec((1, 1, tn), lambda l, j: (l, 0, j))],
        out_specs=pl.BlockSpec((1, cc.shape[0], tn), lambda l, j: (l, 0, j)),
        out_shape=jax.ShapeDtypeStruct((depth, cc.shape[0], n), F32),
        compiler_params=_cparams(2),
        name="ada",
    )(cc, w_ada, b_ada.reshape(depth, 1, n))


def _stream_specs(stream, row0, tm):
    ctx_arr, lat_arr = stream if isinstance(stream, tuple) else (stream, stream)
    lat_tile0 = 0 if isinstance(stream, tuple) else 1
    k = tm // TOK_TILE
    blk = (1, TOK_TILE, lat_arr.shape[-1])
    specs = [pl.BlockSpec(blk, lambda b_, i, j=j: (b_, jnp.maximum(i * k + row0 + j - 1, 0) + lat_tile0, 0))
             for j in range(k)]
    args = [lat_arr] * k
    if row0 == 0:
        specs, args = [pl.BlockSpec(blk, lambda b_, i: (b_, 0, 0))] + specs, [ctx_arr] + args
    return specs, args


def _stream_tile(x_refs, row0):
    parts = [r[0] for r in x_refs]
    if row0 == 0:
        parts = [jnp.where(pl.program_id(1) == 0, parts[0], parts[1])] + parts[2:]
    return parts[0] if len(parts) == 1 else jnp.concatenate(parts, axis=0)


def _proj_body(*refs):
    (shc_ref, shl_ref, scc_ref, scl_ref, g_ref, w_ref, ca_ref, sa_ref, cb_ref, sb_ref, gq_ref, gk_ref,
     o_ref) = refs[-13:]
    x = _stream_tile(refs[:-13], 0)
    tm = x.shape[0]
    h = _norm_modulate(x, g_ref[...], _tile_mod(shc_ref, shl_ref, tm, 0), _tile_mod(scc_ref, scl_ref, tm, 0))
    hb = h.astype(BF16)
    accs = [_dot(hb, w_ref[:, 3 * m * MIXER_W:3 * (m + 1) * MIXER_W]) for m in range(4)]
    ca, sa, cb, sb = ca_ref[...], sa_ref[...], cb_ref[...], sb_ref[...]
    bd = _block_diag_ones(LANES, 32)

    def get(m, j):
        return accs[m][:, j * LANES:(j + 1) * LANES]

    def put(m, j, v):
        c0 = (3 * m) * MIXER_W + j * LANES
        o_ref[0, :, c0:c0 + LANES] = v.astype(BF16)

    def rope(x0, x1, c, s):
        return x0 * c - x1 * s, x0 * s + x1 * c

    def head_norm(x0, x1, g_ref_):
        gs = _dot_hilo(x0 * x0 + x1 * x1, bd)
        r = lax.rsqrt(gs * (1.0 / HEAD_DIM) + EPS)
        return x0 * r * g_ref_[0:1, :], x1 * r * g_ref_[1:2, :]

    for m in range(4):
        q0, q1, k0, k1 = get(m, 0), get(m, 1), get(m, 2), get(m, 3)
        if m == 0:
            q0, q1 = head_norm(q0, q1, gq_ref)
            k0, k1 = head_norm(k0, k1, gk_ref)
        if m in (0, 2):
            q0, q1 = rope(q0, q1, ca, sa)
            k0, k1 = rope(k0, k1, ca, sa)
        elif m == 1:
            q0, q1 = rope(q0, q1, cb, sb)
            k0, k1 = rope(k0, k1, cb, sb)
        qscale = LOG2E * (DIFF_DIM if m == 1 else HEAD_DIM) ** -0.5
        put(m, 0, q0 * qscale)
        put(m, 1, q1 * qscale)
        put(m, 2, k0)
        put(m, 3, k1)
        put(m, 4, get(m, 4))
        put(m, 5, get(m, 5))


def _proj(stream, mod, gain, w_perm, ca, sa, cb, sb, gq, gk):
    lt = ca.shape[0]
    tm = _token_tile(lt)
    x_specs, x_args = _stream_specs(stream, 0, tm)
    b = x_args[0].shape[0]
    tab = pl.BlockSpec((tm, LANES), lambda b_, i: (i, 0))
    full = lambda a: pl.BlockSpec(a.shape, lambda b_, i: (0,) * a.ndim)
    return pl.pallas_call(
        _proj_body,
        grid=(b, lt // tm),
        in_specs=x_specs + _mod_specs(0) + _mod_specs(1) + [
            full(gain), full(w_perm), tab, tab, tab, tab, full(gq), full(gk)],
        out_specs=pl.BlockSpec((1, tm, QKV_W), lambda b_, i: (b_, i, 0)),
        out_shape=jax.ShapeDtypeStruct((b, lt, QKV_W), BF16),
        compiler_params=_cparams(2),
        name="proj",
    )(*x_args, mod, mod, mod, mod, gain, w_perm, ca, sa, cb, sb, gq, gk)


def _sum_lane(h):
    return (HEAD_DIM * (h + 1)) % MIXER_W


def _fill_head_values(v_ref, vm_ref):
    lane = lax.broadcasted_iota(jnp.int32, (1, MIXER_W), 1)
    v = v_ref[0]
    for h in range(N_HEADS):
        tap = jnp.where(lane == _sum_lane(h), 1.0, 0.0).astype(BF16)
        vm_ref[h] = jnp.where(lane // HEAD_DIM == h, v, tap)


def _attend(qm, kk, vm, bias=None, valid=None, extra_logit=None):
    s = _dot_nt(qm, kk)
    if bias is not None:
        n0 = s.shape[1] - bias.shape[1]
        s = jnp.concatenate([s[:, :n0], s[:, n0:] + bias], axis=1)
    if valid is not None:
        s = jnp.where(valid, s, NEG_INF)
    m = jnp.max(s, axis=-1, keepdims=True)
    if extra_logit is not None:
        m = jnp.maximum(m, extra_logit)
    return _dot(jnp.exp2(s - m).astype(BF16), vm), m


def _attn_body(*refs, kind, with_ctx, lam_init, n_ctx, n_lat):
    sub = ATTN_SCORE_ROWS[kind]
    n_sub = ATTN_STEP_ROWS[kind] // sub
    refs = list(refs)
    scalar_ref = refs.pop(0) if kind in ("diff", "window") else None
    q_ref, k_ref, v_ref = refs[:3]
    refs = refs[3:]
    gain_ref = refs.pop(0) if kind == "diff" else None
    bias_refs = [refs.pop(0) for _ in range(n_sub)] if kind == "nbr" else None
    o_ref, vm_ref = refs
    step = pl.program_id(1)
    lane = lax.broadcasted_iota(jnp.int32, (1, MIXER_W), 1)
    band = sub + 2 * WINDOW
    strip = NBR_STRIP_ROWS * GRID_W
    out0 = n_ctx if with_ctx else 0

    def rows_of(ref, ranges, *lead):
        parts = [ref[lead + (pl.ds(start, size), slice(None))] for start, size in ranges]
        return parts[0] if len(parts) == 1 else jnp.concatenate(parts, axis=0)

    def tile(q_row, n_rows, out_row, t=None, bias_ref=None):
        q = q_ref[0, pl.ds(q_row, n_rows), :]
        ranges, valid = [(0, n_ctx)], None
        if t is not None:
            if kind in ("global", "diff"):
                ranges = [(0, n_ctx + n_lat)]
            elif kind == "window":
                k_start = jnp.clip(t * sub - WINDOW, 0, n_lat - band)
                ranges.append((pl.multiple_of(n_ctx + k_start, LANES), band))
                col = lax.broadcasted_iota(jnp.int32, (1, n_ctx + band), 1)
                row = lax.broadcasted_iota(jnp.int32, (n_rows, 1), 0)
                dist = (k_start + col - n_ctx) - (t * sub + row)
                valid = jnp.abs(jnp.where(col < n_ctx, 0, dist)) <= WINDOW
            else:
                rs = _nbr_strip_start(t, n_lat // GRID_W)
                ranges.append((pl.multiple_of(n_ctx + rs * GRID_W, GRID_W), strip))
        kk = rows_of(k_ref, ranges, 0)
        out = jnp.zeros((n_rows, MIXER_W), F32)
        for h in range(N_HEADS):
            vm = rows_of(vm_ref, ranges, h)
            tap = _sum_lane(h)
            if kind == "diff":
                o = None
                for c in range(2):
                    qm = q * _lane_mask((lane % LANES) // (DIFF_DIM // 2) == 2 * h + c)
                    pv, _ = _attend(qm, kk, vm)
                    coef = 1.0 if c == 0 else scalar_ref[0]
                    term = pv * (coef / pv[:, tap:tap + 1])
                    o = term if c == 0 else o - term
            else:
                qm = q * _lane_mask((lane % LANES) // (HEAD_DIM // 2) == h)
                if kind == "window":
                    sink = scalar_ref[h] * LOG2E
                    pv, m = _attend(qm, kk, vm, valid=valid, extra_logit=sink)
                    o = pv * (1.0 / (pv[:, tap:tap + 1] + jnp.exp2(sink - m)))
                else:
                    bias = bias_ref[0, h, 0] if (kind == "nbr" and t is not None) else None
                    pv, _ = _attend(qm, kk, vm, bias=bias)
                    o = pv * (1.0 / pv[:, tap:tap + 1])
            out = jnp.where(lane // HEAD_DIM == h, o, out)
        if kind == "diff":
            gs = _dot_hilo(out * out, _block_diag_ones(MIXER_W, HEAD_DIM))
            out = out * lax.rsqrt(gs * (1.0 / HEAD_DIM) + EPS) * gain_ref[...] * (1.0 - lam_init)
        o_ref[0, pl.ds(out_row, n_rows), :] = out.astype(BF16)

    @pl.when(step == 0)
    def _():
        _fill_head_values(v_ref, vm_ref)
        if with_ctx:
            tile(0, n_ctx, 0)

    for j in range(n_sub):
        t = step * n_sub + j
        row = pl.multiple_of(t * sub, sub)
        tile(n_ctx + row, sub, out0 + row, t=t, bias_ref=bias_refs[j] if bias_refs else None)


def _attention(qkv, mixer, kind, with_ctx, *, scalars=None, gain=None, bias=None, layer=0, lam_init=0.0):
    b, lt, _ = qkv.shape
    n_lat = lt - TOK_TILE
    n_out = lt if with_ctx else n_lat
    col = lambda j: pl.BlockSpec((1, lt, MIXER_W), lambda b_, i: (b_, 0, 3 * mixer + j))
    in_specs, args = [col(0), col(1), col(2)], [qkv, qkv, qkv]
    if scalars is not None:
        in_specs, args = [pl.BlockSpec(memory_space=pltpu.SMEM)] + in_specs, [scalars] + args
    if gain is not None:
        in_specs.append(pl.BlockSpec(gain.shape, lambda b_, i: (0, 0)))
        args.append(gain)
    if bias is not None:
        n_sub = ATTN_STEP_ROWS[kind] // ATTN_SCORE_ROWS[kind]
        n_tiles = n_lat // TOK_TILE
        for j in range(n_sub):
            in_specs.append(pl.BlockSpec(
                (1, N_HEADS, 1) + bias.shape[3:],
                lambda b_, i, j=j: (layer, 0, _nbr_pattern(i * n_sub + j, n_tiles), 0, 0)))
            args.append(bias)
    return pl.pallas_call(
        functools.partial(_attn_body, kind=kind, with_ctx=with_ctx, lam_init=lam_init,
                          n_ctx=TOK_TILE, n_lat=n_lat),
        grid=(b, n_lat // ATTN_STEP_ROWS[kind]),
        in_specs=in_specs,
        out_specs=pl.BlockSpec((1, n_out, MIXER_W), lambda b_, i: (b_, 0, 0)),
        out_shape=jax.ShapeDtypeStruct((b, n_out, MIXER_W), BF16),
        scratch_shapes=[pltpu.VMEM((N_HEADS, lt, MIXER_W), BF16)],
        compiler_params=_cparams(2),
        name="attn_" + kind,
    )(*args)


def _outproj_body(oa_ref, ob_ref, oc_ref, od_ref, w_ref, gc_ref, gl_ref, g_ref,
                  shc_ref, shl_ref, scc_ref, scl_ref, g2_ref, wr_ref, *rest, row0):
    x_refs, (out_ref, h_ref, lg_ref) = rest[:-3], rest[-3:]
    acc = None
    for m, o_ref in enumerate((oa_ref, ob_ref, oc_ref, od_ref)):
        t = _dot(o_ref[0], w_ref[m * MIXER_W:(m + 1) * MIXER_W, :])
        acc = t if acc is None else acc + t
    tm = acc.shape[0]
    ms = jnp.mean(acc * acc, axis=-1, keepdims=True)
    y = (acc * lax.rsqrt(ms + EPS)) * g_ref[...]
    x_new = _stream_tile(x_refs, row0) + _tile_mod(gc_ref, gl_ref, tm, row0) * y
    out_ref[0] = x_new
    h = _norm_modulate(x_new, g2_ref[...], _tile_mod(shc_ref, shl_ref, tm, row0),
                       _tile_mod(scc_ref, scl_ref, tm, row0))
    h_ref[0] = h.astype(BF16)
    w_hi, w_lo = _split(wr_ref[...])
    h_hi, h_lo = _split(h)
    lg_ref[0] = _dot_nt(w_hi, h_hi) + (_dot_nt(w_hi, h_lo) + _dot_nt(w_lo, h_hi))


def _outproj(o_mix, w_out, stream, mod, gain, gain_ffn, w_router_t, row0):
    b, n_tok, _ = o_mix[0].shape
    tm = _token_tile(n_tok)
    x_specs, x_args = _stream_specs(stream, row0, tm)
    d = x_args[0].shape[-1]
    n_e = w_router_t.shape[0]
    o_spec = pl.BlockSpec((1, tm, MIXER_W), lambda b_, i: (b_, i, 0))
    full = lambda a: pl.BlockSpec(a.shape, lambda b_, i: (0, 0))
    tok_spec = pl.BlockSpec((1, tm, d), lambda b_, i: (b_, i, 0))
    return pl.pallas_call(
        functools.partial(_outproj_body, row0=row0),
        grid=(b, n_tok // tm),
        in_specs=[o_spec, o_spec, o_spec, o_spec, full(w_out)] + _mod_specs(2) + [full(gain)]
        + _mod_specs(3) + _mod_specs(4) + [full(gain_ffn), full(w_router_t)] + x_specs,
        out_specs=[tok_spec, tok_spec, pl.BlockSpec((1, n_e, tm), lambda b_, i: (b_, 0, i))],
        out_shape=[jax.ShapeDtypeStruct((b, n_tok, d), F32),
                   jax.ShapeDtypeStruct((b, n_tok, d), BF16),
                   jax.ShapeDtypeStruct((b, n_e, n_tok), F32)],
        compiler_params=_cparams(2),
        name="outproj",
    )(*o_mix, w_out, mod, mod, gain, mod, mod, mod, mod, gain_ffn, w_router_t, *x_args)


def _exclusive_cumsum(x, tri):
    off = jnp.zeros((x.shape[0], 1), F32)
    outs = []
    for c in range(x.shape[1] // LANES):
        xc = x[:, c * LANES:(c + 1) * LANES]
        inc = _dot(xc.astype(BF16), tri)
        outs.append(inc - xc + off)
        off = off + inc[:, LANES - 1:LANES]
    return jnp.concatenate(outs, axis=1)


def _select_body(lg_ref, pos_ref, aff_ref, *, segments):
    r = lax.broadcasted_iota(jnp.int32, (LANES, LANES), 0)
    c = lax.broadcasted_iota(jnp.int32, (LANES, LANES), 1)
    tri = jnp.where(r <= c, 1.0, 0.0).astype(BF16)
    n_b, n_e = lg_ref.shape[:2]
    for t0, t, cap, slot0 in segments:
        affs = []
        for j in range(n_b):
            lg = lg_ref[j, :, t0:t0 + t]
            e = jnp.exp(lg - jnp.max(lg, axis=0, keepdims=True))
            affs.append(e / jnp.sum(e, axis=0, keepdims=True))
        aff = jnp.concatenate(affs, axis=0)

        def count_above(thr):
            return jnp.sum(jnp.where(aff > thr, 1.0, 0.0), axis=1, keepdims=True)

        def unsettled(carry):
            return jnp.logical_and(carry[2] > 0.0, carry[3] < BISECT_MAX_STEPS)

        def bisect(carry):
            lo, hi, _, step = carry
            mid = 0.5 * (lo + hi)
            cnt = count_above(mid)
            new_lo = jnp.where(cnt >= cap, mid, lo)
            new_hi = jnp.where(cnt <= cap, mid, hi)
            moving = jnp.logical_and(new_lo < new_hi, jnp.logical_and(mid > lo, mid < hi))
            return new_lo, new_hi, jnp.sum(jnp.where(moving, 1.0, 0.0)), step + 1

        lo0 = jnp.full((aff.shape[0], 1), -1.0, F32)
        hi0 = jnp.max(aff, axis=1, keepdims=True)
        lo, hi, _, _ = lax.while_loop(unsettled, bisect, (lo0, hi0, jnp.float32(1.0), jnp.int32(0)))
        gt = aff > hi
        eq = jnp.logical_and(aff > lo, aff <= hi)
        need = cap - count_above(hi)
        eq_rank = _exclusive_cumsum(jnp.where(eq, 1.0, 0.0), tri)
        sel = jnp.logical_or(gt, jnp.logical_and(eq, eq_rank < need))
        slot = _exclusive_cumsum(jnp.where(sel, 1.0, 0.0), tri) + slot0
        pos = jnp.where(sel, slot, -1.0).astype(jnp.int32)
        for j in range(n_b):
            pos_ref[j, :, t0:t0 + t] = pos[j * n_e:(j + 1) * n_e]
            aff_ref[j, :, t0:t0 + t] = affs[j]


def _select(logits, segments):
    b, n_e, lt = logits.shape
    spec = pl.BlockSpec((b, n_e, lt), lambda i: (0, 0, 0))
    return pl.pallas_call(
        functools.partial(_select_body, segments=segments),
        grid=(1,),
        in_specs=[spec],
        out_specs=[spec, spec],
        out_shape=[jax.ShapeDtypeStruct((b, n_e, lt), jnp.int32),
                   jax.ShapeDtypeStruct((b, n_e, lt), F32)],
        compiler_params=_cparams(1),
        name="select",
    )(logits)


def _slot_hits(pos_row, n_slots):
    return lax.broadcasted_iota(jnp.int32, (n_slots, pos_row.shape[1]), 0) == pos_row


def _gather_body(pos_ref, h_ref, xe_ref):
    n_grp, n_slots = xe_ref.shape[1:3]
    e0 = pl.program_id(1) * n_grp
    onehot = jnp.concatenate(
        [jnp.where(_slot_hits(pos_ref[0, pl.ds(e0 + j, 1), :], n_slots), 1.0, 0.0).astype(BF16)
         for j in range(n_grp)], axis=0)
    xe = _dot(onehot, h_ref[0]).astype(BF16)
    xe_ref[0] = xe.reshape(n_grp, n_slots, xe.shape[1])


def _gather(pos, h, n_slots):
    b, n_e, n_tok = pos.shape
    d = h.shape[2]
    return pl.pallas_call(
        _gather_body,
        grid=(b, n_e // GATHER_EXPERTS),
        in_specs=[pl.BlockSpec((1, n_e, n_tok), lambda b_, g: (b_, 0, 0)),
                  pl.BlockSpec((1, n_tok, d), lambda b_, g: (b_, 0, 0))],
        out_specs=pl.BlockSpec((1, GATHER_EXPERTS, n_slots, d), lambda b_, g: (b_, g, 0, 0)),
        out_shape=jax.ShapeDtypeStruct((b, n_e, n_slots, d), BF16),
        compiler_params=_cparams(2),
        name="gather",
    )(pos, h)


def _experts_body(pos_ref, aff_ref, xe_ref, wg_ref, wu_ref, wd_ref, y_ref, wg_s, wu_s, wd_s):
    @pl.when(pl.program_id(1) == 0)
    def _():
        wg_s[...] = wg_ref[0, 0].astype(BF16)
        wu_s[...] = wu_ref[0, 0].astype(BF16)
        wd_s[...] = wd_ref[0, 0].astype(BF16)

    n_b, _, n_slots, d = xe_ref.shape
    e = pl.program_id(0)
    x = xe_ref[...].reshape(n_b * n_slots, d)
    gate = jnp.concatenate(
        [jnp.sum(jnp.where(_slot_hits(pos_ref[j, pl.ds(e, 1), :], n_slots),
                           aff_ref[j, pl.ds(e, 1), :], 0.0), axis=1, keepdims=True)
         for j in range(n_b)], axis=0)
    y = None
    for c0 in range(0, wg_s.shape[1], FFN_CHUNK):
        a = _dot(x, wg_s[:, c0:c0 + FFN_CHUNK])
        u = _dot(x, wu_s[:, c0:c0 + FFN_CHUNK])
        act = ((a * jax.nn.sigmoid(a)) * u).astype(BF16)
        t = _dot(act, wd_s[c0:c0 + FFN_CHUNK, :])
        y = t if y is None else y + t
    y_ref[...] = (y * gate).astype(BF16).reshape(n_b, 1, n_slots, d)


def _experts(pos, aff, xe, w_gate, w_up, w_down, layer):
    b, n_e, n_tok = pos.shape
    n_slots = xe.shape[2]
    d, f = w_gate.shape[2:]
    nb = FFN_BATCHES if b % FFN_BATCHES == 0 else 1
    row_spec = pl.BlockSpec((nb, n_e, n_tok), lambda e, g: (g, 0, 0))
    slot_spec = pl.BlockSpec((nb, 1, n_slots, d), lambda e, g: (g, e, 0, 0))
    return pl.pallas_call(
        _experts_body,
        grid=(n_e, b // nb),
        in_specs=[row_spec, row_spec, slot_spec,
                  pl.BlockSpec((1, 1, d, f), lambda e, g: (layer, e, 0, 0)),
                  pl.BlockSpec((1, 1, d, f), lambda e, g: (layer, e, 0, 0)),
                  pl.BlockSpec((1, 1, f, d), lambda e, g: (layer, e, 0, 0))],
        out_specs=slot_spec,
        out_shape=jax.ShapeDtypeStruct(xe.shape, BF16),
        scratch_shapes=[pltpu.VMEM((d, f), BF16), pltpu.VMEM((d, f), BF16), pltpu.VMEM((f, d), BF16)],
        compiler_params=_cparams(2),
        name="experts",
    )(pos, aff, xe, w_gate, w_up, w_down)


def _combine_body(pos_ref, y_ref, x_ref, gc_ref, gl_ref, g_ref, out_ref, *, row0):
    n_e, n_slots = y_ref.shape[1:3]
    pos = pos_ref[0]
    slot = lax.broadcasted_iota(jnp.int32, (n_slots, pos.shape[1]), 0)
    onehot = jnp.concatenate(
        [jnp.where(slot == pos[e:e + 1, :], 1.0, 0.0).astype(BF16) for e in range(n_e)], axis=0)
    acc = lax.dot_general(onehot, y_ref[0].reshape(n_e * n_slots, y_ref.shape[3]),
                          (((0,), (0,)), ((), ())), preferred_element_type=F32)
    ms = jnp.mean(acc * acc, axis=-1, keepdims=True)
    y = (acc * lax.rsqrt(ms + EPS)) * g_ref[...]
    out_ref[0] = x_ref[0] + _tile_mod(gc_ref, gl_ref, acc.shape[0], row0) * y


def _combine(pos, y, xs, mod, gain, row0):
    b, n_tok, d = xs.shape
    tm = _token_tile(n_tok)
    n_e, n_slots = y.shape[1:3]
    return pl.pallas_call(
        functools.partial(_combine_body, row0=row0),
        grid=(b, n_tok // tm),
        in_specs=[pl.BlockSpec((1, n_e, tm), lambda b_, i: (b_, 0, i)),
                  pl.BlockSpec((1, n_e, n_slots, d), lambda b_, i: (b_, 0, 0, 0)),
                  pl.BlockSpec((1, tm, d), lambda b_, i: (b_, i, 0))] + _mod_specs(5) + [
                  pl.BlockSpec(gain.shape, lambda b_, i: (0, 0))],
        out_specs=pl.BlockSpec((1, tm, d), lambda b_, i: (b_, i, 0)),
        out_shape=jax.ShapeDtypeStruct(xs.shape, F32),
        compiler_params=_cparams(2),
        name="combine",
    )(pos, y, xs, mod, mod, gain)


def kernel(x, c, ctx, c_ctx, w_ada, b_ada, g_pre_mix, g_post_mix, g_pre_ffn, g_post_ffn, w_in, w_out, q_gain_a, k_gain_a, lam_q1, lam_k1, lam_q2, lam_k2, subln_gain_b, sink_c, rpb_d, w_router, w_gate, w_up, w_down):
    b, n_lat, d = x.shape
    n_ctx = ctx.shape[1]
    depth = w_ada.shape[0]
    assert n_ctx == TOK_TILE and n_lat % max(ATTN_STEP_ROWS.values()) == 0 and d == D_MODEL and b + 1 <= 16

    stream = (ctx, x)
    cc = jnp.zeros((16, d), F32).at[:b].set(c).at[b].set(c_ctx)
    mod_all = _ada(cc, w_ada, b_ada)
    mod_all = jnp.stack([jnp.broadcast_to(mod_all[:, b:b + 1], (depth, b, 6 * d)), mod_all[:, :b]],
                        axis=2).reshape(depth, b, 2, 6, 1, d)
    ca, sa = _rope_lane_tables(n_ctx, n_lat, HEAD_DIM)
    cb, sb = _rope_lane_tables(n_ctx, n_lat, DIFF_DIM)
    w_perm_all = _permute_w_in(w_in)
    w_out_all = w_out.astype(BF16)
    w_router_t = jnp.swapaxes(w_router, 1, 2)
    nbr_bias = _neighbourhood_bias(rpb_d, n_lat // GRID_W)
    pair_gain = lambda g: jnp.stack([jnp.tile(g[:, 0::2], (1, 4)), jnp.tile(g[:, 1::2], (1, 4))], axis=1)
    gq_all, gk_all = pair_gain(q_gain_a), pair_gain(k_gain_a)
    row = lambda v: v.reshape(1, -1)

    for l in range(depth):
        with_ctx = l < depth - 1
        row0 = 0 if with_ctx else 1
        lam_init = 0.8 - 0.6 * math.exp(-0.3 * l)
        mod = mod_all[l]
        qkv = _proj(stream, mod, row(g_pre_mix[l]), w_perm_all[l], ca, sa, cb, sb, gq_all[l], gk_all[l])

        lam = (jnp.exp(jnp.sum(lam_q1[l] * lam_k1[l])) - jnp.exp(jnp.sum(lam_q2[l] * lam_k2[l]))
               + lam_init).reshape(1)
        o_mix = (
            _attention(qkv, 0, "global", with_ctx),
            _attention(qkv, 1, "diff", with_ctx, scalars=lam, lam_init=lam_init,
                       gain=row(jnp.tile(subln_gain_b[l], N_HEADS))),
            _attention(qkv, 2, "window", with_ctx, scalars=sink_c[l]),
            _attention(qkv, 3, "nbr", with_ctx, bias=nbr_bias, layer=l),
        )
        xs, h, logits = _outproj(o_mix, w_out_all[l], stream, mod, row(g_post_mix[l]),
                                 row(g_pre_ffn[l]), w_router_t[l], row0)
        cap_lat = CAPACITY_FACTOR * n_lat // N_EXPERTS
        cap_ctx = CAPACITY_FACTOR * n_ctx // N_EXPERTS
        if with_ctx:
            segments = ((0, n_ctx, cap_ctx, 0), (n_ctx, n_lat, cap_lat, cap_ctx))
            n_slots = cap_ctx + cap_lat
        else:
            segments = ((0, n_lat, cap_lat, 0),)
            n_slots = cap_lat
        pos, aff = _select(logits, segments)
        y = _experts(pos, aff, _gather(pos, h, n_slots), w_gate, w_up, w_down, l)
        stream = _combine(pos, y, xs, mod, row(g_post_ffn[l]), row0)
    return stream
```
